```python
import functools
import jax, jax.numpy as jnp
from jax import lax
import numpy as np

D_MODEL = 1024
BATCH = 16
SEQ = 256
DEPTH = 2
DEC_BATCH = 2
DEC_SEQ = 1024
PAST_LEN = 512

GRID_W = 64
CONV_W = D_MODEL
CONV_K = 3
NA_HEADS = 8
NA_HEAD_DIM = D_MODEL // NA_HEADS
NA_WIDTH = NA_HEADS * NA_HEAD_DIM
NA_MAX_ROWS = 8
NA_COLS = 16
COL_BLOCK = 16
COL_BAND = 2 * COL_BLOCK
FNET_GROUPS = 4
FNET_GROUP_DIM = D_MODEL // FNET_GROUPS
FNET_WIDTH = FNET_GROUPS * FNET_GROUP_DIM
N_BRANCH = 3
IN_SPLITS = (CONV_W, 2 * CONV_W, 3 * CONV_W,
             3 * CONV_W + NA_WIDTH, 3 * CONV_W + 2 * NA_WIDTH, 3 * CONV_W + 3 * NA_WIDTH,
             3 * CONV_W + 3 * NA_WIDTH + FNET_WIDTH)
IN_COLS = 3 * CONV_W + 3 * NA_WIDTH + FNET_WIDTH + N_BRANCH * D_MODEL
PEER_HEADS = 8
PEER_KEYS = 128
PEER_EXPERTS = PEER_KEYS * PEER_KEYS
PEER_KEY_DIM = 128
PEER_TOPK = 16
TOKEN_BLOCK = 128
Q_BLOCK = 128
N_MOD = 6
RMS_EPS = 1e-6
NEG_INF = -1e30

kernel_name = 'hybrid_natten_conv_fnet_peer_diffusion_step'


def rmsnorm(x, g):
    xf = x.astype(jnp.float32)
    y = xf * lax.rsqrt(jnp.mean(xf * xf, axis=-1, keepdims=True) + RMS_EPS)
    return (y * g.astype(jnp.float32)).astype(x.dtype)


def modulate(h, shift, scale):
    return h * (1 + scale) + shift


def short_conv(u, w):
    C = u.shape[-1]
    return lax.conv_general_dilated(u, w[:, None, :].astype(u.dtype), window_strides=(1,),
                                    padding=((CONV_K // 2, CONV_K // 2),),
                                    dimension_numbers=('NWC', 'WIO', 'NWC'),
                                    feature_group_count=C)


def fourier_mix(u):
    B, S, _ = u.shape
    ug = u.reshape(B, S, FNET_GROUPS, FNET_GROUP_DIM).astype(jnp.float32)
    f = jnp.fft.fft2(ug, axes=(1, 3), norm='ortho').real
    return f.reshape(B, S, FNET_WIDTH).astype(u.dtype)


def context_attention(q, k, v):
    B, S, H, dh = q.shape
    nb = S // Q_BLOCK
    scale = dh ** -0.5
    kf = k.astype(jnp.float32)
    vf = v.astype(jnp.float32)
    qb = q.astype(jnp.float32).reshape(B, nb, Q_BLOCK, H, dh).transpose(1, 0, 2, 3, 4)

    def one_block(qi):
        s = jnp.einsum('bqhd,bkhd->bhqk', qi, kf) * scale
        p = jax.nn.softmax(s, axis=-1)
        return jnp.einsum('bhqk,bkhd->bqhd', p, vf)

    o = lax.map(one_block, qb)
    return o.transpose(1, 0, 2, 3, 4).reshape(B, S, H * dh).astype(q.dtype)


def neighbourhood_attention(q, k, v, k_ctx, v_ctx, rpb):
    B, S, H, dh = q.shape
    rows = S // GRID_W
    kr = min(NA_MAX_ROWS, rows)
    ncb = GRID_W // COL_BLOCK
    scale = dh ** -0.5
    r = jnp.arange(rows)
    r0 = jnp.clip(r - kr // 2, 0, rows - kr)
    row_idx = r0[:, None] + jnp.arange(kr)
    j = jnp.arange(ncb)
    b0 = jnp.clip(j * COL_BLOCK - COL_BLOCK // 2, 0, GRID_W - COL_BAND)
    col_idx = b0[:, None] + jnp.arange(COL_BAND)
    qcol = j[:, None] * COL_BLOCK + jnp.arange(COL_BLOCK)
    c0 = jnp.clip(qcol - NA_COLS // 2, 0, GRID_W - NA_COLS)
    kcol = col_idx[:, None, :]
    in_win = (kcol >= c0[..., None]) & (kcol < c0[..., None] + NA_COLS)
    dr = row_idx - r[:, None] + (NA_MAX_ROWS - 1)
    dc = jnp.clip(kcol - qcol[..., None] + (NA_COLS - 1), 0, 2 * NA_COLS - 2)
    bias = rpb[:, dr[:, None, None, :, None], dc[None, :, :, None, :]]
    bias = jnp.transpose(bias, (1, 2, 3, 0, 4, 5)).astype(jnp.float32)
    bias = jnp.where(in_win[None, :, :, None, None, :], bias, NEG_INF)

    qg = q.astype(jnp.float32).reshape(B, rows, ncb, COL_BLOCK, H, dh)
    gather = lambda t: t.astype(jnp.float32).reshape(B, rows, GRID_W, H, dh)[
        :, row_idx[:, :, None, None], col_idx[None, None, :, :]]
    kg = gather(k)
    vg = gather(v)
    kc = k_ctx.astype(jnp.float32)
    vc = v_ctx.astype(jnp.float32)
    s_loc = jnp.einsum('brjqhd,brkjchd->brjqhkc', qg, kg) * scale + bias[None]
    s_ctx = jnp.einsum('brjqhd,blhd->brjqhl', qg, kc) * scale
    n_loc = kr * COL_BAND
    s = jnp.concatenate([s_loc.reshape(s_loc.shape[:5] + (n_loc,)), s_ctx], axis=-1)
    p = jax.nn.softmax(s, axis=-1)
    p_loc = p[..., :n_loc].reshape(s_loc.shape)
    p_ctx = p[..., n_loc:]
    o = (jnp.einsum('brjqhkc,brkjchd->brjqhd', p_loc, vg)
         + jnp.einsum('brjqhl,blhd->brjqhd', p_ctx, vc))
    return o.reshape(B, S, H * dh).astype(q.dtype)


def peer_ffn(h, w_pq, sub_keys, peer_u, peer_v):
    B, S, D = h.shape
    T = B * S
    hf = h.reshape(T, D)
    q = (hf @ w_pq).reshape(T, PEER_HEADS, 2, PEER_KEY_DIM)
    s = jnp.einsum('thpd,hpnd->thpn', q.astype(jnp.float32), sub_keys.astype(jnp.float32))
    s1, i1 = lax.top_k(s[:, :, 0], PEER_TOPK)
    s2, i2 = lax.top_k(s[:, :, 1], PEER_TOPK)
    cand_s = (s1[..., :, None] + s2[..., None, :]).reshape(T, PEER_HEADS, PEER_TOPK * PEER_TOPK)
    cand_i = (i1[..., :, None] * PEER_KEYS + i2[..., None, :]).reshape(T, PEER_HEADS, PEER_TOPK * PEER_TOPK)
    top_s, pos = lax.top_k(cand_s, PEER_TOPK)
    e_idx = jnp.take_along_axis(cand_i, pos, axis=-1).reshape(T, PEER_HEADS * PEER_TOPK)
    gate = jax.nn.softmax(top_s, axis=-1).reshape(T, PEER_HEADS * PEER_TOPK)
    nb = T // TOKEN_BLOCK
    HK = PEER_HEADS * PEER_TOPK

    def one_block(args):
        xb, ib, gb = args
        ub = jnp.take(peer_u, ib, axis=0).astype(jnp.float32)
        a = jnp.einsum('tkd,td->tk', ub, xb.astype(jnp.float32))
        a = jax.nn.gelu(a, approximate=False) * gb
        vb = jnp.take(peer_v, ib, axis=0).astype(jnp.float32)
        return jnp.einsum('tk,tkd->td', a, vb).astype(xb.dtype)

    out = lax.map(one_block, (hf.reshape(nb, TOKEN_BLOCK, D),
                              e_idx.reshape(nb, TOKEN_BLOCK, HK),
                              gate.reshape(nb, TOKEN_BLOCK, HK)))
    return out.reshape(B, S, D)


def trunk_layer(x, mod, lp, attend):
    (w_in, conv_w, w_conv_out, w_attn_out, w_four_out, w_o,
     g_mix, g_ffn, w_pq, sub_keys, peer_u, peer_v) = lp
    shift1, scale1, gate1, shift2, scale2, gate2 = mod
    B, S, _ = x.shape
    h = modulate(rmsnorm(x, g_mix), shift1, scale1)
    z = h @ w_in
    cb, cc, cx, q, k, v, f, gl = jnp.split(z, IN_SPLITS, axis=-1)
    y_conv = (cb * short_conv(cc * cx, conv_w)) @ w_conv_out
    q = q.reshape(B, S, NA_HEADS, NA_HEAD_DIM)
    k = k.reshape(B, S, NA_HEADS, NA_HEAD_DIM)
    v = v.reshape(B, S, NA_HEADS, NA_HEAD_DIM)
    y_att = attend(q, k, v) @ w_attn_out
    y_four = fourier_mix(f) @ w_four_out
    g = jax.nn.sigmoid(gl.astype(jnp.float32)).astype(x.dtype).reshape(B, S, N_BRANCH, D_MODEL)
    merged = g[:, :, 0] * y_conv + g[:, :, 1] * y_att + g[:, :, 2] * y_four
    x = x + gate1 * (merged @ w_o)
    h = modulate(rmsnorm(x, g_ffn), shift2, scale2)
    x = x + gate2 * peer_ffn(h, w_pq, sub_keys, peer_u, peer_v)
    return x, k, v


def setup_inputs(seed: int = 0) -> dict:
    key = jax.random.key(seed)
    ks = jax.random.split(key, 24)
    nrm = lambda k, shape, s: jax.random.normal(k, shape, jnp.float32) * s
    L = DEPTH
    return {
        'x_prompt': nrm(ks[0], (BATCH, SEQ, D_MODEL), 1.0),
        'x_sample': nrm(ks[1], (DEC_BATCH, DEC_SEQ, D_MODEL), 1.0),
        'cache_k': nrm(ks[2], (DEC_BATCH, DEPTH, PAST_LEN, NA_HEADS, NA_HEAD_DIM), 1.0),
        'cache_v': nrm(ks[3], (DEC_BATCH, DEPTH, PAST_LEN, NA_HEADS, NA_HEAD_DIM), 1.0),
        'c': nrm(ks[4], (DEC_BATCH, D_MODEL), 1.0),
        'c_ctx': nrm(ks[5], (D_MODEL,), 1.0),
        'w_in': nrm(ks[6], (L, D_MODEL, IN_COLS), D_MODEL ** -0.5),
        'conv_w': nrm(ks[7], (L, CONV_K, CONV_W), CONV_K ** -0.5),
        'w_conv_out': nrm(ks[8], (L, CONV_W, D_MODEL), CONV_W ** -0.5),
        'rpb': nrm(ks[9], (L, NA_HEADS, 2 * NA_MAX_ROWS - 1, 2 * NA_COLS - 1), 0.1),
        'w_attn_out': nrm(ks[10], (L, NA_WIDTH, D_MODEL), NA_WIDTH ** -0.5),
        'w_four_out': nrm(ks[11], (L, FNET_WIDTH, D_MODEL), FNET_WIDTH ** -0.5),
        'w_o': nrm(ks[12], (L, D_MODEL, D_MODEL), D_MODEL ** -0.5),
        'g_mix': 1.0 + nrm(ks[13], (L, D_MODEL), 0.02),
        'g_ffn': 1.0 + nrm(ks[14], (L, D_MODEL), 0.02),
        'w_mod': nrm(ks[15], (L, D_MODEL, N_MOD * D_MODEL), D_MODEL ** -0.5),
        'b_mod': nrm(ks[16], (L, N_MOD * D_MODEL), 0.02),
        'w_pq': nrm(ks[17], (L, D_MODEL, PEER_HEADS * 2 * PEER_KEY_DIM), D_MODEL ** -0.5),
        'sub_keys': nrm(ks[18], (L, PEER_HEADS, 2, PEER_KEYS, PEER_KEY_DIM), PEER_KEY_DIM ** -0.5),
        'peer_u': nrm(ks[19], (L, PEER_EXPERTS, D_MODEL), D_MODEL ** -0.5),
        'peer_v': nrm(ks[20], (L, PEER_EXPERTS, D_MODEL), (PEER_HEADS * PEER_TOPK) ** -0.5),
        'g_final': 1.0 + nrm(ks[21], (D_MODEL,), 0.02),
    }


def reference(x_prompt, x_sample, cache_k, cache_v, c, c_ctx, w_in, conv_w, w_conv_out, rpb,
              w_attn_out, w_four_out, w_o, g_mix, g_ffn, w_mod, b_mod, w_pq, sub_keys,
              peer_u, peer_v, g_final):
    xp = x_prompt
    xs = x_sample
    ctx_k, ctx_v = [], []
    for l in range(DEPTH):
        lp = (w_in[l], conv_w[l], w_conv_out[l], w_attn_out[l], w_four_out[l], w_o[l],
              g_mix[l], g_ffn[l], w_pq[l], sub_keys[l], peer_u[l], peer_v[l])
        mod_ctx = (jax.nn.silu(c_ctx) @ w_mod[l] + b_mod[l]).reshape(N_MOD, 1, 1, D_MODEL)
        mod_lat = (jax.nn.silu(c) @ w_mod[l] + b_mod[l]).reshape(-1, N_MOD, D_MODEL)
        mod_lat = mod_lat.transpose(1, 0, 2)[:, :, None, :]
        xp, k_l, v_l = trunk_layer(xp, mod_ctx, lp, context_attention)
        ctx_k.append(k_l)
        ctx_v.append(v_l)
        attend_lat = functools.partial(neighbourhood_attention, k_ctx=cache_k[:, l],
                                       v_ctx=cache_v[:, l], rpb=rpb[l])
        xs, _, _ = trunk_layer(xs, mod_lat, lp, attend_lat)
    y_prompt = rmsnorm(xp, g_final)
    y_sample = rmsnorm(xs, g_final)
    new_cache_k = jnp.stack(ctx_k, axis=1)
    new_cache_v = jnp.stack(ctx_v, axis=1)
    return (y_prompt, y_sample, new_cache_k, new_cache_v)
```

```python
import functools
import math

import numpy as np
import jax
import jax.numpy as jnp
from jax import lax
from jax.experimental import pallas as pl
from jax.experimental.pallas import tpu as pltpu

D_MODEL = 1024
BATCH = 16
SEQ = 256
DEPTH = 2
DEC_BATCH = 2
DEC_SEQ = 1024
PAST_LEN = 512
GRID_W = 64
CONV_K = 3
NA_HEADS = 8
NA_HEAD_DIM = D_MODEL // NA_HEADS
NA_MAX_ROWS = 8
NA_COLS = 16
FNET_GROUPS = 4
FNET_GROUP_DIM = D_MODEL // FNET_GROUPS
N_BRANCH = 3
IN_COLS = 10 * D_MODEL
PEER_HEADS = 8
PEER_KEYS = 128
PEER_EXPERTS = PEER_KEYS * PEER_KEYS
PEER_KEY_DIM = 128
PEER_TOPK = 16
N_MOD = 6
RMS_EPS = 1e-6
NEG_INF = -1e30

T_CTX = BATCH * SEQ
T_LAT = DEC_BATCH * DEC_SEQ
T_ALL = T_CTX + T_LAT

COL_CB, COL_CC, COL_CX, COL_Q, COL_K, COL_V, COL_F, COL_G0, COL_G1, COL_G2 = range(10)
MOD_SHIFT1, MOD_SCALE1, MOD_GATE1, MOD_SHIFT2, MOD_SCALE2, MOD_GATE2 = range(6)

TOKEN_BLOCK = 1024
N_TOKEN_BLOCKS = T_ALL // TOKEN_BLOCK
N_CTX_BLOCKS = T_CTX // TOKEN_BLOCK
MERGE_BLOCK = 512
ROUTE_BLOCK = 256
EXPERT_BLOCK = 512
I1_PER_BLOCK = EXPERT_BLOCK // PEER_KEYS
NOT_SELECTED = 127.0
VMEM_LIMIT = 56 * 1024 * 1024
BF16_TILE_ROWS = 16

F32 = jnp.float32
BF16 = jnp.bfloat16
NT_DIMS = (((1,), (1,)), ((), ()))


def _params(*semantics):
    return pltpu.CompilerParams(dimension_semantics=semantics, vmem_limit_bytes=VMEM_LIMIT)


def _mod_kernel(c_ref, w_ref, b_ref, o_ref):
    c = c_ref[...]
    s = c * jax.nn.sigmoid(c)
    o_ref[0] = jnp.dot(s, w_ref[0], preferred_element_type=F32,
                       precision=lax.Precision.HIGHEST) + b_ref[0]


def _modulation(cvec, w_mod, b_mod):
    tn = 1536
    ncol = N_MOD * D_MODEL
    return pl.pallas_call(
        _mod_kernel,
        grid=(DEPTH, ncol // tn),
        in_specs=[pl.BlockSpec((8, D_MODEL), lambda l, n: (0, 0)),
                  pl.BlockSpec((1, D_MODEL, tn), lambda l, n: (l, 0, n)),
                  pl.BlockSpec((1, 1, tn), lambda l, n: (l, 0, n))],
        out_specs=pl.BlockSpec((1, 8, tn), lambda l, n: (l, 0, n)),
        out_shape=jax.ShapeDtypeStruct((DEPTH, 8, ncol), F32),
        compiler_params=_params("arbitrary", "arbitrary"),
        name="adaln_table",
    )(cvec, w_mod, b_mod.reshape(DEPTH, 1, ncol))


def _rms(x, g):
    return x * lax.rsqrt(jnp.mean(x * x, axis=-1, keepdims=True) + RMS_EPS) * g


def _norm_mod_kernel(x_ref, g_ref, m_ref, o_ref):
    y = _rms(x_ref[...], g_ref[...])
    h = y * (1.0 + m_ref[MOD_SCALE1:MOD_SCALE1 + 1, :]) + m_ref[MOD_SHIFT1:MOD_SHIFT1 + 1, :]
    o_ref[...] = h.astype(BF16)


def _norm_mod(x, g, mod_tok):
    return pl.pallas_call(
        _norm_mod_kernel,
        grid=(N_TOKEN_BLOCKS,),
        in_specs=[pl.BlockSpec((TOKEN_BLOCK, D_MODEL), lambda i: (i, 0)),
                  pl.BlockSpec((1, D_MODEL), lambda i: (0, 0)),
                  pl.BlockSpec((None, N_MOD, D_MODEL), lambda i: (i, 0, 0))],
        out_specs=pl.BlockSpec((TOKEN_BLOCK, D_MODEL), lambda i: (i, 0)),
        out_shape=jax.ShapeDtypeStruct((T_ALL, D_MODEL), BF16),
        compiler_params=_params("arbitrary"),
        name="norm_modulate",
    )(x, g.reshape(1, D_MODEL), mod_tok)


def _final_norm_kernel(x_ref, g_ref, o_ref):
    o_ref[...] = _rms(x_ref[...], g_ref[...])


def _final_norm(x, g, first_block, n_blocks):
    return pl.pallas_call(
        _final_norm_kernel,
        grid=(n_blocks,),
        in_specs=[pl.BlockSpec((TOKEN_BLOCK, D_MODEL), lambda i: (i + first_block, 0)),
                  pl.BlockSpec((1, D_MODEL), lambda i: (0, 0))],
        out_specs=pl.BlockSpec((TOKEN_BLOCK, D_MODEL), lambda i: (i, 0)),
        out_shape=jax.ShapeDtypeStruct((n_blocks * TOKEN_BLOCK, D_MODEL), F32),
        compiler_params=_params("arbitrary"),
        name="final_norm",
    )(x, g.reshape(1, D_MODEL))


def _in_proj_kernel(h_ref, w_ref, *rest):
    z_ref, k_ref, v_ref, wb_scr = rest[-4:]
    n = pl.program_id(0)
    i = pl.program_id(1)

    @pl.when(i == 0)
    def _():
        wb_scr[...] = w_ref[...].astype(BF16)

    z = jnp.dot(h_ref[...], wb_scr[...], preferred_element_type=F32)
    z_ref[...] = z.astype(BF16)
    per_block = TOKEN_BLOCK // SEQ

    @pl.when((n == COL_K) & (i < N_CTX_BLOCKS))
    def _():
        k_ref[...] = z.reshape(per_block, SEQ, D_MODEL)

    @pl.when((n == COL_V) & (i < N_CTX_BLOCKS))
    def _():
        v_ref[...] = z.reshape(per_block, SEQ, D_MODEL)


def _cache_block_index(col, layer, n, i):
    last = N_CTX_BLOCKS - 1
    blk = jnp.where(n < col, 0, jnp.where(n > col, last, jnp.minimum(i, last)))
    return (blk, layer, 0, 0)


def _in_proj(h, w_in, layer, cache_k, cache_v):
    per_block = TOKEN_BLOCK // SEQ
    cache_shape = jax.ShapeDtypeStruct((BATCH, DEPTH, SEQ, D_MODEL), F32)
    cache_spec = lambda col: pl.BlockSpec(
        (per_block, None, SEQ, D_MODEL), functools.partial(_cache_block_index, col, layer))
    in_specs = [pl.BlockSpec((TOKEN_BLOCK, D_MODEL), lambda n, i: (i, 0)),
                pl.BlockSpec((None, D_MODEL, D_MODEL), lambda n, i: (layer, 0, n))]
    args = [h, w_in]
    aliases = {}
    if cache_k is not None:
        in_specs += [pl.BlockSpec(memory_space=pl.ANY)] * 2
        args += [cache_k, cache_v]
        aliases = {2: 1, 3: 2}
    return pl.pallas_call(
        _in_proj_kernel,
        grid=(IN_COLS // D_MODEL, N_TOKEN_BLOCKS),
        in_specs=in_specs,
        out_specs=[pl.BlockSpec((TOKEN_BLOCK, D_MODEL), lambda n, i: (i, n)),
                   cache_spec(COL_K), cache_spec(COL_V)],
        out_shape=[jax.ShapeDtypeStruct((T_ALL, IN_COLS), BF16), cache_shape, cache_shape],
        scratch_shapes=[pltpu.VMEM((D_MODEL, D_MODEL), BF16)],
        input_output_aliases=aliases,
        compiler_params=_params("arbitrary", "arbitrary"),
        name="in_proj",
    )(*args)


def _conv_kernel(cb_ref, cc_ref, cx_ref, w_ref, o_ref):
    i = pl.program_id(0)
    u = cc_ref[...].astype(F32) * cx_ref[...].astype(F32)
    seq = jnp.where(i < N_CTX_BLOCKS, SEQ, DEC_SEQ)
    pos = lax.broadcasted_iota(jnp.int32, (TOKEN_BLOCK, 1), 0) & (seq - 1)
    prev = jnp.where(pos == 0, 0.0, pltpu.roll(u, 1, 0))
    nxt = jnp.where(pos == seq - 1, 0.0, pltpu.roll(u, TOKEN_BLOCK - 1, 0))
    y = w_ref[0:1, :] * prev + w_ref[1:2, :] * u + w_ref[2:3, :] * nxt
    o_ref[...] = (cb_ref[...].astype(F32) * y).astype(BF16)


def _short_conv(z, conv_w):
    col = lambda c: pl.BlockSpec((TOKEN_BLOCK, D_MODEL), lambda i: (i, c))
    return pl.pallas_call(
        _conv_kernel,
        grid=(N_TOKEN_BLOCKS,),
        in_specs=[col(COL_CB), col(COL_CC), col(COL_CX),
                  pl.BlockSpec((CONV_K, D_MODEL), lambda i: (0, 0))],
        out_specs=pl.BlockSpec((TOKEN_BLOCK, D_MODEL), lambda i: (i, 0)),
        out_shape=jax.ShapeDtypeStruct((T_ALL, D_MODEL), BF16),
        compiler_params=_params("arbitrary"),
        name="short_conv",
    )(z, z, z, conv_w)


ATT_SCALE = NA_HEAD_DIM ** -0.5


def _ctx_attn_kernel(q_ref, k_ref, v_ref, o_ref):
    for h in range(NA_HEADS):
        sl = slice(h * NA_HEAD_DIM, (h + 1) * NA_HEAD_DIM)
        s = lax.dot_general(q_ref[:, sl], k_ref[:, sl], NT_DIMS,
                            preferred_element_type=F32) * ATT_SCALE
        p = jnp.exp(s - jnp.max(s, axis=-1, keepdims=True))
        o = jnp.dot(p.astype(BF16), v_ref[:, sl], preferred_element_type=F32)
        o_ref[:, sl] = (o / jnp.sum(p, axis=-1, keepdims=True)).astype(BF16)


def _ctx_attention(z):
    col = lambda c: pl.BlockSpec((SEQ, D_MODEL), lambda b: (b, c))
    return pl.pallas_call(
        _ctx_attn_kernel,
        grid=(BATCH,),
        in_specs=[col(COL_Q), col(COL_K), col(COL_V)],
        out_specs=pl.BlockSpec((SEQ, D_MODEL), lambda b: (b, 0)),
        out_shape=jax.ShapeDtypeStruct((T_ALL, D_MODEL), BF16),
        compiler_params=_params("arbitrary"),
        name="ctx_attention",
    )(z, z, z)


NA_GRID_ROWS = DEC_SEQ // GRID_W
NA_WIN_ROWS = min(NA_MAX_ROWS, NA_GRID_ROWS)


def _na_row_groups():
    groups = []
    for r in range(NA_GRID_ROWS):
        r0 = min(max(r - NA_WIN_ROWS // 2, 0), NA_GRID_ROWS - NA_WIN_ROWS)
        if groups and groups[-1][0] == r0:
            groups[-1][1].append(r)
        else:
            groups.append((r0, [r]))
    return groups


def _na_kernel(q_ref, k_ref, v_ref, ck_ref, cv_ref, b_ref, att_in_ref, o_ref):
    del att_in_ref
    ck = ck_ref[...].astype(BF16)
    cv = cv_ref[...].astype(BF16)
    for r0, q_rows in _na_row_groups():
        rows = slice(q_rows[0] * GRID_W, (q_rows[-1] + 1) * GRID_W)
        win = slice(r0 * GRID_W, (r0 + NA_WIN_ROWS) * GRID_W)
        q = q_ref[rows, :]
        bias = jnp.concatenate([b_ref[r0 - r + NA_MAX_ROWS - 1] for r in q_rows], axis=0)
        s_loc = lax.dot_general(q, k_ref[win, :], NT_DIMS, preferred_element_type=F32) * ATT_SCALE + bias
        s_ctx = lax.dot_general(q, ck, NT_DIMS, preferred_element_type=F32) * ATT_SCALE
        m = jnp.maximum(jnp.max(s_loc, axis=-1, keepdims=True), jnp.max(s_ctx, axis=-1, keepdims=True))
        p_loc = jnp.exp(s_loc - m)
        p_ctx = jnp.exp(s_ctx - m)
        den = jnp.sum(p_loc, axis=-1, keepdims=True) + jnp.sum(p_ctx, axis=-1, keepdims=True)
        o = (jnp.dot(p_loc.astype(BF16), v_ref[win, :], preferred_element_type=F32)
             + jnp.dot(p_ctx.astype(BF16), cv, preferred_element_type=F32))
        o_ref[rows, :] = (o / den).astype(BF16)


def _na_bias(rpb_l):
    c = np.arange(GRID_W)
    c0 = np.clip(c - NA_COLS // 2, 0, GRID_W - NA_COLS)
    in_cols = (c[None, :] >= c0[:, None]) & (c[None, :] < c0[:, None] + NA_COLS)
    dc = c[None, :] - c[:, None] + (NA_COLS - 1)
    pick = (dc[None] == np.arange(2 * NA_COLS - 1)[:, None, None]) & in_cols[None]
    t = jnp.einsum('hdj,jqk->hdqk', rpb_l.astype(F32), jnp.asarray(pick, F32),
                   precision=lax.Precision.HIGHEST)
    t = jnp.where(in_cols[None, None], t, NEG_INF)
    slabs = [t[:, d:d + NA_WIN_ROWS].transpose(0, 2, 1, 3).reshape(NA_HEADS, GRID_W, NA_WIN_ROWS * GRID_W)
             for d in range(NA_WIN_ROWS)]
    return jnp.stack(slabs, axis=1)


def _na_attention(z, cache_k_l, cache_v_l, bias, att, layer):
    lat0 = T_CTX // DEC_SEQ
    col = lambda c: pl.BlockSpec((DEC_SEQ, NA_HEAD_DIM), lambda h, b: (lat0 + b, c * NA_HEADS + h))
    cache = pl.BlockSpec((None, None, PAST_LEN, NA_HEAD_DIM), lambda h, b: (b, layer, 0, h))
    return pl.pallas_call(
        _na_kernel,
        grid=(NA_HEADS, DEC_BATCH),
        in_specs=[col(COL_Q), col(COL_K), col(COL_V), cache, cache,
                  pl.BlockSpec((None, NA_WIN_ROWS, GRID_W, NA_WIN_ROWS * GRID_W), lambda h, b: (h, 0, 0, 0)),
                  pl.BlockSpec(memory_space=pl.ANY)],
        out_specs=pl.BlockSpec((DEC_SEQ, NA_HEAD_DIM), lambda h, b: (lat0 + b, h)),
        out_shape=jax.ShapeDtypeStruct((T_ALL, D_MODEL), BF16),
        input_output_aliases={6: 0},
        compiler_params=_params("arbitrary", "arbitrary"),
        name="na_attention",
    )(z, z, z, cache_k_l, cache_v_l, bias, att)


def _dft_matrices(n):
    j = np.arange(n)
    ang = 2.0 * np.pi * ((j[:, None] * j[None, :]) % n) / n
    return np.cos(ang) / math.sqrt(n), np.sin(ang) / math.sqrt(n)


def _fourier_kernel(cs_ref, ss_ref, f_ref, w2_ref, *rest):
    o_ref = rest[-1]
    f = f_ref[...]
    cu = jnp.dot(cs_ref[...], f, preferred_element_type=F32).astype(BF16)
    su = jnp.dot(ss_ref[...], f, preferred_element_type=F32).astype(BF16)
    for g in range(FNET_GROUPS):
        sl = slice(g * FNET_GROUP_DIM, (g + 1) * FNET_GROUP_DIM)
        lhs = jnp.concatenate([cu[:, sl], su[:, sl]], axis=1)
        o_ref[:, sl] = jnp.dot(lhs, w2_ref[...], preferred_element_type=F32).astype(BF16)


FOURIER_ROWS = 256


def _fourier(z, seq, n_batch, first_row, prev):
    cs, ss = _dft_matrices(seq)
    cc, sc = _dft_matrices(FNET_GROUP_DIM)
    w2 = jnp.asarray(np.concatenate([cc, -sc], axis=0), F32).astype(BF16)
    rows = FOURIER_ROWS
    nr = seq // rows
    in_specs = [pl.BlockSpec((rows, seq), lambda b, r: (r, 0)),
                pl.BlockSpec((rows, seq), lambda b, r: (r, 0)),
                pl.BlockSpec((seq, D_MODEL), lambda b, r: (first_row // seq + b, COL_F)),
                pl.BlockSpec((2 * FNET_GROUP_DIM, FNET_GROUP_DIM), lambda b, r: (0, 0))]
    args = [jnp.asarray(cs, F32).astype(BF16), jnp.asarray(ss, F32).astype(BF16), z, w2]
    aliases = {}
    if prev is not None:
        in_specs.append(pl.BlockSpec(memory_space=pl.ANY))
        args.append(prev)
        aliases = {4: 0}
    return pl.pallas_call(
        _fourier_kernel,
        grid=(n_batch, nr),
        in_specs=in_specs,
        out_specs=pl.BlockSpec((rows, D_MODEL), lambda b, r: (first_row // rows + b * nr + r, 0)),
        out_shape=jax.ShapeDtypeStruct((T_ALL, D_MODEL), BF16),
        input_output_aliases=aliases,
        compiler_params=_params("arbitrary", "arbitrary"),
        name="fourier_mix",
    )(*args)


def _merge_kernel(uc_ref, at_ref, fo_ref, g0_ref, g1_ref, g2_ref, x_ref, wc_ref, wa_ref, wf_ref,
                  wo_ref, m_ref, gf_ref, xo_ref, h2_ref, h2t_ref):
    yc = jnp.dot(uc_ref[...], wc_ref[...], preferred_element_type=F32)
    ya = jnp.dot(at_ref[...], wa_ref[...], preferred_element_type=F32)
    yf = jnp.dot(fo_ref[...], wf_ref[...], preferred_element_type=F32)
    gate = lambda r: jax.nn.sigmoid(r[...].astype(F32))
    merged = gate(g0_ref) * yc + gate(g1_ref) * ya + gate(g2_ref) * yf
    y = jnp.dot(merged.astype(BF16), wo_ref[...], preferred_element_type=F32)
    x = x_ref[...] + m_ref[MOD_GATE1:MOD_GATE1 + 1, :] * y
    xo_ref[...] = x
    hn = _rms(x, gf_ref[...])
    h2 = hn * (1.0 + m_ref[MOD_SCALE2:MOD_SCALE2 + 1, :]) + m_ref[MOD_SHIFT2:MOD_SHIFT2 + 1, :]
    h2_ref[...] = h2.astype(BF16)
    h2t_ref[...] = h2.T.astype(BF16)


def _merge(uconv, att, four, z, x, wc, wa, wf, wo, mod_tok, g_ffn):
    per = TOKEN_BLOCK // MERGE_BLOCK
    row = pl.BlockSpec((MERGE_BLOCK, D_MODEL), lambda i: (i, 0))
    col = lambda c: pl.BlockSpec((MERGE_BLOCK, D_MODEL), lambda i: (i, c))
    wspec = pl.BlockSpec((D_MODEL, D_MODEL), lambda i: (0, 0))
    return pl.pallas_call(
        _merge_kernel,
        grid=(T_ALL // MERGE_BLOCK,),
        in_specs=[row, row, row, col(COL_G0), col(COL_G1), col(COL_G2), row,
                  wspec, wspec, wspec, wspec,
                  pl.BlockSpec((None, N_MOD, D_MODEL), lambda i: (i // per, 0, 0)),
                  pl.BlockSpec((1, D_MODEL), lambda i: (0, 0))],
        out_specs=[row, row, pl.BlockSpec((D_MODEL, MERGE_BLOCK), lambda i: (0, i))],
        out_shape=[jax.ShapeDtypeStruct((T_ALL, D_MODEL), F32),
                   jax.ShapeDtypeStruct((T_ALL, D_MODEL), BF16),
                   jax.ShapeDtypeStruct((D_MODEL, T_ALL), BF16)],
        compiler_params=_params("arbitrary"),
        name="branch_merge",
    )(uconv, att, four, z, z, z, x, wc, wa, wf, wo, mod_tok, g_ffn.reshape(1, D_MODEL))


def _candidate_cells():
    return [(a, b) for a in range(PEER_TOPK) for b in range(PEER_TOPK) if (a + 1) * (b + 1) <= PEER_TOPK]


def _extract_topk(p, orig_scr, s_scr, rank_scr, val_scr, break_ties):
    shape = (PEER_HEADS, PEER_KEYS, ROUTE_BLOCK)
    s_scr[...] = orig_scr[p]
    rank_scr[p] = jnp.full(shape, NOT_SELECTED, F32)
    for k in range(PEER_TOPK):
        cur = s_scr[...]
        m = jnp.max(cur, axis=1, keepdims=True)
        sel = cur == m
        if break_ties:
            key_iota = lax.broadcasted_iota(jnp.int32, shape, 1).astype(F32)
            first = jnp.min(jnp.where(sel, key_iota, float(PEER_KEYS)), axis=1, keepdims=True)
            sel = key_iota == first
        s_scr[...] = jnp.where(sel, -jnp.inf, cur)
        rank_scr[p] = jnp.where(sel, float(k), rank_scr[p])
        for h in range(PEER_HEADS):
            val_scr[p, k, h:h + 1, :] = m[h]


def _route_kernel(h_ref, wpq_ref, keys_ref, r2_ref, e2_ref, n1_ref, c1_ref,
                  s_scr, orig_scr, rank_scr, val_scr, n_scr):
    tb = ROUTE_BLOCK
    q = jnp.dot(h_ref[...], wpq_ref[...], preferred_element_type=F32).astype(BF16)

    for p in range(2):
        qp = q[:, p * PEER_HEADS * PEER_KEY_DIM:(p + 1) * PEER_HEADS * PEER_KEY_DIM]
        s = lax.dot_general(keys_ref[p], qp, NT_DIMS, preferred_element_type=F32)
        orig_scr[p] = s.reshape(PEER_HEADS, PEER_KEYS, tb)
    off = 0.0
    for p in range(2):
        _extract_topk(p, orig_scr, s_scr, rank_scr, val_scr, break_ties=False)
        taken = jnp.sum(jnp.where(rank_scr[p] < float(PEER_TOPK), 1.0, 0.0), axis=1)
        off = jnp.maximum(off, jnp.max(jnp.abs(taken - float(PEER_TOPK))))

    @pl.when(off > 0.0)
    def _():
        for p in range(2):
            _extract_topk(p, orig_scr, s_scr, rank_scr, val_scr, break_ties=True)

    v1 = [val_scr[0, a] for a in range(PEER_TOPK)]
    v2 = [val_scr[1, b] for b in range(PEER_TOPK)]
    cells = _candidate_cells()
    sums = {c: v1[c[0]] + v2[c[1]] for c in cells}
    before = {c: float((c[0] + 1) * (c[1] + 1) - 1) for c in cells}
    for ci, c in enumerate(cells):
        for d in cells[ci + 1:]:
            if d[0] >= c[0] and d[1] >= c[1]:
                continue
            first = jnp.where(sums[c] >= sums[d], 1.0, 0.0)
            before[d] = before[d] + first
            before[c] = before[c] + (1.0 - first)
    e1 = [jnp.exp(v1[a] - v1[0]) for a in range(PEER_TOPK)]
    e2 = [jnp.exp(v2[b] - v2[0]) for b in range(PEER_TOPK)]
    zsum = 0.0
    count = [0.0] * PEER_TOPK
    for c in cells:
        chosen = before[c] < float(PEER_TOPK)
        count[c[0]] = count[c[0]] + jnp.where(chosen, 1.0, 0.0)
        zsum = zsum + jnp.where(chosen, e1[c[0]] * e2[c[1]], 0.0)
    for a in range(PEER_TOPK):
        n_scr[a] = count[a]
    n_scr[PEER_TOPK] = 0.5 / zsum

    for h in range(PEER_HEADS):
        row = pl.ds(h, 1)
        rank1 = rank_scr[0, h]
        n1 = jnp.zeros((PEER_KEYS, tb), F32)
        for a in range(PEER_TOPK):
            n1 = jnp.where(rank1 == float(a), n_scr[a, row, :], n1)
        n1_ref[h] = n1
        w1 = jnp.where(rank1 < float(PEER_TOPK), jnp.exp(orig_scr[0, h] - val_scr[0, 0, row, :]), 0.0)
        c1_ref[h] = w1 * n_scr[PEER_TOPK, row, :]
        rank2 = rank_scr[1, h]
        r2_ref[h] = rank2.astype(BF16)
        w2 = jnp.where(rank2 < float(PEER_TOPK), jnp.exp(orig_scr[1, h] - val_scr[1, 0, row, :]), 0.0)
        e2_ref[h] = w2.astype(BF16)


def _route(h2, wpq, keys):
    tb = ROUTE_BLOCK
    out = lambda dt: jax.ShapeDtypeStruct((PEER_HEADS, PEER_KEYS, T_ALL), dt)
    ospec = pl.BlockSpec((PEER_HEADS, PEER_KEYS, tb), lambda i: (0, 0, i))
    hk = PEER_HEADS * PEER_KEYS
    return pl.pallas_call(
        _route_kernel,
        grid=(T_ALL // tb,),
        in_specs=[pl.BlockSpec((tb, D_MODEL), lambda i: (i, 0)),
                  pl.BlockSpec((D_MODEL, 2 * hk), lambda i: (0, 0)),
                  pl.BlockSpec((2, hk, hk), lambda i: (0, 0, 0))],
        out_specs=[ospec, ospec, ospec, ospec],
        out_shape=[out(BF16), out(BF16), out(F32), out(F32)],
        scratch_shapes=[pltpu.VMEM((PEER_HEADS, PEER_KEYS, tb), F32),
                        pltpu.VMEM((2, PEER_HEADS, PEER_KEYS, tb), F32),
                        pltpu.VMEM((2, PEER_HEADS, PEER_KEYS, tb), F32),
                        pltpu.VMEM((2, PEER_TOPK, PEER_HEADS, tb), F32),
                        pltpu.VMEM((PEER_TOPK + 1, PEER_HEADS, tb), F32)],
        compiler_params=_params("arbitrary"),
        name="peer_route",
    )(h2, wpq, keys)


SQRT_HALF = math.sqrt(0.5)


EXPERT_TOKEN_CHUNK = 256
N_EXPERT_CHUNKS = TOKEN_BLOCK // EXPERT_TOKEN_CHUNK


def _experts_kernel(xt_ref, u_ref, vt_ref, r2_ref, e2_ref, n1_ref, c1_ref, res_ref, m_ref,
                    o_ref, acc_scr, act_scr, p_scr):
    eb = pl.program_id(1)

    @pl.when(eb == 0)
    def _():
        acc_scr[...] = jnp.zeros_like(acc_scr)

    cols = lambda j: slice(j * EXPERT_TOKEN_CHUNK, (j + 1) * EXPERT_TOKEN_CHUNK)
    def pre_activate(j):
        act_scr[j % 2] = jnp.dot(u_ref[...], xt_ref[:, cols(j)], preferred_element_type=F32)

    def activate(j):
        for c in range(I1_PER_BLOCK):
            i1 = pl.ds(eb * I1_PER_BLOCK + c, 1)
            tiles = (PEER_KEYS // BF16_TILE_ROWS, BF16_TILE_ROWS, EXPERT_TOKEN_CHUNK)
            tile_row = lambda ref, h: jnp.broadcast_to(ref[h, i1, cols(j)], tiles[1:]).astype(BF16)[None]
            gate = jnp.zeros(tiles, BF16)
            for h in range(PEER_HEADS):
                taken = r2_ref[h, :, cols(j)].reshape(tiles) < tile_row(n1_ref, h)
                e2 = e2_ref[h, :, cols(j)].reshape(tiles)
                gate = gate + jnp.where(taken, e2, jnp.zeros((), BF16)) * tile_row(c1_ref, h)
            keys = slice(c * PEER_KEYS, (c + 1) * PEER_KEYS)
            a = act_scr[j % 2, keys, :]
            erf1 = 1.0 + lax.erf(a * SQRT_HALF)
            p_scr[keys, cols(j)] = (a.astype(BF16) * erf1.astype(BF16)
                                    * gate.reshape(PEER_KEYS, EXPERT_TOKEN_CHUNK))

    def mix(j):
        acc_scr[:, cols(j)] += jnp.dot(vt_ref[...], p_scr[:, cols(j)], preferred_element_type=F32)

    for stage in range(N_EXPERT_CHUNKS + 2):
        if stage < N_EXPERT_CHUNKS:
            pre_activate(stage)
        if 1 <= stage <= N_EXPERT_CHUNKS:
            activate(stage - 1)
        if stage >= 2:
            mix(stage - 2)

    @pl.when(eb == pl.num_programs(1) - 1)
    def _():
        o_ref[...] = res_ref[...] + m_ref[MOD_GATE2:MOD_GATE2 + 1, :] * acc_scr[...].T


def _experts(h2t, u_bf, vt_bf, r2, e2, n1, c1, x_mid, mod_tok):
    once = pl.Buffered(1)
    tok = pl.BlockSpec((PEER_HEADS, PEER_KEYS, TOKEN_BLOCK), lambda t, e: (0, 0, t), pipeline_mode=once)
    return pl.pallas_call(
        _experts_kernel,
        grid=(N_TOKEN_BLOCKS, PEER_EXPERTS // EXPERT_BLOCK),
        in_specs=[pl.BlockSpec((D_MODEL, TOKEN_BLOCK), lambda t, e: (0, t)),
                  pl.BlockSpec((EXPERT_BLOCK, D_MODEL), lambda t, e: (e, 0)),
                  pl.BlockSpec((D_MODEL, EXPERT_BLOCK), lambda t, e: (0, e)),
                  tok, tok, tok, tok,
                  pl.BlockSpec((TOKEN_BLOCK, D_MODEL), lambda t, e: (t, 0), pipeline_mode=once),
                  pl.BlockSpec((None, N_MOD, D_MODEL), lambda t, e: (t, 0, 0))],
        out_specs=pl.BlockSpec((TOKEN_BLOCK, D_MODEL), lambda t, e: (t, 0)),
        out_shape=jax.ShapeDtypeStruct((T_ALL, D_MODEL), F32),
        scratch_shapes=[pltpu.VMEM((D_MODEL, TOKEN_BLOCK), F32),
                        pltpu.VMEM((2, EXPERT_BLOCK, EXPERT_TOKEN_CHUNK), F32),
                        pltpu.VMEM((EXPERT_BLOCK, TOKEN_BLOCK), BF16)],
        compiler_params=_params("arbitrary", "arbitrary"),
        name="peer_experts",
    )(h2t, u_bf, vt_bf, r2, e2, n1, c1, x_mid, mod_tok)


def _route_weights(w_pq_l, sub_keys_l):
    hk = PEER_HEADS * PEER_KEY_DIM
    wpq = w_pq_l.reshape(D_MODEL, PEER_HEADS, 2, PEER_KEY_DIM).transpose(0, 2, 1, 3).reshape(D_MODEL, 2 * hk)
    eye = jnp.eye(PEER_HEADS, dtype=sub_keys_l.dtype)
    keys = jnp.einsum('hpjd,hg->phjgd', sub_keys_l, eye).reshape(2, PEER_HEADS * PEER_KEYS, hk)
    return wpq.astype(BF16), keys.astype(BF16)


def kernel(x_prompt, x_sample, cache_k, cache_v, c, c_ctx, w_in, conv_w, w_conv_out, rpb, w_attn_out,
           w_four_out, w_o, g_mix, g_ffn, w_mod, b_mod, w_pq, sub_keys, peer_u, peer_v, g_final):
    assert SEQ & (SEQ - 1) == 0 and DEC_SEQ & (DEC_SEQ - 1) == 0
    x = jnp.concatenate([x_prompt.reshape(T_CTX, D_MODEL), x_sample.reshape(T_LAT, D_MODEL)], axis=0)
    cvec = jnp.zeros((8, D_MODEL), F32).at[0].set(c_ctx).at[1:1 + DEC_BATCH].set(c)
    mod = _modulation(cvec, w_mod, b_mod)
    lat_per_block = DEC_SEQ // TOKEN_BLOCK
    block_row = np.array([0] * N_CTX_BLOCKS
                         + [1 + b for b in range(DEC_BATCH) for _ in range(lat_per_block)])
    mod_tok = mod[:, block_row].reshape(DEPTH, N_TOKEN_BLOCKS, N_MOD, D_MODEL)
    cache_k4 = cache_k.reshape(DEC_BATCH, DEPTH, PAST_LEN, D_MODEL)
    cache_v4 = cache_v.reshape(DEC_BATCH, DEPTH, PAST_LEN, D_MODEL)

    new_k = new_v = None
    for l in range(DEPTH):
        h = _norm_mod(x, g_mix[l], mod_tok[l])
        z, new_k, new_v = _in_proj(h, w_in, l, new_k, new_v)
        uconv = _short_conv(z, conv_w[l])
        att = _ctx_attention(z)
        att = _na_attention(z, cache_k4, cache_v4, _na_bias(rpb[l]), att, l)
        four = _fourier(z, SEQ, BATCH, 0, None)
        four = _fourier(z, DEC_SEQ, DEC_BATCH, T_CTX, four)
        x_mid, h2, h2t = _merge(uconv, att, four, z, x,
                           w_conv_out[l].astype(BF16), w_attn_out[l].astype(BF16),
                           w_four_out[l].astype(BF16), w_o[l].astype(BF16), mod_tok[l], g_ffn[l])
        wpq, keys = _route_weights(w_pq[l], sub_keys[l])
        r2, e2, n1, c1 = _route(h2, wpq, keys)
        x = _experts(h2t, peer_u[l].astype(BF16), peer_v[l].astype(BF16).T, r2, e2, n1, c1,
                     x_mid, mod_tok[l])

    y_prompt = _final_norm(x, g_final, 0, N_CTX_BLOCKS).reshape(BATCH, SEQ, D_MODEL)
    y_sample = _final_norm(x, g_final, N_CTX_BLOCKS, N_TOKEN_BLOCKS - N_CTX_BLOCKS)
    y_sample = y_sample.reshape(DEC_BATCH, DEC_SEQ, D_MODEL)
    shape5 = (BATCH, DEPTH, SEQ, NA_HEADS, NA_HEAD_DIM)
    return (y_prompt, y_sample, new_k.reshape(shape5), new_v.reshape(shape5))
```

```python
import functools
import math

import numpy as np
import jax
import jax.numpy as jnp
from jax import lax
from jax.experimental import pallas as pl
from jax.experimental.pallas import tpu as pltpu

D_MODEL = 1024
BATCH = 16
SEQ = 256
DEPTH = 2
DEC_BATCH = 2
DEC_SEQ = 1024
PAST_LEN = 512
GRID_W = 64
CONV_K = 3
NA_HEADS = 8
NA_HEAD_DIM = D_MODEL // NA_HEADS
NA_MAX_ROWS = 8
NA_COLS = 16
FNET_GROUPS = 4
FNET_GROUP_DIM = D_MODEL // FNET_GROUPS
N_BRANCH = 3
IN_COLS = 10 * D_MODEL
PEER_HEADS = 8
PEER_KEYS = 128
PEER_EXPERTS = PEER_KEYS * PEER_KEYS
PEER_KEY_DIM = 128
PEER_TOPK = 16
N_MOD = 6
RMS_EPS = 1e-6
NEG_INF = -1e30

T_CTX = BATCH * SEQ
T_LAT = DEC_BATCH * DEC_SEQ
T_ALL = T_CTX + T_LAT

COL_CB, COL_CC, COL_CX, COL_Q, COL_K, COL_V, COL_F, COL_G0, COL_G1, COL_G2 = range(10)
MOD_SHIFT1, MOD_SCALE1, MOD_GATE1, MOD_SHIFT2, MOD_SCALE2, MOD_GATE2 = range(6)

TOKEN_BLOCK = 1024
N_TOKEN_BLOCKS = T_ALL // TOKEN_BLOCK
N_CTX_BLOCKS = T_CTX // TOKEN_BLOCK
MERGE_BLOCK = 512
ROUTE_BLOCK = 256
EXPERT_BLOCK = 1024
I1_PER_BLOCK = EXPERT_BLOCK // PEER_KEYS
NOT_SELECTED = 127.0
VMEM_LIMIT = 56 * 1024 * 1024
BF16_TILE_ROWS = 16

F32 = jnp.float32
BF16 = jnp.bfloat16
NT_DIMS = (((1,), (1,)), ((), ()))


def _params(*semantics):
    return pltpu.CompilerParams(dimension_semantics=semantics, vmem_limit_bytes=VMEM_LIMIT)


def _mod_kernel(c_ref, w_ref, b_ref, o_ref):
    c = c_ref[...]
    s = c * jax.nn.sigmoid(c)
    o_ref[0] = jnp.dot(s, w_ref[0], preferred_element_type=F32,
                       precision=lax.Precision.HIGHEST) + b_ref[0]


def _modulation(cvec, w_mod, b_mod):
    tn = 1536
    ncol = N_MOD * D_MODEL
    return pl.pallas_call(
        _mod_kernel,
        grid=(DEPTH, ncol // tn),
        in_specs=[pl.BlockSpec((8, D_MODEL), lambda l, n: (0, 0)),
                  pl.BlockSpec((1, D_MODEL, tn), lambda l, n: (l, 0, n)),
                  pl.BlockSpec((1, 1, tn), lambda l, n: (l, 0, n))],
        out_specs=pl.BlockSpec((1, 8, tn), lambda l, n: (l, 0, n)),
        out_shape=jax.ShapeDtypeStruct((DEPTH, 8, ncol), F32),
        compiler_params=_params("arbitrary", "arbitrary"),
        name="adaln_table",
    )(cvec, w_mod, b_mod.reshape(DEPTH, 1, ncol))


def _rms(x, g):
    return x * lax.rsqrt(jnp.mean(x * x, axis=-1, keepdims=True) + RMS_EPS) * g


def _norm_mod_kernel(x_ref, g_ref, m_ref, o_ref):
    y = _rms(x_ref[...], g_ref[...])
    h = y * (1.0 + m_ref[MOD_SCALE1:MOD_SCALE1 + 1, :]) + m_ref[MOD_SHIFT1:MOD_SHIFT1 + 1, :]
    o_ref[...] = h.astype(BF16)


def _norm_mod(x, g, mod_tok):
    return pl.pallas_call(
        _norm_mod_kernel,
        grid=(N_TOKEN_BLOCKS,),
        in_specs=[pl.BlockSpec((TOKEN_BLOCK, D_MODEL), lambda i: (i, 0)),
                  pl.BlockSpec((1, D_MODEL), lambda i: (0, 0)),
                  pl.BlockSpec((None, N_MOD, D_MODEL), lambda i: (i, 0, 0))],
        out_specs=pl.BlockSpec((TOKEN_BLOCK, D_MODEL), lambda i: (i, 0)),
        out_shape=jax.ShapeDtypeStruct((T_ALL, D_MODEL), BF16),
        compiler_params=_params("arbitrary"),
        name="norm_modulate",
    )(x, g.reshape(1, D_MODEL), mod_tok)


def _final_norm_kernel(x_ref, g_ref, o_ref):
    o_ref[...] = _rms(x_ref[...], g_ref[...])


def _final_norm(x, g, first_block, n_blocks):
    return pl.pallas_call(
        _final_norm_kernel,
        grid=(n_blocks,),
        in_specs=[pl.BlockSpec((TOKEN_BLOCK, D_MODEL), lambda i: (i + first_block, 0)),
                  pl.BlockSpec((1, D_MODEL), lambda i: (0, 0))],
        out_specs=pl.BlockSpec((TOKEN_BLOCK, D_MODEL), lambda i: (i, 0)),
        out_shape=jax.ShapeDtypeStruct((n_blocks * TOKEN_BLOCK, D_MODEL), F32),
        compiler_params=_params("arbitrary"),
        name="final_norm",
    )(x, g.reshape(1, D_MODEL))


def _in_proj_kernel(h_ref, w_ref, *rest):
    z_ref, k_ref, v_ref, wb_scr = rest[-4:]
    n = pl.program_id(0)
    i = pl.program_id(1)

    @pl.when(i == 0)
    def _():
        wb_scr[...] = w_ref[...].astype(BF16)

    z = jnp.dot(h_ref[...], wb_scr[...], preferred_element_type=F32)
    z_ref[...] = z.astype(BF16)
    per_block = TOKEN_BLOCK // SEQ

    @pl.when((n == COL_K) & (i < N_CTX_BLOCKS))
    def _():
        k_ref[...] = z.reshape(per_block, SEQ, D_MODEL)

    @pl.when((n == COL_V) & (i < N_CTX_BLOCKS))
    def _():
        v_ref[...] = z.reshape(per_block, SEQ, D_MODEL)


def _cache_block_index(col, layer, n, i):
    last = N_CTX_BLOCKS - 1
    blk = jnp.where(n < col, 0, jnp.where(n > col, last, jnp.minimum(i, last)))
    return (blk, layer, 0, 0)


def _in_proj(h, w_in, layer, cache_k, cache_v):
    per_block = TOKEN_BLOCK // SEQ
    cache_shape = jax.ShapeDtypeStruct((BATCH, DEPTH, SEQ, D_MODEL), F32)
    cache_spec = lambda col: pl.BlockSpec(
        (per_block, None, SEQ, D_MODEL), functools.partial(_cache_block_index, col, layer))
    in_specs = [pl.BlockSpec((TOKEN_BLOCK, D_MODEL), lambda n, i: (i, 0)),
                pl.BlockSpec((None, D_MODEL, D_MODEL), lambda n, i: (layer, 0, n))]
    args = [h, w_in]
    aliases = {}
    if cache_k is not None:
        in_specs += [pl.BlockSpec(memory_space=pl.ANY)] * 2
        args += [cache_k, cache_v]
        aliases = {2: 1, 3: 2}
    return pl.pallas_call(
        _in_proj_kernel,
        grid=(IN_COLS // D_MODEL, N_TOKEN_BLOCKS),
        in_specs=in_specs,
        out_specs=[pl.BlockSpec((TOKEN_BLOCK, D_MODEL), lambda n, i: (i, n)),
                   cache_spec(COL_K), cache_spec(COL_V)],
        out_shape=[jax.ShapeDtypeStruct((T_ALL, IN_COLS), BF16), cache_shape, cache_shape],
        scratch_shapes=[pltpu.VMEM((D_MODEL, D_MODEL), BF16)],
        input_output_aliases=aliases,
        compiler_params=_params("arbitrary", "arbitrary"),
        name="in_proj",
    )(*args)


def _conv_kernel(cb_ref, cc_ref, cx_ref, w_ref, o_ref):
    i = pl.program_id(0)
    u = cc_ref[...].astype(F32) * cx_ref[...].astype(F32)
    seq = jnp.where(i < N_CTX_BLOCKS, SEQ, DEC_SEQ)
    pos = lax.broadcasted_iota(jnp.int32, (TOKEN_BLOCK, 1), 0) & (seq - 1)
    prev = jnp.where(pos == 0, 0.0, pltpu.roll(u, 1, 0))
    nxt = jnp.where(pos == seq - 1, 0.0, pltpu.roll(u, TOKEN_BLOCK - 1, 0))
    y = w_ref[0:1, :] * prev + w_ref[1:2, :] * u + w_ref[2:3, :] * nxt
    o_ref[...] = (cb_ref[...].astype(F32) * y).astype(BF16)


def _short_conv(z, conv_w):
    col = lambda c: pl.BlockSpec((TOKEN_BLOCK, D_MODEL), lambda i: (i, c))
    return pl.pallas_call(
        _conv_kernel,
        grid=(N_TOKEN_BLOCKS,),
        in_specs=[col(COL_CB), col(COL_CC), col(COL_CX),
                  pl.BlockSpec((CONV_K, D_MODEL), lambda i: (0, 0))],
        out_specs=pl.BlockSpec((TOKEN_BLOCK, D_MODEL), lambda i: (i, 0)),
        out_shape=jax.ShapeDtypeStruct((T_ALL, D_MODEL), BF16),
        compiler_params=_params("arbitrary"),
        name="short_conv",
    )(z, z, z, conv_w)


ATT_SCALE = NA_HEAD_DIM ** -0.5


def _ctx_attn_kernel(q_ref, k_ref, v_ref, o_ref):
    for h in range(NA_HEADS):
        sl = slice(h * NA_HEAD_DIM, (h + 1) * NA_HEAD_DIM)
        s = lax.dot_general(q_ref[:, sl], k_ref[:, sl], NT_DIMS,
                            preferred_element_type=F32) * ATT_SCALE
        p = jnp.exp(s - jnp.max(s, axis=-1, keepdims=True))
        o = jnp.dot(p.astype(BF16), v_ref[:, sl], preferred_element_type=F32)
        o_ref[:, sl] = (o / jnp.sum(p, axis=-1, keepdims=True)).astype(BF16)


def _ctx_attention(z):
    col = lambda c: pl.BlockSpec((SEQ, D_MODEL), lambda b: (b, c))
    return pl.pallas_call(
        _ctx_attn_kernel,
        grid=(BATCH,),
        in_specs=[col(COL_Q), col(COL_K), col(COL_V)],
        out_specs=pl.BlockSpec((SEQ, D_MODEL), lambda b: (b, 0)),
        out_shape=jax.ShapeDtypeStruct((T_ALL, D_MODEL), BF16),
        compiler_params=_params("arbitrary"),
        name="ctx_attention",
    )(z, z, z)


NA_GRID_ROWS = DEC_SEQ // GRID_W
NA_WIN_ROWS = min(NA_MAX_ROWS, NA_GRID_ROWS)


def _na_row_groups():
    groups = []
    for r in range(NA_GRID_ROWS):
        r0 = min(max(r - NA_WIN_ROWS // 2, 0), NA_GRID_ROWS - NA_WIN_ROWS)
        if groups and groups[-1][0] == r0:
            groups[-1][1].append(r)
        else:
            groups.append((r0, [r]))
    return groups


def _na_kernel(q_ref, k_ref, v_ref, ck_ref, cv_ref, b_ref, att_in_ref, o_ref):
    del att_in_ref
    ck = ck_ref[...].astype(BF16)
    cv = cv_ref[...].astype(BF16)
    for r0, q_rows in _na_row_groups():
        rows = slice(q_rows[0] * GRID_W, (q_rows[-1] + 1) * GRID_W)
        win = slice(r0 * GRID_W, (r0 + NA_WIN_ROWS) * GRID_W)
        q = q_ref[rows, :]
        bias = jnp.concatenate([b_ref[r0 - r + NA_MAX_ROWS - 1] for r in q_rows], axis=0)
        s_loc = lax.dot_general(q, k_ref[win, :], NT_DIMS, preferred_element_type=F32) * ATT_SCALE + bias
        s_ctx = lax.dot_general(q, ck, NT_DIMS, preferred_element_type=F32) * ATT_SCALE
        m = jnp.maximum(jnp.max(s_loc, axis=-1, keepdims=True), jnp.max(s_ctx, axis=-1, keepdims=True))
        p_loc = jnp.exp(s_loc - m)
        p_ctx = jnp.exp(s_ctx - m)
        den = jnp.sum(p_loc, axis=-1, keepdims=True) + jnp.sum(p_ctx, axis=-1, keepdims=True)
        o = (jnp.dot(p_loc.astype(BF16), v_ref[win, :], preferred_element_type=F32)
             + jnp.dot(p_ctx.astype(BF16), cv, preferred_element_type=F32))
        o_ref[rows, :] = (o / den).astype(BF16)


def _na_bias(rpb_l):
    c = np.arange(GRID_W)
    c0 = np.clip(c - NA_COLS // 2, 0, GRID_W - NA_COLS)
    in_cols = (c[None, :] >= c0[:, None]) & (c[None, :] < c0[:, None] + NA_COLS)
    dc = c[None, :] - c[:, None] + (NA_COLS - 1)
    pick = (dc[None] == np.arange(2 * NA_COLS - 1)[:, None, None]) & in_cols[None]
    t = jnp.einsum('hdj,jqk->hdqk', rpb_l.astype(F32), jnp.asarray(pick, F32),
                   precision=lax.Precision.HIGHEST)
    t = jnp.where(in_cols[None, None], t, NEG_INF)
    slabs = [t[:, d:d + NA_WIN_ROWS].transpose(0, 2, 1, 3).reshape(NA_HEADS, GRID_W, NA_WIN_ROWS * GRID_W)
             for d in range(NA_WIN_ROWS)]
    return jnp.stack(slabs, axis=1)


def _na_attention(z, cache_k_l, cache_v_l, bias, att, layer):
    lat0 = T_CTX // DEC_SEQ
    col = lambda c: pl.BlockSpec((DEC_SEQ, NA_HEAD_DIM), lambda h, b: (lat0 + b, c * NA_HEADS + h))
    cache = pl.BlockSpec((None, None, PAST_LEN, NA_HEAD_DIM), lambda h, b: (b, layer, 0, h))
    return pl.pallas_call(
        _na_kernel,
        grid=(NA_HEADS, DEC_BATCH),
        in_specs=[col(COL_Q), col(COL_K), col(COL_V), cache, cache,
                  pl.BlockSpec((None, NA_WIN_ROWS, GRID_W, NA_WIN_ROWS * GRID_W), lambda h, b: (h, 0, 0, 0)),
                  pl.BlockSpec(memory_space=pl.ANY)],
        out_specs=pl.BlockSpec((DEC_SEQ, NA_HEAD_DIM), lambda h, b: (lat0 + b, h)),
        out_shape=jax.ShapeDtypeStruct((T_ALL, D_MODEL), BF16),
        input_output_aliases={6: 0},
        compiler_params=_params("arbitrary", "arbitrary"),
        name="na_attention",
    )(z, z, z, cache_k_l, cache_v_l, bias, att)


def _dft_matrices(n):
    j = np.arange(n)
    ang = 2.0 * np.pi * ((j[:, None] * j[None, :]) % n) / n
    return np.cos(ang) / math.sqrt(n), np.sin(ang) / math.sqrt(n)


def _fourier_kernel(cs_ref, ss_ref, f_ref, w2_ref, *rest):
    o_ref = rest[-1]
    f = f_ref[...]
    cu = jnp.dot(cs_ref[...], f, preferred_element_type=F32).astype(BF16)
    su = jnp.dot(ss_ref[...], f, preferred_element_type=F32).astype(BF16)
    for g in range(FNET_GROUPS):
        sl = slice(g * FNET_GROUP_DIM, (g + 1) * FNET_GROUP_DIM)
        lhs = jnp.concatenate([cu[:, sl], su[:, sl]], axis=1)
        o_ref[:, sl] = jnp.dot(lhs, w2_ref[...], preferred_element_type=F32).astype(BF16)


FOURIER_ROWS = 256


def _fourier(z, seq, n_batch, first_row, prev):
    cs, ss = _dft_matrices(seq)
    cc, sc = _dft_matrices(FNET_GROUP_DIM)
    w2 = jnp.asarray(np.concatenate([cc, -sc], axis=0), F32).astype(BF16)
    rows = FOURIER_ROWS
    nr = seq // rows
    in_specs = [pl.BlockSpec((rows, seq), lambda b, r: (r, 0)),
                pl.BlockSpec((rows, seq), lambda b, r: (r, 0)),
                pl.BlockSpec((seq, D_MODEL), lambda b, r: (first_row // seq + b, COL_F)),
                pl.BlockSpec((2 * FNET_GROUP_DIM, FNET_GROUP_DIM), lambda b, r: (0, 0))]
    args = [jnp.asarray(cs, F32).astype(BF16), jnp.asarray(ss, F32).astype(BF16), z, w2]
    aliases = {}
    if prev is not None:
        in_specs.append(pl.BlockSpec(memory_space=pl.ANY))
        args.append(prev)
        aliases = {4: 0}
    return pl.pallas_call(
        _fourier_kernel,
        grid=(n_batch, nr),
        in_specs=in_specs,
        out_specs=pl.BlockSpec((rows, D_MODEL), lambda b, r: (first_row // rows + b * nr + r, 0)),
        out_shape=jax.ShapeDtypeStruct((T_ALL, D_MODEL), BF16),
        input_output_aliases=aliases,
        compiler_params=_params("arbitrary", "arbitrary"),
        name="fourier_mix",
    )(*args)


def _merge_kernel(uc_ref, at_ref, fo_ref, g0_ref, g1_ref, g2_ref, x_ref, wc_ref, wa_ref, wf_ref,
                  wo_ref, m_ref, gf_ref, xo_ref, h2_ref, h2t_ref):
    yc = jnp.dot(uc_ref[...], wc_ref[...], preferred_element_type=F32)
    ya = jnp.dot(at_ref[...], wa_ref[...], preferred_element_type=F32)
    yf = jnp.dot(fo_ref[...], wf_ref[...], preferred_element_type=F32)
    gate = lambda r: jax.nn.sigmoid(r[...].astype(F32))
    merged = gate(g0_ref) * yc + gate(g1_ref) * ya + gate(g2_ref) * yf
    y = jnp.dot(merged.astype(BF16), wo_ref[...], preferred_element_type=F32)
    x = x_ref[...] + m_ref[MOD_GATE1:MOD_GATE1 + 1, :] * y
    xo_ref[...] = x
    hn = _rms(x, gf_ref[...])
    h2 = hn * (1.0 + m_ref[MOD_SCALE2:MOD_SCALE2 + 1, :]) + m_ref[MOD_SHIFT2:MOD_SHIFT2 + 1, :]
    h2_ref[...] = h2.astype(BF16)
    h2t_ref[...] = h2.T.astype(BF16)


def _merge(uconv, att, four, z, x, wc, wa, wf, wo, mod_tok, g_ffn):
    per = TOKEN_BLOCK // MERGE_BLOCK
    row = pl.BlockSpec((MERGE_BLOCK, D_MODEL), lambda i: (i, 0))
    col = lambda c: pl.BlockSpec((MERGE_BLOCK, D_MODEL), lambda i: (i, c))
    wspec = pl.BlockSpec((D_MODEL, D_MODEL), lambda i: (0, 0))
    return pl.pallas_call(
        _merge_kernel,
        grid=(T_ALL // MERGE_BLOCK,),
        in_specs=[row, row, row, col(COL_G0), col(COL_G1), col(COL_G2), row,
                  wspec, wspec, wspec, wspec,
                  pl.BlockSpec((None, N_MOD, D_MODEL), lambda i: (i // per, 0, 0)),
                  pl.BlockSpec((1, D_MODEL), lambda i: (0, 0))],
        out_specs=[row, row, pl.BlockSpec((D_MODEL, MERGE_BLOCK), lambda i: (0, i))],
        out_shape=[jax.ShapeDtypeStruct((T_ALL, D_MODEL), F32),
                   jax.ShapeDtypeStruct((T_ALL, D_MODEL), BF16),
                   jax.ShapeDtypeStruct((D_MODEL, T_ALL), BF16)],
        compiler_params=_params("arbitrary"),
        name="branch_merge",
    )(uconv, att, four, z, z, z, x, wc, wa, wf, wo, mod_tok, g_ffn.reshape(1, D_MODEL))


def _candidate_cells():
    return [(a, b) for a in range(PEER_TOPK) for b in range(PEER_TOPK) if (a + 1) * (b + 1) <= PEER_TOPK]


def _extract_topk(p, orig_scr, s_scr, rank_scr, val_scr, break_ties):
    shape = (PEER_HEADS, PEER_KEYS, ROUTE_BLOCK)
    s_scr[...] = orig_scr[p]
    rank_scr[p] = jnp.full(shape, NOT_SELECTED, F32)
    for k in range(PEER_TOPK):
        cur = s_scr[...]
        m = jnp.max(cur, axis=1, keepdims=True)
        sel = cur == m
        if break_ties:
            key_iota = lax.broadcasted_iota(jnp.int32, shape, 1).astype(F32)
            first = jnp.min(jnp.where(sel, key_iota, float(PEER_KEYS)), axis=1, keepdims=True)
            sel = key_iota == first
        s_scr[...] = jnp.where(sel, -jnp.inf, cur)
        rank_scr[p] = jnp.where(sel, float(k), rank_scr[p])
        for h in range(PEER_HEADS):
            val_scr[p, k, h:h + 1, :] = m[h]


def _route_kernel(h_ref, wpq_ref, keys_ref, r2_ref, e2_ref, n1_ref, c1_ref,
                  s_scr, orig_scr, rank_scr, val_scr, n_scr):
    tb = ROUTE_BLOCK
    q = jnp.dot(h_ref[...], wpq_ref[...], preferred_element_type=F32).astype(BF16)

    for p in range(2):
        qp = q[:, p * PEER_HEADS * PEER_KEY_DIM:(p + 1) * PEER_HEADS * PEER_KEY_DIM]
        s = lax.dot_general(keys_ref[p], qp, NT_DIMS, preferred_element_type=F32)
        orig_scr[p] = s.reshape(PEER_HEADS, PEER_KEYS, tb)
    off = 0.0
    for p in range(2):
        _extract_topk(p, orig_scr, s_scr, rank_scr, val_scr, break_ties=False)
        taken = jnp.sum(jnp.where(rank_scr[p] < float(PEER_TOPK), 1.0, 0.0), axis=1)
        off = jnp.maximum(off, jnp.max(jnp.abs(taken - float(PEER_TOPK))))

    @pl.when(off > 0.0)
    def _():
        for p in range(2):
            _extract_topk(p, orig_scr, s_scr, rank_scr, val_scr, break_ties=True)

    v1 = [val_scr[0, a] for a in range(PEER_TOPK)]
    v2 = [val_scr[1, b] for b in range(PEER_TOPK)]
    cells = _candidate_cells()
    sums = {c: v1[c[0]] + v2[c[1]] for c in cells}
    undecided = lambda c, d: not (d[0] >= c[0] and d[1] >= c[1])
    before = {c: float((c[0] + 1) * (c[1] + 1) - 1 + sum(undecided(c, d) for d in cells[ci + 1:]))
              for ci, c in enumerate(cells)}
    for ci, c in enumerate(cells):
        for d in cells[ci + 1:]:
            if undecided(c, d):
                first = jnp.where(sums[c] >= sums[d], 1.0, 0.0)
                before[d] = before[d] + first
                before[c] = before[c] - first
    e1 = [jnp.exp(v1[a] - v1[0]) for a in range(PEER_TOPK)]
    e2 = [jnp.exp(v2[b] - v2[0]) for b in range(PEER_TOPK)]
    zsum = 0.0
    count = [0.0] * PEER_TOPK
    for c in cells:
        chosen = before[c] < float(PEER_TOPK)
        count[c[0]] = count[c[0]] + jnp.where(chosen, 1.0, 0.0)
        zsum = zsum + jnp.where(chosen, e1[c[0]] * e2[c[1]], 0.0)
    for a in range(PEER_TOPK):
        n_scr[a] = count[a]
    n_scr[PEER_TOPK] = 0.5 / zsum

    for h in range(PEER_HEADS):
        row = pl.ds(h, 1)
        rank1 = rank_scr[0, h]
        n1 = jnp.zeros((PEER_KEYS, tb), F32)
        for a in range(PEER_TOPK):
            n1 = jnp.where(rank1 == float(a), n_scr[a, row, :], n1)
        n1_ref[h] = n1
        w1 = jnp.where(rank1 < float(PEER_TOPK), jnp.exp(orig_scr[0, h] - val_scr[0, 0, row, :]), 0.0)
        c1_ref[h] = w1 * n_scr[PEER_TOPK, row, :]
        rank2 = rank_scr[1, h]
        r2_ref[h] = rank2.astype(BF16)
        w2 = jnp.where(rank2 < float(PEER_TOPK), jnp.exp(orig_scr[1, h] - val_scr[1, 0, row, :]), 0.0)
        e2_ref[h] = w2.astype(BF16)


def _route(h2, wpq, keys):
    tb = ROUTE_BLOCK
    out = lambda dt: jax.ShapeDtypeStruct((PEER_HEADS, PEER_KEYS, T_ALL), dt)
    ospec = pl.BlockSpec((PEER_HEADS, PEER_KEYS, tb), lambda i: (0, 0, i))
    hk = PEER_HEADS * PEER_KEYS
    return pl.pallas_call(
        _route_kernel,
        grid=(T_ALL // tb,),
        in_specs=[pl.BlockSpec((tb, D_MODEL), lambda i: (i, 0)),
                  pl.BlockSpec((D_MODEL, 2 * hk), lambda i: (0, 0)),
                  pl.BlockSpec((2, hk, hk), lambda i: (0, 0, 0))],
        out_specs=[ospec, ospec, ospec, ospec],
        out_shape=[out(BF16), out(BF16), out(F32), out(F32)],
        scratch_shapes=[pltpu.VMEM((PEER_HEADS, PEER_KEYS, tb), F32),
                        pltpu.VMEM((2, PEER_HEADS, PEER_KEYS, tb), F32),
                        pltpu.VMEM((2, PEER_HEADS, PEER_KEYS, tb), F32),
                        pltpu.VMEM((2, PEER_TOPK, PEER_HEADS, tb), F32),
                        pltpu.VMEM((PEER_TOPK + 1, PEER_HEADS, tb), F32)],
        compiler_params=_params("arbitrary"),
        name="peer_route",
    )(h2, wpq, keys)


SQRT_HALF = math.sqrt(0.5)


EXPERT_TOKEN_CHUNK = 1024
N_EXPERT_CHUNKS = TOKEN_BLOCK // EXPERT_TOKEN_CHUNK


def _experts_kernel(xt_ref, u_ref, vt_ref, r2_ref, e2_ref, n1_ref, c1_ref, res_ref, m_ref,
                    o_ref, acc_scr, act_scr, p_scr):
    eb = pl.program_id(1)

    @pl.when(eb == 0)
    def _():
        acc_scr[...] = jnp.zeros_like(acc_scr)

    cols = lambda j: slice(j * EXPERT_TOKEN_CHUNK, (j + 1) * EXPERT_TOKEN_CHUNK)
    def pre_activate(j):
        act_scr[j % 2] = jnp.dot(u_ref[...], xt_ref[:, cols(j)], preferred_element_type=F32)

    def activate(j):
        for c in range(I1_PER_BLOCK):
            i1 = pl.ds(eb * I1_PER_BLOCK + c, 1)
            tiles = (PEER_KEYS // BF16_TILE_ROWS, BF16_TILE_ROWS, EXPERT_TOKEN_CHUNK)
            tile_row = lambda ref, h: jnp.broadcast_to(ref[h, i1, cols(j)], tiles[1:]).astype(BF16)[None]
            gate = jnp.zeros(tiles, BF16)
            for h in range(PEER_HEADS):
                taken = r2_ref[h, :, cols(j)].reshape(tiles) < tile_row(n1_ref, h)
                e2 = e2_ref[h, :, cols(j)].reshape(tiles)
                gate = gate + jnp.where(taken, e2, jnp.zeros((), BF16)) * tile_row(c1_ref, h)
            keys = slice(c * PEER_KEYS, (c + 1) * PEER_KEYS)
            a = act_scr[j % 2, keys, :]
            erf1 = 1.0 + lax.erf(a * SQRT_HALF)
            p_scr[keys, cols(j)] = (a.astype(BF16) * erf1.astype(BF16)
                                    * gate.reshape(PEER_KEYS, EXPERT_TOKEN_CHUNK))

    def mix(j):
        acc_scr[:, cols(j)] += jnp.dot(vt_ref[...], p_scr[:, cols(j)], preferred_element_type=F32)

    for stage in range(N_EXPERT_CHUNKS + 2):
        if stage < N_EXPERT_CHUNKS:
            pre_activate(stage)
        if 1 <= stage <= N_EXPERT_CHUNKS:
            activate(stage - 1)
        if stage >= 2:
            mix(stage - 2)

    @pl.when(eb == pl.num_programs(1) - 1)
    def _():
        o_ref[...] = res_ref[...] + m_ref[MOD_GATE2:MOD_GATE2 + 1, :] * acc_scr[...].T


def _experts(h2t, u_bf, vt_bf, r2, e2, n1, c1, x_mid, mod_tok):
    once = pl.Buffered(1)
    tok = pl.BlockSpec((PEER_HEADS, PEER_KEYS, TOKEN_BLOCK), lambda t, e: (0, 0, t), pipeline_mode=once)
    return pl.pallas_call(
        _experts_kernel,
        grid=(N_TOKEN_BLOCKS, PEER_EXPERTS // EXPERT_BLOCK),
        in_specs=[pl.BlockSpec((D_MODEL, TOKEN_BLOCK), lambda t, e: (0, t)),
                  pl.BlockSpec((EXPERT_BLOCK, D_MODEL), lambda t, e: (e, 0)),
                  pl.BlockSpec((D_MODEL, EXPERT_BLOCK), lambda t, e: (0, e)),
                  tok, tok, tok, tok,
                  pl.BlockSpec((TOKEN_BLOCK, D_MODEL), lambda t, e: (t, 0), pipeline_mode=once),
                  pl.BlockSpec((None, N_MOD, D_MODEL), lambda t, e: (t, 0, 0))],
        out_specs=pl.BlockSpec((TOKEN_BLOCK, D_MODEL), lambda t, e: (t, 0)),
        out_shape=jax.ShapeDtypeStruct((T_ALL, D_MODEL), F32),
        scratch_shapes=[pltpu.VMEM((D_MODEL, TOKEN_BLOCK), F32),
                        pltpu.VMEM((2, EXPERT_BLOCK, EXPERT_TOKEN_CHUNK), F32),
                        pltpu.VMEM((EXPERT_BLOCK, TOKEN_BLOCK), BF16)],
        compiler_params=_params("arbitrary", "arbitrary"),
        name="peer_experts",
    )(h2t, u_bf, vt_bf, r2, e2, n1, c1, x_mid, mod_tok)


def _route_weights(w_pq_l, sub_keys_l):
    hk = PEER_HEADS * PEER_KEY_DIM
    wpq = w_pq_l.reshape(D_MODEL, PEER_HEADS, 2, PEER_KEY_DIM).transpose(0, 2, 1, 3).reshape(D_MODEL, 2 * hk)
    eye = jnp.eye(PEER_HEADS, dtype=sub_keys_l.dtype)
    keys = jnp.einsum('hpjd,hg->phjgd', sub_keys_l, eye).reshape(2, PEER_HEADS * PEER_KEYS, hk)
    return wpq.astype(BF16), keys.astype(BF16)


def kernel(x_prompt, x_sample, cache_k, cache_v, c, c_ctx, w_in, conv_w, w_conv_out, rpb, w_attn_out,
           w_four_out, w_o, g_mix, g_ffn, w_mod, b_mod, w_pq, sub_keys, peer_u, peer_v, g_final):
    assert SEQ & (SEQ - 1) == 0 and DEC_SEQ & (DEC_SEQ - 1) == 0
    x = jnp.concatenate([x_prompt.reshape(T_CTX, D_MODEL), x_sample.reshape(T_LAT, D_MODEL)], axis=0)
    cvec = jnp.zeros((8, D_MODEL), F32).at[0].set(c_ctx).at[1:1 + DEC_BATCH].set(c)
    mod = _modulation(cvec, w_mod, b_mod)
    lat_per_block = DEC_SEQ // TOKEN_BLOCK
    block_row = np.array([0] * N_CTX_BLOCKS
                         + [1 + b for b in range(DEC_BATCH) for _ in range(lat_per_block)])
    mod_tok = mod[:, block_row].reshape(DEPTH, N_TOKEN_BLOCKS, N_MOD, D_MODEL)
    cache_k4 = cache_k.reshape(DEC_BATCH, DEPTH, PAST_LEN, D_MODEL)
    cache_v4 = cache_v.reshape(DEC_BATCH, DEPTH, PAST_LEN, D_MODEL)

    new_k = new_v = None
    for l in range(DEPTH):
        h = _norm_mod(x, g_mix[l], mod_tok[l])
        z, new_k, new_v = _in_proj(h, w_in, l, new_k, new_v)
        uconv = _short_conv(z, conv_w[l])
        att = _ctx_attention(z)
        att = _na_attention(z, cache_k4, cache_v4, _na_bias(rpb[l]), att, l)
        four = _fourier(z, SEQ, BATCH, 0, None)
        four = _fourier(z, DEC_SEQ, DEC_BATCH, T_CTX, four)
        x_mid, h2, h2t = _merge(uconv, att, four, z, x,
                           w_conv_out[l].astype(BF16), w_attn_out[l].astype(BF16),
                           w_four_out[l].astype(BF16), w_o[l].astype(BF16), mod_tok[l], g_ffn[l])
        wpq, keys = _route_weights(w_pq[l], sub_keys[l])
        r2, e2, n1, c1 = _route(h2, wpq, keys)
        x = _experts(h2t, peer_u[l].astype(BF16), peer_v[l].astype(BF16).T, r2, e2, n1, c1,
                     x_mid, mod_tok[l])

    y_prompt = _final_norm(x, g_final, 0, N_CTX_BLOCKS).reshape(BATCH, SEQ, D_MODEL)
    y_sample = _final_norm(x, g_final, N_CTX_BLOCKS, N_TOKEN_BLOCKS - N_CTX_BLOCKS)
    y_sample = y_sample.reshape(DEC_BATCH, DEC_SEQ, D_MODEL)
    shape5 = (BATCH, DEPTH, SEQ, NA_HEADS, NA_HEAD_DIM)
    return (y_prompt, y_sample, new_k.reshape(shape5), new_v.reshape(shape5))
```

```python
import functools
import math

import numpy as np
import jax
import jax.numpy as jnp
from jax import lax
from jax.experimental import pallas as pl
from jax.experimental.pallas import tpu as pltpu

D_MODEL = 1024
BATCH = 16
SEQ = 256
DEPTH = 2
DEC_BATCH = 2
DEC_SEQ = 1024
PAST_LEN = 512
GRID_W = 64
CONV_K = 3
NA_HEADS = 8
NA_HEAD_DIM = D_MODEL // NA_HEADS
NA_MAX_ROWS = 8
NA_COLS = 16
FNET_GROUPS = 4
FNET_GROUP_DIM = D_MODEL // FNET_GROUPS
N_BRANCH = 3
IN_COLS = 10 * D_MODEL
PEER_HEADS = 8
PEER_KEYS = 128
PEER_EXPERTS = PEER_KEYS * PEER_KEYS
PEER_KEY_DIM = 128
PEER_TOPK = 16
N_MOD = 6
RMS_EPS = 1e-6
NEG_INF = -1e30

T_CTX = BATCH * SEQ
T_LAT = DEC_BATCH * DEC_SEQ
T_ALL = T_CTX + T_LAT

COL_CB, COL_CC, COL_CX, COL_Q, COL_K, COL_V, COL_F, COL_G0, COL_G1, COL_G2 = range(10)
MOD_SHIFT1, MOD_SCALE1, MOD_GATE1, MOD_SHIFT2, MOD_SCALE2, MOD_GATE2 = range(6)

TOKEN_BLOCK = 1024
N_TOKEN_BLOCKS = T_ALL // TOKEN_BLOCK
N_CTX_BLOCKS = T_CTX // TOKEN_BLOCK
MERGE_BLOCK = 512
ROUTE_BLOCK = 256
EXPERT_BLOCK = 1024
I1_PER_BLOCK = EXPERT_BLOCK // PEER_KEYS
NOT_SELECTED = 127.0
VMEM_LIMIT = 56 * 1024 * 1024
BF16_TILE_ROWS = 16

F32 = jnp.float32
BF16 = jnp.bfloat16
NT_DIMS = (((1,), (1,)), ((), ()))


def _params(*semantics):
    return pltpu.CompilerParams(dimension_semantics=semantics, vmem_limit_bytes=VMEM_LIMIT)


def _mod_kernel(c_ref, w_ref, b_ref, o_ref):
    c = c_ref[...]
    s = c * jax.nn.sigmoid(c)
    o_ref[0] = jnp.dot(s, w_ref[0], preferred_element_type=F32,
                       precision=lax.Precision.HIGHEST) + b_ref[0]


def _modulation(cvec, w_mod, b_mod):
    tn = 1536
    ncol = N_MOD * D_MODEL
    return pl.pallas_call(
        _mod_kernel,
        grid=(DEPTH, ncol // tn),
        in_specs=[pl.BlockSpec((8, D_MODEL), lambda l, n: (0, 0)),
                  pl.BlockSpec((1, D_MODEL, tn), lambda l, n: (l, 0, n)),
                  pl.BlockSpec((1, 1, tn), lambda l, n: (l, 0, n))],
        out_specs=pl.BlockSpec((1, 8, tn), lambda l, n: (l, 0, n)),
        out_shape=jax.ShapeDtypeStruct((DEPTH, 8, ncol), F32),
        compiler_params=_params("arbitrary", "arbitrary"),
        name="adaln_table",
    )(cvec, w_mod, b_mod.reshape(DEPTH, 1, ncol))


def _rms(x, g):
    return x * lax.rsqrt(jnp.mean(x * x, axis=-1, keepdims=True) + RMS_EPS) * g


def _norm_mod_kernel(x_ref, g_ref, m_ref, o_ref):
    y = _rms(x_ref[...], g_ref[...])
    h = y * (1.0 + m_ref[MOD_SCALE1:MOD_SCALE1 + 1, :]) + m_ref[MOD_SHIFT1:MOD_SHIFT1 + 1, :]
    o_ref[...] = h.astype(BF16)


def _norm_mod(x, g, mod_tok):
    return pl.pallas_call(
        _norm_mod_kernel,
        grid=(N_TOKEN_BLOCKS,),
        in_specs=[pl.BlockSpec((TOKEN_BLOCK, D_MODEL), lambda i: (i, 0)),
                  pl.BlockSpec((1, D_MODEL), lambda i: (0, 0)),
                  pl.BlockSpec((None, N_MOD, D_MODEL), lambda i: (i, 0, 0))],
        out_specs=pl.BlockSpec((TOKEN_BLOCK, D_MODEL), lambda i: (i, 0)),
        out_shape=jax.ShapeDtypeStruct((T_ALL, D_MODEL), BF16),
        compiler_params=_params("arbitrary"),
        name="norm_modulate",
    )(x, g.reshape(1, D_MODEL), mod_tok)


def _final_norm_kernel(x_ref, g_ref, o_ref):
    o_ref[...] = _rms(x_ref[...], g_ref[...])


def _final_norm(x, g, first_block, n_blocks):
    return pl.pallas_call(
        _final_norm_kernel,
        grid=(n_blocks,),
        in_specs=[pl.BlockSpec((TOKEN_BLOCK, D_MODEL), lambda i: (i + first_block, 0)),
                  pl.BlockSpec((1, D_MODEL), lambda i: (0, 0))],
        out_specs=pl.BlockSpec((TOKEN_BLOCK, D_MODEL), lambda i: (i, 0)),
        out_shape=jax.ShapeDtypeStruct((n_blocks * TOKEN_BLOCK, D_MODEL), F32),
        compiler_params=_params("arbitrary"),
        name="final_norm",
    )(x, g.reshape(1, D_MODEL))


def _in_proj_kernel(h_ref, w_ref, *rest):
    z_ref, k_ref, v_ref, wb_scr = rest[-4:]
    n = pl.program_id(0)
    i = pl.program_id(1)

    @pl.when(i == 0)
    def _():
        wb_scr[...] = w_ref[...].astype(BF16)

    z = jnp.dot(h_ref[...], wb_scr[...], preferred_element_type=F32)
    z_ref[...] = z.astype(BF16)
    per_block = TOKEN_BLOCK // SEQ

    @pl.when((n == COL_K) & (i < N_CTX_BLOCKS))
    def _():
        k_ref[...] = z.reshape(per_block, SEQ, D_MODEL)

    @pl.when((n == COL_V) & (i < N_CTX_BLOCKS))
    def _():
        v_ref[...] = z.reshape(per_block, SEQ, D_MODEL)


def _cache_block_index(col, layer, n, i):
    last = N_CTX_BLOCKS - 1
    blk = jnp.where(n < col, 0, jnp.where(n > col, last, jnp.minimum(i, last)))
    return (blk, layer, 0, 0)


def _in_proj(h, w_in, layer, cache_k, cache_v):
    per_block = TOKEN_BLOCK // SEQ
    cache_shape = jax.ShapeDtypeStruct((BATCH, DEPTH, SEQ, D_MODEL), F32)
    cache_spec = lambda col: pl.BlockSpec(
        (per_block, None, SEQ, D_MODEL), functools.partial(_cache_block_index, col, layer))
    in_specs = [pl.BlockSpec((TOKEN_BLOCK, D_MODEL), lambda n, i: (i, 0)),
                pl.BlockSpec((None, D_MODEL, D_MODEL), lambda n, i: (layer, 0, n))]
    args = [h, w_in]
    aliases = {}
    if cache_k is not None:
        in_specs += [pl.BlockSpec(memory_space=pl.ANY)] * 2
        args += [cache_k, cache_v]
        aliases = {2: 1, 3: 2}
    return pl.pallas_call(
        _in_proj_kernel,
        grid=(IN_COLS // D_MODEL, N_TOKEN_BLOCKS),
        in_specs=in_specs,
        out_specs=[pl.BlockSpec((TOKEN_BLOCK, D_MODEL), lambda n, i: (i, n)),
                   cache_spec(COL_K), cache_spec(COL_V)],
        out_shape=[jax.ShapeDtypeStruct((T_ALL, IN_COLS), BF16), cache_shape, cache_shape],
        scratch_shapes=[pltpu.VMEM((D_MODEL, D_MODEL), BF16)],
        input_output_aliases=aliases,
        compiler_params=_params("arbitrary", "arbitrary"),
        name="in_proj",
    )(*args)


def _conv_kernel(cb_ref, cc_ref, cx_ref, w_ref, o_ref):
    i = pl.program_id(0)
    u = cc_ref[...].astype(F32) * cx_ref[...].astype(F32)
    seq = jnp.where(i < N_CTX_BLOCKS, SEQ, DEC_SEQ)
    pos = lax.broadcasted_iota(jnp.int32, (TOKEN_BLOCK, 1), 0) & (seq - 1)
    prev = jnp.where(pos == 0, 0.0, pltpu.roll(u, 1, 0))
    nxt = jnp.where(pos == seq - 1, 0.0, pltpu.roll(u, TOKEN_BLOCK - 1, 0))
    y = w_ref[0:1, :] * prev + w_ref[1:2, :] * u + w_ref[2:3, :] * nxt
    o_ref[...] = (cb_ref[...].astype(F32) * y).astype(BF16)


def _short_conv(z, conv_w):
    col = lambda c: pl.BlockSpec((TOKEN_BLOCK, D_MODEL), lambda i: (i, c))
    return pl.pallas_call(
        _conv_kernel,
        grid=(N_TOKEN_BLOCKS,),
        in_specs=[col(COL_CB), col(COL_CC), col(COL_CX),
                  pl.BlockSpec((CONV_K, D_MODEL), lambda i: (0, 0))],
        out_specs=pl.BlockSpec((TOKEN_BLOCK, D_MODEL), lambda i: (i, 0)),
        out_shape=jax.ShapeDtypeStruct((T_ALL, D_MODEL), BF16),
        compiler_params=_params("arbitrary"),
        name="short_conv",
    )(z, z, z, conv_w)


ATT_SCALE = NA_HEAD_DIM ** -0.5


def _ctx_attn_kernel(q_ref, k_ref, v_ref, o_ref):
    for h in range(NA_HEADS):
        sl = slice(h * NA_HEAD_DIM, (h + 1) * NA_HEAD_DIM)
        s = lax.dot_general(q_ref[:, sl], k_ref[:, sl], NT_DIMS,
                            preferred_element_type=F32) * ATT_SCALE
        p = jnp.exp(s - jnp.max(s, axis=-1, keepdims=True))
        o = jnp.dot(p.astype(BF16), v_ref[:, sl], preferred_element_type=F32)
        o_ref[:, sl] = (o / jnp.sum(p, axis=-1, keepdims=True)).astype(BF16)


def _ctx_attention(z):
    col = lambda c: pl.BlockSpec((SEQ, D_MODEL), lambda b: (b, c))
    return pl.pallas_call(
        _ctx_attn_kernel,
        grid=(BATCH,),
        in_specs=[col(COL_Q), col(COL_K), col(COL_V)],
        out_specs=pl.BlockSpec((SEQ, D_MODEL), lambda b: (b, 0)),
        out_shape=jax.ShapeDtypeStruct((T_ALL, D_MODEL), BF16),
        compiler_params=_params("arbitrary"),
        name="ctx_attention",
    )(z, z, z)


NA_GRID_ROWS = DEC_SEQ // GRID_W
NA_WIN_ROWS = min(NA_MAX_ROWS, NA_GRID_ROWS)


def _na_row_groups():
    groups = []
    for r in range(NA_GRID_ROWS):
        r0 = min(max(r - NA_WIN_ROWS // 2, 0), NA_GRID_ROWS - NA_WIN_ROWS)
        if groups and groups[-1][0] == r0:
            groups[-1][1].append(r)
        else:
            groups.append((r0, [r]))
    return groups


def _na_kernel(q_ref, k_ref, v_ref, ck_ref, cv_ref, b_ref, att_in_ref, o_ref):
    del att_in_ref
    ck = ck_ref[...].astype(BF16)
    cv = cv_ref[...].astype(BF16)
    for r0, q_rows in _na_row_groups():
        rows = slice(q_rows[0] * GRID_W, (q_rows[-1] + 1) * GRID_W)
        win = slice(r0 * GRID_W, (r0 + NA_WIN_ROWS) * GRID_W)
        q = q_ref[rows, :]
        bias = jnp.concatenate([b_ref[r0 - r + NA_MAX_ROWS - 1] for r in q_rows], axis=0)
        s_loc = lax.dot_general(q, k_ref[win, :], NT_DIMS, preferred_element_type=F32) * ATT_SCALE + bias
        s_ctx = lax.dot_general(q, ck, NT_DIMS, preferred_element_type=F32) * ATT_SCALE
        m = jnp.maximum(jnp.max(s_loc, axis=-1, keepdims=True), jnp.max(s_ctx, axis=-1, keepdims=True))
        p_loc = jnp.exp(s_loc - m)
        p_ctx = jnp.exp(s_ctx - m)
        den = jnp.sum(p_loc, axis=-1, keepdims=True) + jnp.sum(p_ctx, axis=-1, keepdims=True)
        o = (jnp.dot(p_loc.astype(BF16), v_ref[win, :], preferred_element_type=F32)
             + jnp.dot(p_ctx.astype(BF16), cv, preferred_element_type=F32))
        o_ref[rows, :] = (o / den).astype(BF16)


def _na_bias(rpb):
    c = np.arange(GRID_W)
    c0 = np.clip(c - NA_COLS // 2, 0, GRID_W - NA_COLS)
    in_cols = (c[None, :] >= c0[:, None]) & (c[None, :] < c0[:, None] + NA_COLS)
    dc = c[None, :] - c[:, None] + (NA_COLS - 1)
    pick_col = (dc[None] == np.arange(2 * NA_COLS - 1)[:, None, None]) & in_cols[None]
    w = np.arange(NA_WIN_ROWS)
    pick_row = np.arange(2 * NA_MAX_ROWS - 1)[:, None, None] == (w[:, None] + w[None, :])[None]
    t = jnp.einsum('lhdj,dat,jqk->lhaqtk', rpb.astype(F32), jnp.asarray(pick_row, F32),
                   jnp.asarray(pick_col, F32), precision=lax.Precision.HIGHEST)
    t = jnp.where(in_cols[None, None, None, :, None, :], t, NEG_INF)
    return t.reshape(DEPTH, NA_HEADS, NA_WIN_ROWS, GRID_W, NA_WIN_ROWS * GRID_W)


def _na_attention(z, cache_k_l, cache_v_l, bias, att, layer):
    lat0 = T_CTX // DEC_SEQ
    col = lambda c: pl.BlockSpec((DEC_SEQ, NA_HEAD_DIM), lambda h, b: (lat0 + b, c * NA_HEADS + h))
    cache = pl.BlockSpec((None, None, PAST_LEN, NA_HEAD_DIM), lambda h, b: (b, layer, 0, h))
    return pl.pallas_call(
        _na_kernel,
        grid=(NA_HEADS, DEC_BATCH),
        in_specs=[col(COL_Q), col(COL_K), col(COL_V), cache, cache,
                  pl.BlockSpec((None, None, NA_WIN_ROWS, GRID_W, NA_WIN_ROWS * GRID_W),
                               lambda h, b: (layer, h, 0, 0, 0)),
                  pl.BlockSpec(memory_space=pl.ANY)],
        out_specs=pl.BlockSpec((DEC_SEQ, NA_HEAD_DIM), lambda h, b: (lat0 + b, h)),
        out_shape=jax.ShapeDtypeStruct((T_ALL, D_MODEL), BF16),
        input_output_aliases={6: 0},
        compiler_params=_params("arbitrary", "arbitrary"),
        name="na_attention",
    )(z, z, z, cache_k_l, cache_v_l, bias, att)


def _dft_matrices(n):
    j = np.arange(n)
    ang = 2.0 * np.pi * ((j[:, None] * j[None, :]) % n) / n
    return np.cos(ang) / math.sqrt(n), np.sin(ang) / math.sqrt(n)


def _fourier_kernel(cs_ref, ss_ref, f_ref, w2_ref, *rest):
    o_ref = rest[-1]
    f = f_ref[...]
    cu = jnp.dot(cs_ref[...], f, preferred_element_type=F32).astype(BF16)
    su = jnp.dot(ss_ref[...], f, preferred_element_type=F32).astype(BF16)
    for g in range(FNET_GROUPS):
        sl = slice(g * FNET_GROUP_DIM, (g + 1) * FNET_GROUP_DIM)
        lhs = jnp.concatenate([cu[:, sl], su[:, sl]], axis=1)
        o_ref[:, sl] = jnp.dot(lhs, w2_ref[...], preferred_element_type=F32).astype(BF16)


FOURIER_ROWS = 256


def _fourier(z, seq, n_batch, first_row, prev):
    cs, ss = _dft_matrices(seq)
    cc, sc = _dft_matrices(FNET_GROUP_DIM)
    w2 = jnp.asarray(np.concatenate([cc, -sc], axis=0), F32).astype(BF16)
    rows = FOURIER_ROWS
    nr = seq // rows
    in_specs = [pl.BlockSpec((rows, seq), lambda b, r: (r, 0)),
                pl.BlockSpec((rows, seq), lambda b, r: (r, 0)),
                pl.BlockSpec((seq, D_MODEL), lambda b, r: (first_row // seq + b, COL_F)),
                pl.BlockSpec((2 * FNET_GROUP_DIM, FNET_GROUP_DIM), lambda b, r: (0, 0))]
    args = [jnp.asarray(cs, F32).astype(BF16), jnp.asarray(ss, F32).astype(BF16), z, w2]
    aliases = {}
    if prev is not None:
        in_specs.append(pl.BlockSpec(memory_space=pl.ANY))
        args.append(prev)
        aliases = {4: 0}
    return pl.pallas_call(
        _fourier_kernel,
        grid=(n_batch, nr),
        in_specs=in_specs,
        out_specs=pl.BlockSpec((rows, D_MODEL), lambda b, r: (first_row // rows + b * nr + r, 0)),
        out_shape=jax.ShapeDtypeStruct((T_ALL, D_MODEL), BF16),
        input_output_aliases=aliases,
        compiler_params=_params("arbitrary", "arbitrary"),
        name="fourier_mix",
    )(*args)


def _merge_kernel(uc_ref, at_ref, fo_ref, g0_ref, g1_ref, g2_ref, x_ref, wc_ref, wa_ref, wf_ref,
                  wo_ref, m_ref, gf_ref, xo_ref, h2_ref, h2t_ref):
    yc = jnp.dot(uc_ref[...], wc_ref[...], preferred_element_type=F32)
    ya = jnp.dot(at_ref[...], wa_ref[...], preferred_element_type=F32)
    yf = jnp.dot(fo_ref[...], wf_ref[...], preferred_element_type=F32)
    gate = lambda r: jax.nn.sigmoid(r[...].astype(F32))
    merged = gate(g0_ref) * yc + gate(g1_ref) * ya + gate(g2_ref) * yf
    y = jnp.dot(merged.astype(BF16), wo_ref[...], preferred_element_type=F32)
    x = x_ref[...] + m_ref[MOD_GATE1:MOD_GATE1 + 1, :] * y
    xo_ref[...] = x
    hn = _rms(x, gf_ref[...])
    h2 = hn * (1.0 + m_ref[MOD_SCALE2:MOD_SCALE2 + 1, :]) + m_ref[MOD_SHIFT2:MOD_SHIFT2 + 1, :]
    h2_ref[...] = h2.astype(BF16)
    h2t_ref[...] = h2.T.astype(BF16)


def _merge(uconv, att, four, z, x, wc, wa, wf, wo, mod_tok, g_ffn):
    per = TOKEN_BLOCK // MERGE_BLOCK
    row = pl.BlockSpec((MERGE_BLOCK, D_MODEL), lambda i: (i, 0))
    col = lambda c: pl.BlockSpec((MERGE_BLOCK, D_MODEL), lambda i: (i, c))
    wspec = pl.BlockSpec((D_MODEL, D_MODEL), lambda i: (0, 0))
    return pl.pallas_call(
        _merge_kernel,
        grid=(T_ALL // MERGE_BLOCK,),
        in_specs=[row, row, row, col(COL_G0), col(COL_G1), col(COL_G2), row,
                  wspec, wspec, wspec, wspec,
                  pl.BlockSpec((None, N_MOD, D_MODEL), lambda i: (i // per, 0, 0)),
                  pl.BlockSpec((1, D_MODEL), lambda i: (0, 0))],
        out_specs=[row, row, pl.BlockSpec((D_MODEL, MERGE_BLOCK), lambda i: (0, i))],
        out_shape=[jax.ShapeDtypeStruct((T_ALL, D_MODEL), F32),
                   jax.ShapeDtypeStruct((T_ALL, D_MODEL), BF16),
                   jax.ShapeDtypeStruct((D_MODEL, T_ALL), BF16)],
        compiler_params=_params("arbitrary"),
        name="branch_merge",
    )(uconv, att, four, z, z, z, x, wc, wa, wf, wo, mod_tok, g_ffn.reshape(1, D_MODEL))


def _candidate_cells():
    return [(a, b) for a in range(PEER_TOPK) for b in range(PEER_TOPK) if (a + 1) * (b + 1) <= PEER_TOPK]


def _extract_topk(p, orig_scr, s_scr, rank_scr, val_scr, break_ties):
    shape = (PEER_HEADS, PEER_KEYS, ROUTE_BLOCK)
    s_scr[...] = orig_scr[p]
    rank_scr[p] = jnp.full(shape, NOT_SELECTED, F32)
    for k in range(PEER_TOPK):
        cur = s_scr[...]
        m = jnp.max(cur, axis=1, keepdims=True)
        sel = cur == m
        if break_ties:
            key_iota = lax.broadcasted_iota(jnp.int32, shape, 1).astype(F32)
            first = jnp.min(jnp.where(sel, key_iota, float(PEER_KEYS)), axis=1, keepdims=True)
            sel = key_iota == first
        s_scr[...] = jnp.where(sel, -jnp.inf, cur)
        rank_scr[p] = jnp.where(sel, float(k), rank_scr[p])
        for h in range(PEER_HEADS):
            val_scr[p, k, h:h + 1, :] = m[h]


def _route_kernel(h_ref, wpq_ref, keys_ref, r2_ref, e2_ref, n1_ref, c1_ref,
                  s_scr, orig_scr, rank_scr, val_scr, n_scr):
    tb = ROUTE_BLOCK
    q = jnp.dot(h_ref[...], wpq_ref[...], preferred_element_type=F32).astype(BF16)

    for p in range(2):
        piece = lambda h: slice((2 * h + p) * PEER_KEY_DIM, (2 * h + p + 1) * PEER_KEY_DIM)
        qp = jnp.concatenate([q[:, piece(h)] for h in range(PEER_HEADS)], axis=1)
        s = lax.dot_general(keys_ref[p], qp, NT_DIMS, preferred_element_type=F32)
        orig_scr[p] = s.reshape(PEER_HEADS, PEER_KEYS, tb)

    off = 0.0
    for p in range(2):
        _extract_topk(p, orig_scr, s_scr, rank_scr, val_scr, break_ties=False)
        taken = jnp.sum(jnp.where(rank_scr[p] < float(PEER_TOPK), 1.0, 0.0), axis=1)
        off = jnp.maximum(off, jnp.max(jnp.abs(taken - float(PEER_TOPK))))

    @pl.when(off > 0.0)
    def _():
        for p in range(2):
            _extract_topk(p, orig_scr, s_scr, rank_scr, val_scr, break_ties=True)

    v1 = [val_scr[0, a] for a in range(PEER_TOPK)]
    v2 = [val_scr[1, b] for b in range(PEER_TOPK)]
    cells = _candidate_cells()
    sums = {c: v1[c[0]] + v2[c[1]] for c in cells}
    undecided = lambda c, d: not (d[0] >= c[0] and d[1] >= c[1])
    before = {c: float((c[0] + 1) * (c[1] + 1) - 1 + sum(undecided(c, d) for d in cells[ci + 1:]))
              for ci, c in enumerate(cells)}
    for ci, c in enumerate(cells):
        for d in cells[ci + 1:]:
            if undecided(c, d):
                first = jnp.where(sums[c] >= sums[d], 1.0, 0.0)
                before[d] = before[d] + first
                before[c] = before[c] - first
    e1 = [jnp.exp(v1[a] - v1[0]) for a in range(PEER_TOPK)]
    e2 = [jnp.exp(v2[b] - v2[0]) for b in range(PEER_TOPK)]
    zsum = 0.0
    count = [0.0] * PEER_TOPK
    for c in cells:
        chosen = before[c] < float(PEER_TOPK)
        count[c[0]] = count[c[0]] + jnp.where(chosen, 1.0, 0.0)
        zsum = zsum + jnp.where(chosen, e1[c[0]] * e2[c[1]], 0.0)
    for a in range(PEER_TOPK):
        n_scr[a] = count[a]
    n_scr[PEER_TOPK] = 0.5 / zsum

    for h in range(PEER_HEADS):
        row = pl.ds(h, 1)
        rank1 = rank_scr[0, h]
        n1 = jnp.zeros((PEER_KEYS, tb), F32)
        for a in range(PEER_TOPK):
            n1 = jnp.where(rank1 == float(a), n_scr[a, row, :], n1)
        n1_ref[h] = n1
        w1 = jnp.where(rank1 < float(PEER_TOPK), jnp.exp(orig_scr[0, h] - val_scr[0, 0, row, :]), 0.0)
        c1_ref[h] = w1 * n_scr[PEER_TOPK, row, :]
        rank2 = rank_scr[1, h]
        r2_ref[h] = rank2.astype(BF16)
        w2 = jnp.where(rank2 < float(PEER_TOPK), jnp.exp(orig_scr[1, h] - val_scr[1, 0, row, :]), 0.0)
        e2_ref[h] = w2.astype(BF16)


def _route(h2, wpq, keys, layer):
    tb = ROUTE_BLOCK
    out = lambda dt: jax.ShapeDtypeStruct((PEER_HEADS, PEER_KEYS, T_ALL), dt)
    ospec = pl.BlockSpec((PEER_HEADS, PEER_KEYS, tb), lambda i: (0, 0, i))
    hk = PEER_HEADS * PEER_KEYS
    return pl.pallas_call(
        _route_kernel,
        grid=(T_ALL // tb,),
        in_specs=[pl.BlockSpec((tb, D_MODEL), lambda i: (i, 0)),
                  pl.BlockSpec((None, D_MODEL, 2 * hk), lambda i: (layer, 0, 0)),
                  pl.BlockSpec((None, 2, hk, hk), lambda i: (layer, 0, 0, 0))],
        out_specs=[ospec, ospec, ospec, ospec],
        out_shape=[out(BF16), out(BF16), out(F32), out(F32)],
        scratch_shapes=[pltpu.VMEM((PEER_HEADS, PEER_KEYS, tb), F32),
                        pltpu.VMEM((2, PEER_HEADS, PEER_KEYS, tb), F32),
                        pltpu.VMEM((2, PEER_HEADS, PEER_KEYS, tb), F32),
                        pltpu.VMEM((2, PEER_TOPK, PEER_HEADS, tb), F32),
                        pltpu.VMEM((PEER_TOPK + 1, PEER_HEADS, tb), F32)],
        compiler_params=_params("arbitrary"),
        name="peer_route",
    )(h2, wpq, keys)


SQRT_HALF = math.sqrt(0.5)


EXPERT_TOKEN_CHUNK = 1024
N_EXPERT_CHUNKS = TOKEN_BLOCK // EXPERT_TOKEN_CHUNK


def _experts_kernel(xt_ref, u_ref, vt_ref, r2_ref, e2_ref, n1_ref, c1_ref, res_ref, m_ref,
                    o_ref, acc_scr, act_scr, p_scr):
    eb = pl.program_id(1)

    @pl.when(eb == 0)
    def _():
        acc_scr[...] = jnp.zeros_like(acc_scr)

    cols = lambda j: slice(j * EXPERT_TOKEN_CHUNK, (j + 1) * EXPERT_TOKEN_CHUNK)
    def pre_activate(j):
        act_scr[j % 2] = jnp.dot(u_ref[...], xt_ref[:, cols(j)], preferred_element_type=F32)

    def activate(j):
        for c in range(I1_PER_BLOCK):
            i1 = pl.ds(eb * I1_PER_BLOCK + c, 1)
            tiles = (PEER_KEYS // BF16_TILE_ROWS, BF16_TILE_ROWS, EXPERT_TOKEN_CHUNK)
            tile_row = lambda ref, h: jnp.broadcast_to(ref[h, i1, cols(j)], tiles[1:]).astype(BF16)[None]
            gate = jnp.zeros(tiles, BF16)
            for h in range(PEER_HEADS):
                taken = r2_ref[h, :, cols(j)].reshape(tiles) < tile_row(n1_ref, h)
                e2 = e2_ref[h, :, cols(j)].reshape(tiles)
                gate = gate + jnp.where(taken, e2, jnp.zeros((), BF16)) * tile_row(c1_ref, h)
            keys = slice(c * PEER_KEYS, (c + 1) * PEER_KEYS)
            a = act_scr[j % 2, keys, :]
            erf1 = 1.0 + lax.erf(a * SQRT_HALF)
            p_scr[keys, cols(j)] = (a.astype(BF16) * erf1.astype(BF16)
                                    * gate.reshape(PEER_KEYS, EXPERT_TOKEN_CHUNK))

    def mix(j):
        acc_scr[:, cols(j)] += jnp.dot(vt_ref[...], p_scr[:, cols(j)], preferred_element_type=F32)

    for stage in range(N_EXPERT_CHUNKS + 2):
        if stage < N_EXPERT_CHUNKS:
            pre_activate(stage)
        if 1 <= stage <= N_EXPERT_CHUNKS:
            activate(stage - 1)
        if stage >= 2:
            mix(stage - 2)

    @pl.when(eb == pl.num_programs(1) - 1)
    def _():
        o_ref[...] = res_ref[...] + m_ref[MOD_GATE2:MOD_GATE2 + 1, :] * acc_scr[...].T


def _experts(h2t, u_bf, vt_bf, r2, e2, n1, c1, x_mid, mod_tok, layer):
    once = pl.Buffered(1)
    tok = pl.BlockSpec((PEER_HEADS, PEER_KEYS, TOKEN_BLOCK), lambda t, e: (0, 0, t), pipeline_mode=once)
    return pl.pallas_call(
        _experts_kernel,
        grid=(N_TOKEN_BLOCKS, PEER_EXPERTS // EXPERT_BLOCK),
        in_specs=[pl.BlockSpec((D_MODEL, TOKEN_BLOCK), lambda t, e: (0, t)),
                  pl.BlockSpec((None, EXPERT_BLOCK, D_MODEL), lambda t, e: (layer, e, 0)),
                  pl.BlockSpec((None, D_MODEL, EXPERT_BLOCK), lambda t, e: (layer, 0, e)),
                  tok, tok, tok, tok,
                  pl.BlockSpec((TOKEN_BLOCK, D_MODEL), lambda t, e: (t, 0), pipeline_mode=once),
                  pl.BlockSpec((None, N_MOD, D_MODEL), lambda t, e: (t, 0, 0))],
        out_specs=pl.BlockSpec((TOKEN_BLOCK, D_MODEL), lambda t, e: (t, 0)),
        out_shape=jax.ShapeDtypeStruct((T_ALL, D_MODEL), F32),
        scratch_shapes=[pltpu.VMEM((D_MODEL, TOKEN_BLOCK), F32),
                        pltpu.VMEM((2, EXPERT_BLOCK, EXPERT_TOKEN_CHUNK), F32),
                        pltpu.VMEM((EXPERT_BLOCK, TOKEN_BLOCK), BF16)],
        compiler_params=_params("arbitrary", "arbitrary"),
        name="peer_experts",
    )(h2t, u_bf, vt_bf, r2, e2, n1, c1, x_mid, mod_tok)


TRANSPOSE_ROWS = 512


def _transpose_cast_kernel(v_ref, o_ref):
    o_ref[...] = v_ref[...].T.astype(BF16)


def _transpose_cast(peer_v):
    return pl.pallas_call(
        _transpose_cast_kernel,
        grid=(DEPTH, PEER_EXPERTS // TRANSPOSE_ROWS),
        in_specs=[pl.BlockSpec((None, TRANSPOSE_ROWS, D_MODEL), lambda l, e: (l, e, 0))],
        out_specs=pl.BlockSpec((None, D_MODEL, TRANSPOSE_ROWS), lambda l, e: (l, 0, e)),
        out_shape=jax.ShapeDtypeStruct((DEPTH, D_MODEL, PEER_EXPERTS), BF16),
        compiler_params=_params("arbitrary", "arbitrary"),
        name="expert_table_transpose",
    )(peer_v)


def _block_diag_keys(sub_keys):
    eye = jnp.eye(PEER_HEADS, dtype=sub_keys.dtype)
    keys = jnp.einsum('lhpjd,hg->lphjgd', sub_keys, eye)
    return keys.reshape(DEPTH, 2, PEER_HEADS * PEER_KEYS, PEER_HEADS * PEER_KEY_DIM).astype(BF16)


def kernel(x_prompt, x_sample, cache_k, cache_v, c, c_ctx, w_in, conv_w, w_conv_out, rpb, w_attn_out,
           w_four_out, w_o, g_mix, g_ffn, w_mod, b_mod, w_pq, sub_keys, peer_u, peer_v, g_final):
    assert SEQ & (SEQ - 1) == 0 and DEC_SEQ & (DEC_SEQ - 1) == 0
    x = jnp.concatenate([x_prompt.reshape(T_CTX, D_MODEL), x_sample.reshape(T_LAT, D_MODEL)], axis=0)
    cvec = jnp.zeros((8, D_MODEL), F32).at[0].set(c_ctx).at[1:1 + DEC_BATCH].set(c)
    mod = _modulation(cvec, w_mod, b_mod)
    lat_per_block = DEC_SEQ // TOKEN_BLOCK
    block_row = np.array([0] * N_CTX_BLOCKS
                         + [1 + b for b in range(DEC_BATCH) for _ in range(lat_per_block)])
    mod_tok = mod[:, block_row].reshape(DEPTH, N_TOKEN_BLOCKS, N_MOD, D_MODEL)
    cache_k4 = cache_k.reshape(DEC_BATCH, DEPTH, PAST_LEN, D_MODEL)
    cache_v4 = cache_v.reshape(DEC_BATCH, DEPTH, PAST_LEN, D_MODEL)

    na_bias = _na_bias(rpb)
    keys = _block_diag_keys(sub_keys)
    w_pq_bf = w_pq.astype(BF16)
    u_bf = peer_u.astype(BF16)
    vt_bf = _transpose_cast(peer_v)

    new_k = new_v = None
    for l in range(DEPTH):
        h = _norm_mod(x, g_mix[l], mod_tok[l])
        z, new_k, new_v = _in_proj(h, w_in, l, new_k, new_v)
        uconv = _short_conv(z, conv_w[l])
        att = _ctx_attention(z)
        att = _na_attention(z, cache_k4, cache_v4, na_bias, att, l)
        four = _fourier(z, SEQ, BATCH, 0, None)
        four = _fourier(z, DEC_SEQ, DEC_BATCH, T_CTX, four)
        x_mid, h2, h2t = _merge(uconv, att, four, z, x,
                           w_conv_out[l].astype(BF16), w_attn_out[l].astype(BF16),
                           w_four_out[l].astype(BF16), w_o[l].astype(BF16), mod_tok[l], g_ffn[l])
        r2, e2, n1, c1 = _route(h2, w_pq_bf, keys, l)
        x = _experts(h2t, u_bf, vt_bf, r2, e2, n1, c1, x_mid, mod_tok[l], l)

    y_prompt = _final_norm(x, g_final, 0, N_CTX_BLOCKS).reshape(BATCH, SEQ, D_MODEL)
    y_sample = _final_norm(x, g_final, N_CTX_BLOCKS, N_TOKEN_BLOCKS - N_CTX_BLOCKS)
    y_sample = y_sample.reshape(DEC_BATCH, DEC_SEQ, D_MODEL)
    shape5 = (BATCH, DEPTH, SEQ, NA_HEADS, NA_HEAD_DIM)
    return (y_prompt, y_sample, new_k.reshape(shape5), new_v.reshape(shape5))
```

```python
import functools
import math

import numpy as np
import jax
import jax.numpy as jnp
from jax import lax
from jax.experimental import pallas as pl
from jax.experimental.pallas import tpu as pltpu

D_MODEL = 1024
BATCH = 16
SEQ = 256
DEPTH = 2
DEC_BATCH = 2
DEC_SEQ = 1024
PAST_LEN = 512
GRID_W = 64
CONV_K = 3
NA_HEADS = 8
NA_HEAD_DIM = D_MODEL // NA_HEADS
NA_MAX_ROWS = 8
NA_COLS = 16
FNET_GROUPS = 4
FNET_GROUP_DIM = D_MODEL // FNET_GROUPS
N_BRANCH = 3
IN_COLS = 10 * D_MODEL
PEER_HEADS = 8
PEER_KEYS = 128
PEER_EXPERTS = PEER_KEYS * PEER_KEYS
PEER_KEY_DIM = 128
PEER_TOPK = 16
N_MOD = 6
RMS_EPS = 1e-6
NEG_INF = -1e30

T_CTX = BATCH * SEQ
T_LAT = DEC_BATCH * DEC_SEQ
T_ALL = T_CTX + T_LAT

COL_CB, COL_CC, COL_CX, COL_Q, COL_K, COL_V, COL_F, COL_G0, COL_G1, COL_G2 = range(10)
MOD_SHIFT1, MOD_SCALE1, MOD_GATE1, MOD_SHIFT2, MOD_SCALE2, MOD_GATE2 = range(6)

TOKEN_BLOCK = 1024
N_TOKEN_BLOCKS = T_ALL // TOKEN_BLOCK
N_CTX_BLOCKS = T_CTX // TOKEN_BLOCK
MERGE_BLOCK = 512
ROUTE_BLOCK = 256
EXPERT_BLOCK = 1024
I1_PER_BLOCK = EXPERT_BLOCK // PEER_KEYS
NOT_SELECTED = 127.0
VMEM_LIMIT = 56 * 1024 * 1024
BF16_TILE_ROWS = 16

F32 = jnp.float32
BF16 = jnp.bfloat16
NT_DIMS = (((1,), (1,)), ((), ()))


def _params(*semantics):
    return pltpu.CompilerParams(dimension_semantics=semantics, vmem_limit_bytes=VMEM_LIMIT)


def _mod_kernel(c_ref, w_ref, b_ref, o_ref):
    c = c_ref[...]
    s = c * jax.nn.sigmoid(c)
    o_ref[0] = jnp.dot(s, w_ref[0], preferred_element_type=F32,
                       precision=lax.Precision.HIGHEST) + b_ref[0]


def _modulation(cvec, w_mod, b_mod):
    tn = 1536
    ncol = N_MOD * D_MODEL
    return pl.pallas_call(
        _mod_kernel,
        grid=(DEPTH, ncol // tn),
        in_specs=[pl.BlockSpec((8, D_MODEL), lambda l, n: (0, 0)),
                  pl.BlockSpec((1, D_MODEL, tn), lambda l, n: (l, 0, n)),
                  pl.BlockSpec((1, 1, tn), lambda l, n: (l, 0, n))],
        out_specs=pl.BlockSpec((1, 8, tn), lambda l, n: (l, 0, n)),
        out_shape=jax.ShapeDtypeStruct((DEPTH, 8, ncol), F32),
        compiler_params=_params("arbitrary", "arbitrary"),
        name="adaln_table",
    )(cvec, w_mod, b_mod.reshape(DEPTH, 1, ncol))


def _rms(x, g):
    return x * lax.rsqrt(jnp.mean(x * x, axis=-1, keepdims=True) + RMS_EPS) * g


def _norm_mod_kernel(x_ref, g_ref, m_ref, o_ref):
    y = _rms(x_ref[...], g_ref[...])
    h = y * (1.0 + m_ref[MOD_SCALE1:MOD_SCALE1 + 1, :]) + m_ref[MOD_SHIFT1:MOD_SHIFT1 + 1, :]
    o_ref[...] = h.astype(BF16)


def _norm_mod(x, g, mod_tok):
    return pl.pallas_call(
        _norm_mod_kernel,
        grid=(N_TOKEN_BLOCKS,),
        in_specs=[pl.BlockSpec((TOKEN_BLOCK, D_MODEL), lambda i: (i, 0)),
                  pl.BlockSpec((1, D_MODEL), lambda i: (0, 0)),
                  pl.BlockSpec((None, N_MOD, D_MODEL), lambda i: (i, 0, 0))],
        out_specs=pl.BlockSpec((TOKEN_BLOCK, D_MODEL), lambda i: (i, 0)),
        out_shape=jax.ShapeDtypeStruct((T_ALL, D_MODEL), BF16),
        compiler_params=_params("arbitrary"),
        name="norm_modulate",
    )(x, g.reshape(1, D_MODEL), mod_tok)


def _final_norm_kernel(x_ref, g_ref, o_ref):
    o_ref[...] = _rms(x_ref[...], g_ref[...])


def _final_norm(x, g, first_block, n_blocks):
    return pl.pallas_call(
        _final_norm_kernel,
        grid=(n_blocks,),
        in_specs=[pl.BlockSpec((TOKEN_BLOCK, D_MODEL), lambda i: (i + first_block, 0)),
                  pl.BlockSpec((1, D_MODEL), lambda i: (0, 0))],
        out_specs=pl.BlockSpec((TOKEN_BLOCK, D_MODEL), lambda i: (i, 0)),
        out_shape=jax.ShapeDtypeStruct((n_blocks * TOKEN_BLOCK, D_MODEL), F32),
        compiler_params=_params("arbitrary"),
        name="final_norm",
    )(x, g.reshape(1, D_MODEL))


def _in_proj_kernel(h_ref, w_ref, *rest):
    z_ref, k_ref, v_ref, wb_scr = rest[-4:]
    n = pl.program_id(0)
    i = pl.program_id(1)

    @pl.when(i == 0)
    def _():
        wb_scr[...] = w_ref[...].astype(BF16)

    z = jnp.dot(h_ref[...], wb_scr[...], preferred_element_type=F32)
    z_ref[...] = z.astype(BF16)
    per_block = TOKEN_BLOCK // SEQ

    @pl.when((n == COL_K) & (i < N_CTX_BLOCKS))
    def _():
        k_ref[...] = z.reshape(per_block, SEQ, NA_HEADS, NA_HEAD_DIM)

    @pl.when((n == COL_V) & (i < N_CTX_BLOCKS))
    def _():
        v_ref[...] = z.reshape(per_block, SEQ, NA_HEADS, NA_HEAD_DIM)


def _cache_block_index(col, layer, n, i):
    last = N_CTX_BLOCKS - 1
    blk = jnp.where(n < col, 0, jnp.where(n > col, last, jnp.minimum(i, last)))
    return (blk, layer, 0, 0, 0)


def _in_proj(h, w_in, layer, cache_k, cache_v):
    per_block = TOKEN_BLOCK // SEQ
    cache_shape = jax.ShapeDtypeStruct((BATCH, DEPTH, SEQ, NA_HEADS, NA_HEAD_DIM), F32)
    cache_spec = lambda col: pl.BlockSpec(
        (per_block, None, SEQ, NA_HEADS, NA_HEAD_DIM), functools.partial(_cache_block_index, col, layer))
    in_specs = [pl.BlockSpec((TOKEN_BLOCK, D_MODEL), lambda n, i: (i, 0)),
                pl.BlockSpec((None, D_MODEL, D_MODEL), lambda n, i: (layer, 0, n))]
    args = [h, w_in]
    aliases = {}
    if cache_k is not None:
        in_specs += [pl.BlockSpec(memory_space=pl.ANY)] * 2
        args += [cache_k, cache_v]
        aliases = {2: 1, 3: 2}
    return pl.pallas_call(
        _in_proj_kernel,
        grid=(IN_COLS // D_MODEL, N_TOKEN_BLOCKS),
        in_specs=in_specs,
        out_specs=[pl.BlockSpec((TOKEN_BLOCK, D_MODEL), lambda n, i: (i, n)),
                   cache_spec(COL_K), cache_spec(COL_V)],
        out_shape=[jax.ShapeDtypeStruct((T_ALL, IN_COLS), BF16), cache_shape, cache_shape],
        scratch_shapes=[pltpu.VMEM((D_MODEL, D_MODEL), BF16)],
        input_output_aliases=aliases,
        compiler_params=_params("arbitrary", "arbitrary"),
        name="in_proj",
    )(*args)


def _conv_kernel(cb_ref, cc_ref, cx_ref, w_ref, o_ref):
    i = pl.program_id(0)
    u = cc_ref[...].astype(F32) * cx_ref[...].astype(F32)
    seq = jnp.where(i < N_CTX_BLOCKS, SEQ, DEC_SEQ)
    pos = lax.broadcasted_iota(jnp.int32, (TOKEN_BLOCK, 1), 0) & (seq - 1)
    prev = jnp.where(pos == 0, 0.0, pltpu.roll(u, 1, 0))
    nxt = jnp.where(pos == seq - 1, 0.0, pltpu.roll(u, TOKEN_BLOCK - 1, 0))
    y = w_ref[0:1, :] * prev + w_ref[1:2, :] * u + w_ref[2:3, :] * nxt
    o_ref[...] = (cb_ref[...].astype(F32) * y).astype(BF16)


def _short_conv(z, conv_w):
    col = lambda c: pl.BlockSpec((TOKEN_BLOCK, D_MODEL), lambda i: (i, c))
    return pl.pallas_call(
        _conv_kernel,
        grid=(N_TOKEN_BLOCKS,),
        in_specs=[col(COL_CB), col(COL_CC), col(COL_CX),
                  pl.BlockSpec((CONV_K, D_MODEL), lambda i: (0, 0))],
        out_specs=pl.BlockSpec((TOKEN_BLOCK, D_MODEL), lambda i: (i, 0)),
        out_shape=jax.ShapeDtypeStruct((T_ALL, D_MODEL), BF16),
        compiler_params=_params("arbitrary"),
        name="short_conv",
    )(z, z, z, conv_w)


ATT_SCALE = NA_HEAD_DIM ** -0.5


def _ctx_attn_kernel(q_ref, k_ref, v_ref, o_ref):
    for h in range(NA_HEADS):
        sl = slice(h * NA_HEAD_DIM, (h + 1) * NA_HEAD_DIM)
        s = lax.dot_general(q_ref[:, sl], k_ref[:, sl], NT_DIMS,
                            preferred_element_type=F32) * ATT_SCALE
        p = jnp.exp(s - jnp.max(s, axis=-1, keepdims=True))
        o = jnp.dot(p.astype(BF16), v_ref[:, sl], preferred_element_type=F32)
        o_ref[:, sl] = (o / jnp.sum(p, axis=-1, keepdims=True)).astype(BF16)


def _ctx_attention(z):
    col = lambda c: pl.BlockSpec((SEQ, D_MODEL), lambda b: (b, c))
    return pl.pallas_call(
        _ctx_attn_kernel,
        grid=(BATCH,),
        in_specs=[col(COL_Q), col(COL_K), col(COL_V)],
        out_specs=pl.BlockSpec((SEQ, D_MODEL), lambda b: (b, 0)),
        out_shape=jax.ShapeDtypeStruct((T_ALL, D_MODEL), BF16),
        compiler_params=_params("arbitrary"),
        name="ctx_attention",
    )(z, z, z)


NA_GRID_ROWS = DEC_SEQ // GRID_W
NA_WIN_ROWS = min(NA_MAX_ROWS, NA_GRID_ROWS)


def _na_row_groups():
    groups = []
    for r in range(NA_GRID_ROWS):
        r0 = min(max(r - NA_WIN_ROWS // 2, 0), NA_GRID_ROWS - NA_WIN_ROWS)
        if groups and groups[-1][0] == r0:
            groups[-1][1].append(r)
        else:
            groups.append((r0, [r]))
    return groups


def _na_kernel(q_ref, k_ref, v_ref, ck_ref, cv_ref, b_ref, att_in_ref, o_ref):
    del att_in_ref
    pairs = NA_WIN_ROWS // 2
    ck = ck_ref[...].astype(BF16)
    cv = cv_ref[...].astype(BF16)
    for r0, q_rows in _na_row_groups():
        rows = slice(q_rows[0] * GRID_W, (q_rows[-1] + 1) * GRID_W)
        win = slice(r0 * GRID_W, (r0 + NA_WIN_ROWS) * GRID_W)
        q = q_ref[rows, :]
        slab = lambda d: jnp.concatenate([b_ref[d + 2 * m] for m in range(pairs)], axis=1)
        bias = jnp.concatenate([slab(r0 - r + NA_MAX_ROWS - 1) for r in q_rows], axis=0)
        s_loc = lax.dot_general(q, k_ref[win, :], NT_DIMS, preferred_element_type=F32) * ATT_SCALE + bias
        s_ctx = lax.dot_general(q, ck, NT_DIMS, preferred_element_type=F32) * ATT_SCALE
        m = jnp.maximum(jnp.max(s_loc, axis=-1, keepdims=True), jnp.max(s_ctx, axis=-1, keepdims=True))
        p_loc = jnp.exp(s_loc - m)
        p_ctx = jnp.exp(s_ctx - m)
        den = jnp.sum(p_loc, axis=-1, keepdims=True) + jnp.sum(p_ctx, axis=-1, keepdims=True)
        o = (jnp.dot(p_loc.astype(BF16), v_ref[win, :], preferred_element_type=F32)
             + jnp.dot(p_ctx.astype(BF16), cv, preferred_element_type=F32))
        o_ref[rows, :] = (o / den).astype(BF16)


NA_PAIR_TABLES = 2 * NA_MAX_ROWS - 2


def _na_bias(rpb):
    assert NA_WIN_ROWS % 2 == 0
    c = np.arange(GRID_W)
    c0 = np.clip(c - NA_COLS // 2, 0, GRID_W - NA_COLS)
    in_cols = (c[None, :] >= c0[:, None]) & (c[None, :] < c0[:, None] + NA_COLS)
    dc = c[None, :] - c[:, None] + (NA_COLS - 1)
    pick = (dc[None] == np.arange(2 * NA_COLS - 1)[:, None, None]) & in_cols[None]
    t = jnp.einsum('lhdj,jqk->lhdqk', rpb.astype(F32), jnp.asarray(pick, F32), precision=lax.Precision.HIGHEST)
    t = jnp.where(in_cols, t, NEG_INF)
    return jnp.concatenate([t[:, :, :-1], t[:, :, 1:]], axis=-1)


def _na_attention(z, cache_k, cache_v, bias, att, layer):
    lat0 = T_CTX // DEC_SEQ
    col = lambda c: pl.BlockSpec((DEC_SEQ, NA_HEAD_DIM), lambda h, b: (lat0 + b, c * NA_HEADS + h))
    cache = pl.BlockSpec((None, None, PAST_LEN, NA_HEAD_DIM), lambda h, b: (b, layer, 0, h))
    return pl.pallas_call(
        _na_kernel,
        grid=(NA_HEADS, DEC_BATCH),
        in_specs=[col(COL_Q), col(COL_K), col(COL_V), cache, cache,
                  pl.BlockSpec((None, None, NA_PAIR_TABLES, GRID_W, 2 * GRID_W), lambda h, b: (layer, h, 0, 0, 0)),
                  pl.BlockSpec(memory_space=pl.ANY)],
        out_specs=pl.BlockSpec((DEC_SEQ, NA_HEAD_DIM), lambda h, b: (lat0 + b, h)),
        out_shape=jax.ShapeDtypeStruct((T_ALL, D_MODEL), BF16),
        input_output_aliases={6: 0},
        compiler_params=_params("arbitrary", "arbitrary"),
        name="na_attention",
    )(z, z, z, cache_k, cache_v, bias, att)


def _dft_matrices(n):
    j = np.arange(n)
    ang = 2.0 * np.pi * ((j[:, None] * j[None, :]) % n) / n
    return np.cos(ang) / math.sqrt(n), np.sin(ang) / math.sqrt(n)


def _fourier_kernel(cs_ref, ss_ref, f_ref, w2_ref, *rest):
    o_ref = rest[-1]
    f = f_ref[...]
    cu = jnp.dot(cs_ref[...], f, preferred_element_type=F32).astype(BF16)
    su = jnp.dot(ss_ref[...], f, preferred_element_type=F32).astype(BF16)
    for g in range(FNET_GROUPS):
        sl = slice(g * FNET_GROUP_DIM, (g + 1) * FNET_GROUP_DIM)
        lhs = jnp.concatenate([cu[:, sl], su[:, sl]], axis=1)
        o_ref[:, sl] = jnp.dot(lhs, w2_ref[...], preferred_element_type=F32).astype(BF16)


FOURIER_ROWS = 256


def _fourier(z, seq, n_batch, first_row, prev):
    cs, ss = _dft_matrices(seq)
    cc, sc = _dft_matrices(FNET_GROUP_DIM)
    w2 = jnp.asarray(np.concatenate([cc, -sc], axis=0), F32).astype(BF16)
    rows = FOURIER_ROWS
    nr = seq // rows
    in_specs = [pl.BlockSpec((rows, seq), lambda b, r: (r, 0)),
                pl.BlockSpec((rows, seq), lambda b, r: (r, 0)),
                pl.BlockSpec((seq, D_MODEL), lambda b, r: (first_row // seq + b, COL_F)),
                pl.BlockSpec((2 * FNET_GROUP_DIM, FNET_GROUP_DIM), lambda b, r: (0, 0))]
    args = [jnp.asarray(cs, F32).astype(BF16), jnp.asarray(ss, F32).astype(BF16), z, w2]
    aliases = {}
    if prev is not None:
        in_specs.append(pl.BlockSpec(memory_space=pl.ANY))
        args.append(prev)
        aliases = {4: 0}
    return pl.pallas_call(
        _fourier_kernel,
        grid=(n_batch, nr),
        in_specs=in_specs,
        out_specs=pl.BlockSpec((rows, D_MODEL), lambda b, r: (first_row // rows + b * nr + r, 0)),
        out_shape=jax.ShapeDtypeStruct((T_ALL, D_MODEL), BF16),
        input_output_aliases=aliases,
        compiler_params=_params("arbitrary", "arbitrary"),
        name="fourier_mix",
    )(*args)


def _merge_kernel(uc_ref, at_ref, fo_ref, g0_ref, g1_ref, g2_ref, x_ref, wc_ref, wa_ref, wf_ref,
                  wo_ref, m_ref, gf_ref, xo_ref, h2_ref, h2t_ref):
    yc = jnp.dot(uc_ref[...], wc_ref[...], preferred_element_type=F32)
    ya = jnp.dot(at_ref[...], wa_ref[...], preferred_element_type=F32)
    yf = jnp.dot(fo_ref[...], wf_ref[...], preferred_element_type=F32)
    gate = lambda r: jax.nn.sigmoid(r[...].astype(F32))
    merged = gate(g0_ref) * yc + gate(g1_ref) * ya + gate(g2_ref) * yf
    y = jnp.dot(merged.astype(BF16), wo_ref[...], preferred_element_type=F32)
    x = x_ref[...] + m_ref[MOD_GATE1:MOD_GATE1 + 1, :] * y
    xo_ref[...] = x
    hn = _rms(x, gf_ref[...])
    h2 = hn * (1.0 + m_ref[MOD_SCALE2:MOD_SCALE2 + 1, :]) + m_ref[MOD_SHIFT2:MOD_SHIFT2 + 1, :]
    h2_ref[...] = h2.astype(BF16)
    h2t_ref[...] = h2.T.astype(BF16)


def _merge(uconv, att, four, z, x, wc, wa, wf, wo, mod_tok, g_ffn):
    per = TOKEN_BLOCK // MERGE_BLOCK
    row = pl.BlockSpec((MERGE_BLOCK, D_MODEL), lambda i: (i, 0))
    col = lambda c: pl.BlockSpec((MERGE_BLOCK, D_MODEL), lambda i: (i, c))
    wspec = pl.BlockSpec((D_MODEL, D_MODEL), lambda i: (0, 0))
    return pl.pallas_call(
        _merge_kernel,
        grid=(T_ALL // MERGE_BLOCK,),
        in_specs=[row, row, row, col(COL_G0), col(COL_G1), col(COL_G2), row,
                  wspec, wspec, wspec, wspec,
                  pl.BlockSpec((None, N_MOD, D_MODEL), lambda i: (i // per, 0, 0)),
                  pl.BlockSpec((1, D_MODEL), lambda i: (0, 0))],
        out_specs=[row, row, pl.BlockSpec((D_MODEL, MERGE_BLOCK), lambda i: (0, i))],
        out_shape=[jax.ShapeDtypeStruct((T_ALL, D_MODEL), F32),
                   jax.ShapeDtypeStruct((T_ALL, D_MODEL), BF16),
                   jax.ShapeDtypeStruct((D_MODEL, T_ALL), BF16)],
        compiler_params=_params("arbitrary"),
        name="branch_merge",
    )(uconv, att, four, z, z, z, x, wc, wa, wf, wo, mod_tok, g_ffn.reshape(1, D_MODEL))


def _candidate_cells():
    return [(a, b) for a in range(PEER_TOPK) for b in range(PEER_TOPK) if (a + 1) * (b + 1) <= PEER_TOPK]


def _extract_topk(p, orig_scr, s_scr, rank_scr, val_scr, break_ties):
    shape = (PEER_HEADS, PEER_KEYS, ROUTE_BLOCK)
    s_scr[...] = orig_scr[p]
    rank_scr[p] = jnp.full(shape, NOT_SELECTED, F32)
    for k in range(PEER_TOPK):
        cur = s_scr[...]
        m = jnp.max(cur, axis=1, keepdims=True)
        sel = cur == m
        if break_ties:
            key_iota = lax.broadcasted_iota(jnp.int32, shape, 1).astype(F32)
            first = jnp.min(jnp.where(sel, key_iota, float(PEER_KEYS)), axis=1, keepdims=True)
            sel = key_iota == first
        s_scr[...] = jnp.where(sel, -jnp.inf, cur)
        rank_scr[p] = jnp.where(sel, float(k), rank_scr[p])
        for h in range(PEER_HEADS):
            val_scr[p, k, h:h + 1, :] = m[h]


def _route_kernel(h_ref, wpq_ref, keys_ref, r2_ref, e2_ref, n1_ref, c1_ref,
                  s_scr, orig_scr, rank_scr, val_scr, n_scr):
    tb = ROUTE_BLOCK
    q = jnp.dot(h_ref[...], wpq_ref[...], preferred_element_type=F32).astype(BF16)

    for p in range(2):
        piece = lambda h: slice((2 * h + p) * PEER_KEY_DIM, (2 * h + p + 1) * PEER_KEY_DIM)
        qp = jnp.concatenate([q[:, piece(h)] for h in range(PEER_HEADS)], axis=1)
        s = lax.dot_general(keys_ref[p], qp, NT_DIMS, preferred_element_type=F32)
        orig_scr[p] = s.reshape(PEER_HEADS, PEER_KEYS, tb)

    off = 0.0
    for p in range(2):
        _extract_topk(p, orig_scr, s_scr, rank_scr, val_scr, break_ties=False)
        taken = jnp.sum(jnp.where(rank_scr[p] < float(PEER_TOPK), 1.0, 0.0), axis=1)
        off = jnp.maximum(off, jnp.max(jnp.abs(taken - float(PEER_TOPK))))

    @pl.when(off > 0.0)
    def _():
        for p in range(2):
            _extract_topk(p, orig_scr, s_scr, rank_scr, val_scr, break_ties=True)

    v1 = [val_scr[0, a] for a in range(PEER_TOPK)]
    v2 = [val_scr[1, b] for b in range(PEER_TOPK)]
    cells = _candidate_cells()
    sums = {c: v1[c[0]] + v2[c[1]] for c in cells}
    undecided = lambda c, d: not (d[0] >= c[0] and d[1] >= c[1])
    before = {c: float((c[0] + 1) * (c[1] + 1) - 1 + sum(undecided(c, d) for d in cells[ci + 1:]))
              for ci, c in enumerate(cells)}
    for ci, c in enumerate(cells):
        for d in cells[ci + 1:]:
            if undecided(c, d):
                first = jnp.where(sums[c] >= sums[d], 1.0, 0.0)
                before[d] = before[d] + first
                before[c] = before[c] - first
    e1 = [jnp.exp(v1[a] - v1[0]) for a in range(PEER_TOPK)]
    e2 = [jnp.exp(v2[b] - v2[0]) for b in range(PEER_TOPK)]
    zsum = 0.0
    count = [0.0] * PEER_TOPK
    for c in cells:
        chosen = before[c] < float(PEER_TOPK)
        count[c[0]] = count[c[0]] + jnp.where(chosen, 1.0, 0.0)
        zsum = zsum + jnp.where(chosen, e1[c[0]] * e2[c[1]], 0.0)
    for a in range(PEER_TOPK):
        n_scr[a] = count[a]
    n_scr[PEER_TOPK] = 0.5 / zsum

    for h in range(PEER_HEADS):
        row = pl.ds(h, 1)
        rank1 = rank_scr[0, h]
        n1 = jnp.zeros((PEER_KEYS, tb), F32)
        for a in range(PEER_TOPK):
            n1 = jnp.where(rank1 == float(a), n_scr[a, row, :], n1)
        n1_ref[h] = n1
        w1 = jnp.where(rank1 < float(PEER_TOPK), jnp.exp(orig_scr[0, h] - val_scr[0, 0, row, :]), 0.0)
        c1_ref[h] = w1 * n_scr[PEER_TOPK, row, :]
        rank2 = rank_scr[1, h]
        r2_ref[h] = rank2.astype(BF16)
        w2 = jnp.where(rank2 < float(PEER_TOPK), jnp.exp(orig_scr[1, h] - val_scr[1, 0, row, :]), 0.0)
        e2_ref[h] = w2.astype(BF16)


def _route(h2, wpq, keys, layer):
    tb = ROUTE_BLOCK
    out = lambda dt: jax.ShapeDtypeStruct((PEER_HEADS, PEER_KEYS, T_ALL), dt)
    ospec = pl.BlockSpec((PEER_HEADS, PEER_KEYS, tb), lambda i: (0, 0, i))
    hk = PEER_HEADS * PEER_KEYS
    return pl.pallas_call(
        _route_kernel,
        grid=(T_ALL // tb,),
        in_specs=[pl.BlockSpec((tb, D_MODEL), lambda i: (i, 0)),
                  pl.BlockSpec((None, D_MODEL, 2 * hk), lambda i: (layer, 0, 0)),
                  pl.BlockSpec((None, 2, hk, hk), lambda i: (layer, 0, 0, 0))],
        out_specs=[ospec, ospec, ospec, ospec],
        out_shape=[out(BF16), out(BF16), out(F32), out(F32)],
        scratch_shapes=[pltpu.VMEM((PEER_HEADS, PEER_KEYS, tb), F32),
                        pltpu.VMEM((2, PEER_HEADS, PEER_KEYS, tb), F32),
                        pltpu.VMEM((2, PEER_HEADS, PEER_KEYS, tb), F32),
                        pltpu.VMEM((2, PEER_TOPK, PEER_HEADS, tb), F32),
                        pltpu.VMEM((PEER_TOPK + 1, PEER_HEADS, tb), F32)],
        compiler_params=_params("arbitrary"),
        name="peer_route",
    )(h2, wpq, keys)


SQRT_HALF = math.sqrt(0.5)


EXPERT_TOKEN_CHUNK = 1024
N_EXPERT_CHUNKS = TOKEN_BLOCK // EXPERT_TOKEN_CHUNK


def _experts_kernel(xt_ref, u_ref, vt_ref, r2_ref, e2_ref, n1_ref, c1_ref, res_ref, m_ref,
                    o_ref, acc_scr, act_scr, p_scr):
    eb = pl.program_id(1)

    @pl.when(eb == 0)
    def _():
        acc_scr[...] = jnp.zeros_like(acc_scr)

    cols = lambda j: slice(j * EXPERT_TOKEN_CHUNK, (j + 1) * EXPERT_TOKEN_CHUNK)
    def pre_activate(j):
        act_scr[j % 2] = jnp.dot(u_ref[...], xt_ref[:, cols(j)], preferred_element_type=F32)

    def activate(j):
        for c in range(I1_PER_BLOCK):
            i1 = pl.ds(eb * I1_PER_BLOCK + c, 1)
            tiles = (PEER_KEYS // BF16_TILE_ROWS, BF16_TILE_ROWS, EXPERT_TOKEN_CHUNK)
            tile_row = lambda ref, h: jnp.broadcast_to(ref[h, i1, cols(j)], tiles[1:]).astype(BF16)[None]
            gate = jnp.zeros(tiles, BF16)
            for h in range(PEER_HEADS):
                taken = r2_ref[h, :, cols(j)].reshape(tiles) < tile_row(n1_ref, h)
                e2 = e2_ref[h, :, cols(j)].reshape(tiles)
                gate = gate + jnp.where(taken, e2, jnp.zeros((), BF16)) * tile_row(c1_ref, h)
            keys = slice(c * PEER_KEYS, (c + 1) * PEER_KEYS)
            a = act_scr[j % 2, keys, :]
            erf1 = 1.0 + lax.erf(a * SQRT_HALF)
            p_scr[keys, cols(j)] = (a.astype(BF16) * erf1.astype(BF16)
                                    * gate.reshape(PEER_KEYS, EXPERT_TOKEN_CHUNK))

    def mix(j):
        acc_scr[:, cols(j)] += jnp.dot(vt_ref[...], p_scr[:, cols(j)], preferred_element_type=F32)

    for stage in range(N_EXPERT_CHUNKS + 2):
        if stage < N_EXPERT_CHUNKS:
            pre_activate(stage)
        if 1 <= stage <= N_EXPERT_CHUNKS:
            activate(stage - 1)
        if stage >= 2:
            mix(stage - 2)

    @pl.when(eb == pl.num_programs(1) - 1)
    def _():
        o_ref[...] = res_ref[...] + m_ref[MOD_GATE2:MOD_GATE2 + 1, :] * acc_scr[...].T


def _experts(h2t, u_bf, vt_bf, r2, e2, n1, c1, x_mid, mod_tok, layer):
    once = pl.Buffered(1)
    tok = pl.BlockSpec((PEER_HEADS, PEER_KEYS, TOKEN_BLOCK), lambda t, e: (0, 0, t), pipeline_mode=once)
    return pl.pallas_call(
        _experts_kernel,
        grid=(N_TOKEN_BLOCKS, PEER_EXPERTS // EXPERT_BLOCK),
        in_specs=[pl.BlockSpec((D_MODEL, TOKEN_BLOCK), lambda t, e: (0, t)),
                  pl.BlockSpec((None, EXPERT_BLOCK, D_MODEL), lambda t, e: (layer, e, 0)),
                  pl.BlockSpec((None, D_MODEL, EXPERT_BLOCK), lambda t, e: (layer, 0, e)),
                  tok, tok, tok, tok,
                  pl.BlockSpec((TOKEN_BLOCK, D_MODEL), lambda t, e: (t, 0), pipeline_mode=once),
                  pl.BlockSpec((None, N_MOD, D_MODEL), lambda t, e: (t, 0, 0))],
        out_specs=pl.BlockSpec((TOKEN_BLOCK, D_MODEL), lambda t, e: (t, 0)),
        out_shape=jax.ShapeDtypeStruct((T_ALL, D_MODEL), F32),
        scratch_shapes=[pltpu.VMEM((D_MODEL, TOKEN_BLOCK), F32),
                        pltpu.VMEM((2, EXPERT_BLOCK, EXPERT_TOKEN_CHUNK), F32),
                        pltpu.VMEM((EXPERT_BLOCK, TOKEN_BLOCK), BF16)],
        compiler_params=_params("arbitrary", "arbitrary"),
        name="peer_experts",
    )(h2t, u_bf, vt_bf, r2, e2, n1, c1, x_mid, mod_tok)


TRANSPOSE_ROWS = 512


def _transpose_cast_kernel(v_ref, o_ref):
    o_ref[...] = v_ref[...].T.astype(BF16)


def _transpose_cast(peer_v):
    return pl.pallas_call(
        _transpose_cast_kernel,
        grid=(DEPTH, PEER_EXPERTS // TRANSPOSE_ROWS),
        in_specs=[pl.BlockSpec((None, TRANSPOSE_ROWS, D_MODEL), lambda l, e: (l, e, 0))],
        out_specs=pl.BlockSpec((None, D_MODEL, TRANSPOSE_ROWS), lambda l, e: (l, 0, e)),
        out_shape=jax.ShapeDtypeStruct((DEPTH, D_MODEL, PEER_EXPERTS), BF16),
        compiler_params=_params("arbitrary", "arbitrary"),
        name="expert_table_transpose",
    )(peer_v)


def _block_diag_keys(sub_keys):
    eye = jnp.eye(PEER_HEADS, dtype=sub_keys.dtype)
    keys = jnp.einsum('lhpjd,hg->lphjgd', sub_keys, eye)
    return keys.reshape(DEPTH, 2, PEER_HEADS * PEER_KEYS, PEER_HEADS * PEER_KEY_DIM).astype(BF16)


def kernel(x_prompt, x_sample, cache_k, cache_v, c, c_ctx, w_in, conv_w, w_conv_out, rpb, w_attn_out,
           w_four_out, w_o, g_mix, g_ffn, w_mod, b_mod, w_pq, sub_keys, peer_u, peer_v, g_final):
    assert SEQ & (SEQ - 1) == 0 and DEC_SEQ & (DEC_SEQ - 1) == 0
    x = jnp.concatenate([x_prompt.reshape(T_CTX, D_MODEL), x_sample.reshape(T_LAT, D_MODEL)], axis=0)
    cvec = jnp.zeros((8, D_MODEL), F32).at[0].set(c_ctx).at[1:1 + DEC_BATCH].set(c)
    mod = _modulation(cvec, w_mod, b_mod)
    lat_per_block = DEC_SEQ // TOKEN_BLOCK
    block_row = np.array([0] * N_CTX_BLOCKS
                         + [1 + b for b in range(DEC_BATCH) for _ in range(lat_per_block)])
    mod_tok = mod[:, block_row].reshape(DEPTH, N_TOKEN_BLOCKS, N_MOD, D_MODEL)
    cache_k4 = cache_k.reshape(DEC_BATCH, DEPTH, PAST_LEN, D_MODEL)
    cache_v4 = cache_v.reshape(DEC_BATCH, DEPTH, PAST_LEN, D_MODEL)

    na_bias = _na_bias(rpb)
    keys = _block_diag_keys(sub_keys)
    w_pq_bf = w_pq.astype(BF16)
    u_bf = peer_u.astype(BF16)
    vt_bf = _transpose_cast(peer_v)

    new_k = new_v = None
    for l in range(DEPTH):
        h = _norm_mod(x, g_mix[l], mod_tok[l])
        z, new_k, new_v = _in_proj(h, w_in, l, new_k, new_v)
        uconv = _short_conv(z, conv_w[l])
        att = _ctx_attention(z)
        att = _na_attention(z, cache_k4, cache_v4, na_bias, att, l)
        four = _fourier(z, SEQ, BATCH, 0, None)
        four = _fourier(z, DEC_SEQ, DEC_BATCH, T_CTX, four)
        x_mid, h2, h2t = _merge(uconv, att, four, z, x,
                           w_conv_out[l].astype(BF16), w_attn_out[l].astype(BF16),
                           w_four_out[l].astype(BF16), w_o[l].astype(BF16), mod_tok[l], g_ffn[l])
        r2, e2, n1, c1 = _route(h2, w_pq_bf, keys, l)
        x = _experts(h2t, u_bf, vt_bf, r2, e2, n1, c1, x_mid, mod_tok[l], l)

    y_prompt = _final_norm(x, g_final, 0, N_CTX_BLOCKS).reshape(BATCH, SEQ, D_MODEL)
    y_sample = _final_norm(x, g_final, N_CTX_BLOCKS, N_TOKEN_BLOCKS - N_CTX_BLOCKS)
    y_sample = y_sample.reshape(DEC_BATCH, DEC_SEQ, D_MODEL)
    return (y_prompt, y_sample, new_k, new_v)
```

```python
import functools
import math

import numpy as np
import jax
import jax.numpy as jnp
from jax import lax
from jax.experimental import pallas as pl
from jax.experimental.pallas import tpu as pltpu

D_MODEL = 1024
BATCH = 16
SEQ = 256
DEPTH = 2
DEC_BATCH = 2
DEC_SEQ = 1024
PAST_LEN = 512
GRID_W = 64
CONV_K = 3
NA_HEADS = 8
NA_HEAD_DIM = D_MODEL // NA_HEADS
NA_MAX_ROWS = 8
NA_COLS = 16
FNET_GROUPS = 4
FNET_GROUP_DIM = D_MODEL // FNET_GROUPS
N_BRANCH = 3
IN_COLS = 10 * D_MODEL
PEER_HEADS = 8
PEER_KEYS = 128
PEER_EXPERTS = PEER_KEYS * PEER_KEYS
PEER_KEY_DIM = 128
PEER_TOPK = 16
N_MOD = 6
RMS_EPS = 1e-6
NEG_INF = -1e30

T_CTX = BATCH * SEQ
T_LAT = DEC_BATCH * DEC_SEQ
T_ALL = T_CTX + T_LAT

COL_CB, COL_CC, COL_CX, COL_Q, COL_K, COL_V, COL_F, COL_G0, COL_G1, COL_G2 = range(10)
MOD_SHIFT1, MOD_SCALE1, MOD_GATE1, MOD_SHIFT2, MOD_SCALE2, MOD_GATE2 = range(6)

TOKEN_BLOCK = 1024
N_TOKEN_BLOCKS = T_ALL // TOKEN_BLOCK
N_CTX_BLOCKS = T_CTX // TOKEN_BLOCK
MERGE_BLOCK = 512
ROUTE_BLOCK = 256
EXPERT_BLOCK = 1024
I1_PER_BLOCK = EXPERT_BLOCK // PEER_KEYS
VMEM_LIMIT = 56 * 1024 * 1024
BF16_TILE_ROWS = 16

F32 = jnp.float32
BF16 = jnp.bfloat16
NT_DIMS = (((1,), (1,)), ((), ()))


def _params(*semantics):
    return pltpu.CompilerParams(dimension_semantics=semantics, vmem_limit_bytes=VMEM_LIMIT)


def _mod_kernel(c_ref, w_ref, b_ref, o_ref):
    c = c_ref[...]
    s = c * jax.nn.sigmoid(c)
    o_ref[0] = jnp.dot(s, w_ref[0], preferred_element_type=F32,
                       precision=lax.Precision.HIGHEST) + b_ref[0]


def _modulation(cvec, w_mod, b_mod):
    tn = 1536
    ncol = N_MOD * D_MODEL
    return pl.pallas_call(
        _mod_kernel,
        grid=(DEPTH, ncol // tn),
        in_specs=[pl.BlockSpec((8, D_MODEL), lambda l, n: (0, 0)),
                  pl.BlockSpec((1, D_MODEL, tn), lambda l, n: (l, 0, n)),
                  pl.BlockSpec((1, 1, tn), lambda l, n: (l, 0, n))],
        out_specs=pl.BlockSpec((1, 8, tn), lambda l, n: (l, 0, n)),
        out_shape=jax.ShapeDtypeStruct((DEPTH, 8, ncol), F32),
        compiler_params=_params("arbitrary", "arbitrary"),
        name="adaln_table",
    )(cvec, w_mod, b_mod.reshape(DEPTH, 1, ncol))


def _rms(x, g):
    return x * lax.rsqrt(jnp.mean(x * x, axis=-1, keepdims=True) + RMS_EPS) * g


def _norm_mod_kernel(x_ref, g_ref, m_ref, o_ref):
    y = _rms(x_ref[...], g_ref[...])
    h = y * (1.0 + m_ref[MOD_SCALE1:MOD_SCALE1 + 1, :]) + m_ref[MOD_SHIFT1:MOD_SHIFT1 + 1, :]
    o_ref[...] = h.astype(BF16)


def _norm_mod(x, g, mod_tok):
    return pl.pallas_call(
        _norm_mod_kernel,
        grid=(N_TOKEN_BLOCKS,),
        in_specs=[pl.BlockSpec((TOKEN_BLOCK, D_MODEL), lambda i: (i, 0)),
                  pl.BlockSpec((1, D_MODEL), lambda i: (0, 0)),
                  pl.BlockSpec((None, N_MOD, D_MODEL), lambda i: (i, 0, 0))],
        out_specs=pl.BlockSpec((TOKEN_BLOCK, D_MODEL), lambda i: (i, 0)),
        out_shape=jax.ShapeDtypeStruct((T_ALL, D_MODEL), BF16),
        compiler_params=_params("arbitrary"),
        name="norm_modulate",
    )(x, g.reshape(1, D_MODEL), mod_tok)


def _final_norm_kernel(x_ref, g_ref, o_ref):
    o_ref[...] = _rms(x_ref[...], g_ref[...])


def _final_norm(x, g, first_block, n_blocks):
    return pl.pallas_call(
        _final_norm_kernel,
        grid=(n_blocks,),
        in_specs=[pl.BlockSpec((TOKEN_BLOCK, D_MODEL), lambda i: (i + first_block, 0)),
                  pl.BlockSpec((1, D_MODEL), lambda i: (0, 0))],
        out_specs=pl.BlockSpec((TOKEN_BLOCK, D_MODEL), lambda i: (i, 0)),
        out_shape=jax.ShapeDtypeStruct((n_blocks * TOKEN_BLOCK, D_MODEL), F32),
        compiler_params=_params("arbitrary"),
        name="final_norm",
    )(x, g.reshape(1, D_MODEL))


def _in_proj_kernel(h_ref, w_ref, *rest):
    z_ref, k_ref, v_ref, wb_scr = rest[-4:]
    n = pl.program_id(0)
    i = pl.program_id(1)

    @pl.when(i == 0)
    def _():
        wb_scr[...] = w_ref[...].astype(BF16)

    z = jnp.dot(h_ref[...], wb_scr[...], preferred_element_type=F32)
    z_ref[...] = z.astype(BF16)
    per_block = TOKEN_BLOCK // SEQ

    @pl.when((n == COL_K) & (i < N_CTX_BLOCKS))
    def _():
        k_ref[...] = z.reshape(per_block, SEQ, NA_HEADS, NA_HEAD_DIM)

    @pl.when((n == COL_V) & (i < N_CTX_BLOCKS))
    def _():
        v_ref[...] = z.reshape(per_block, SEQ, NA_HEADS, NA_HEAD_DIM)


def _cache_block_index(col, layer, n, i):
    last = N_CTX_BLOCKS - 1
    blk = jnp.where(n < col, 0, jnp.where(n > col, last, jnp.minimum(i, last)))
    return (blk, layer, 0, 0, 0)


def _in_proj(h, w_in, layer, cache_k, cache_v):
    per_block = TOKEN_BLOCK // SEQ
    cache_shape = jax.ShapeDtypeStruct((BATCH, DEPTH, SEQ, NA_HEADS, NA_HEAD_DIM), F32)
    cache_spec = lambda col: pl.BlockSpec(
        (per_block, None, SEQ, NA_HEADS, NA_HEAD_DIM), functools.partial(_cache_block_index, col, layer))
    in_specs = [pl.BlockSpec((TOKEN_BLOCK, D_MODEL), lambda n, i: (i, 0)),
                pl.BlockSpec((None, D_MODEL, D_MODEL), lambda n, i: (layer, 0, n))]
    args = [h, w_in]
    aliases = {}
    if cache_k is not None:
        in_specs += [pl.BlockSpec(memory_space=pl.ANY)] * 2
        args += [cache_k, cache_v]
        aliases = {2: 1, 3: 2}
    return pl.pallas_call(
        _in_proj_kernel,
        grid=(IN_COLS // D_MODEL, N_TOKEN_BLOCKS),
        in_specs=in_specs,
        out_specs=[pl.BlockSpec((TOKEN_BLOCK, D_MODEL), lambda n, i: (i, n)),
                   cache_spec(COL_K), cache_spec(COL_V)],
        out_shape=[jax.ShapeDtypeStruct((T_ALL, IN_COLS), BF16), cache_shape, cache_shape],
        scratch_shapes=[pltpu.VMEM((D_MODEL, D_MODEL), BF16)],
        input_output_aliases=aliases,
        compiler_params=_params("arbitrary", "arbitrary"),
        name="in_proj",
    )(*args)


def _conv_kernel(cb_ref, cc_ref, cx_ref, w_ref, o_ref):
    i = pl.program_id(0)
    u = cc_ref[...].astype(F32) * cx_ref[...].astype(F32)
    seq = jnp.where(i < N_CTX_BLOCKS, SEQ, DEC_SEQ)
    pos = lax.broadcasted_iota(jnp.int32, (TOKEN_BLOCK, 1), 0) & (seq - 1)
    prev = jnp.where(pos == 0, 0.0, pltpu.roll(u, 1, 0))
    nxt = jnp.where(pos == seq - 1, 0.0, pltpu.roll(u, TOKEN_BLOCK - 1, 0))
    y = w_ref[0:1, :] * prev + w_ref[1:2, :] * u + w_ref[2:3, :] * nxt
    o_ref[...] = (cb_ref[...].astype(F32) * y).astype(BF16)


def _short_conv(z, conv_w):
    col = lambda c: pl.BlockSpec((TOKEN_BLOCK, D_MODEL), lambda i: (i, c))
    return pl.pallas_call(
        _conv_kernel,
        grid=(N_TOKEN_BLOCKS,),
        in_specs=[col(COL_CB), col(COL_CC), col(COL_CX),
                  pl.BlockSpec((CONV_K, D_MODEL), lambda i: (0, 0))],
        out_specs=pl.BlockSpec((TOKEN_BLOCK, D_MODEL), lambda i: (i, 0)),
        out_shape=jax.ShapeDtypeStruct((T_ALL, D_MODEL), BF16),
        compiler_params=_params("arbitrary"),
        name="short_conv",
    )(z, z, z, conv_w)


ATT_SCALE = NA_HEAD_DIM ** -0.5


def _ctx_attn_kernel(q_ref, k_ref, v_ref, o_ref):
    for h in range(NA_HEADS):
        sl = slice(h * NA_HEAD_DIM, (h + 1) * NA_HEAD_DIM)
        s = lax.dot_general(q_ref[:, sl], k_ref[:, sl], NT_DIMS,
                            preferred_element_type=F32) * ATT_SCALE
        p = jnp.exp(s - jnp.max(s, axis=-1, keepdims=True))
        o = jnp.dot(p.astype(BF16), v_ref[:, sl], preferred_element_type=F32)
        o_ref[:, sl] = (o / jnp.sum(p, axis=-1, keepdims=True)).astype(BF16)


def _ctx_attention(z):
    col = lambda c: pl.BlockSpec((SEQ, D_MODEL), lambda b: (b, c))
    return pl.pallas_call(
        _ctx_attn_kernel,
        grid=(BATCH,),
        in_specs=[col(COL_Q), col(COL_K), col(COL_V)],
        out_specs=pl.BlockSpec((SEQ, D_MODEL), lambda b: (b, 0)),
        out_shape=jax.ShapeDtypeStruct((T_ALL, D_MODEL), BF16),
        compiler_params=_params("arbitrary"),
        name="ctx_attention",
    )(z, z, z)


NA_GRID_ROWS = DEC_SEQ // GRID_W
NA_WIN_ROWS = min(NA_MAX_ROWS, NA_GRID_ROWS)


def _na_row_groups():
    groups = []
    for r in range(NA_GRID_ROWS):
        r0 = min(max(r - NA_WIN_ROWS // 2, 0), NA_GRID_ROWS - NA_WIN_ROWS)
        if groups and groups[-1][0] == r0:
            groups[-1][1].append(r)
        else:
            groups.append((r0, [r]))
    return groups


def _na_kernel(q_ref, k_ref, v_ref, ck_ref, cv_ref, b_ref, att_in_ref, o_ref):
    del att_in_ref
    pairs = NA_WIN_ROWS // 2
    ck = ck_ref[...].astype(BF16)
    cv = cv_ref[...].astype(BF16)
    for r0, q_rows in _na_row_groups():
        rows = slice(q_rows[0] * GRID_W, (q_rows[-1] + 1) * GRID_W)
        win = slice(r0 * GRID_W, (r0 + NA_WIN_ROWS) * GRID_W)
        q = q_ref[rows, :]
        slab = lambda d: jnp.concatenate([b_ref[d + 2 * m] for m in range(pairs)], axis=1)
        bias = jnp.concatenate([slab(r0 - r + NA_MAX_ROWS - 1) for r in q_rows], axis=0)
        s_loc = lax.dot_general(q, k_ref[win, :], NT_DIMS, preferred_element_type=F32) * ATT_SCALE + bias
        s_ctx = lax.dot_general(q, ck, NT_DIMS, preferred_element_type=F32) * ATT_SCALE
        m = jnp.maximum(jnp.max(s_loc, axis=-1, keepdims=True), jnp.max(s_ctx, axis=-1, keepdims=True))
        p_loc = jnp.exp(s_loc - m)
        p_ctx = jnp.exp(s_ctx - m)
        den = jnp.sum(p_loc, axis=-1, keepdims=True) + jnp.sum(p_ctx, axis=-1, keepdims=True)
        o = (jnp.dot(p_loc.astype(BF16), v_ref[win, :], preferred_element_type=F32)
             + jnp.dot(p_ctx.astype(BF16), cv, preferred_element_type=F32))
        o_ref[rows, :] = (o / den).astype(BF16)


NA_PAIR_TABLES = 2 * NA_MAX_ROWS - 2


def _na_bias(rpb):
    assert NA_WIN_ROWS % 2 == 0
    c = np.arange(GRID_W)
    c0 = np.clip(c - NA_COLS // 2, 0, GRID_W - NA_COLS)
    in_cols = (c[None, :] >= c0[:, None]) & (c[None, :] < c0[:, None] + NA_COLS)
    dc = c[None, :] - c[:, None] + (NA_COLS - 1)
    pick = (dc[None] == np.arange(2 * NA_COLS - 1)[:, None, None]) & in_cols[None]
    t = jnp.einsum('lhdj,jqk->lhdqk', rpb.astype(F32), jnp.asarray(pick, F32), precision=lax.Precision.HIGHEST)
    t = jnp.where(in_cols, t, NEG_INF)
    return jnp.concatenate([t[:, :, :-1], t[:, :, 1:]], axis=-1)


def _na_attention(z, cache_k, cache_v, bias, att, layer):
    lat0 = T_CTX // DEC_SEQ
    col = lambda c: pl.BlockSpec((DEC_SEQ, NA_HEAD_DIM), lambda h, b: (lat0 + b, c * NA_HEADS + h))
    cache = pl.BlockSpec((None, None, PAST_LEN, NA_HEAD_DIM), lambda h, b: (b, layer, 0, h))
    return pl.pallas_call(
        _na_kernel,
        grid=(NA_HEADS, DEC_BATCH),
        in_specs=[col(COL_Q), col(COL_K), col(COL_V), cache, cache,
                  pl.BlockSpec((None, None, NA_PAIR_TABLES, GRID_W, 2 * GRID_W), lambda h, b: (layer, h, 0, 0, 0)),
                  pl.BlockSpec(memory_space=pl.ANY)],
        out_specs=pl.BlockSpec((DEC_SEQ, NA_HEAD_DIM), lambda h, b: (lat0 + b, h)),
        out_shape=jax.ShapeDtypeStruct((T_ALL, D_MODEL), BF16),
        input_output_aliases={6: 0},
        compiler_params=_params("arbitrary", "arbitrary"),
        name="na_attention",
    )(z, z, z, cache_k, cache_v, bias, att)


def _dft_matrices(n):
    j = np.arange(n)
    ang = 2.0 * np.pi * ((j[:, None] * j[None, :]) % n) / n
    return np.cos(ang) / math.sqrt(n), np.sin(ang) / math.sqrt(n)


def _fourier_kernel(cs_ref, ss_ref, f_ref, w2_ref, *rest):
    o_ref = rest[-1]
    f = f_ref[...]
    cu = jnp.dot(cs_ref[...], f, preferred_element_type=F32).astype(BF16)
    su = jnp.dot(ss_ref[...], f, preferred_element_type=F32).astype(BF16)
    for g in range(FNET_GROUPS):
        sl = slice(g * FNET_GROUP_DIM, (g + 1) * FNET_GROUP_DIM)
        lhs = jnp.concatenate([cu[:, sl], su[:, sl]], axis=1)
        o_ref[:, sl] = jnp.dot(lhs, w2_ref[...], preferred_element_type=F32).astype(BF16)


FOURIER_ROWS = 256


def _fourier(z, seq, n_batch, first_row, prev):
    cs, ss = _dft_matrices(seq)
    cc, sc = _dft_matrices(FNET_GROUP_DIM)
    w2 = jnp.asarray(np.concatenate([cc, -sc], axis=0), F32).astype(BF16)
    rows = FOURIER_ROWS
    nr = seq // rows
    in_specs = [pl.BlockSpec((rows, seq), lambda b, r: (r, 0)),
                pl.BlockSpec((rows, seq), lambda b, r: (r, 0)),
                pl.BlockSpec((seq, D_MODEL), lambda b, r: (first_row // seq + b, COL_F)),
                pl.BlockSpec((2 * FNET_GROUP_DIM, FNET_GROUP_DIM), lambda b, r: (0, 0))]
    args = [jnp.asarray(cs, F32).astype(BF16), jnp.asarray(ss, F32).astype(BF16), z, w2]
    aliases = {}
    if prev is not None:
        in_specs.append(pl.BlockSpec(memory_space=pl.ANY))
        args.append(prev)
        aliases = {4: 0}
    return pl.pallas_call(
        _fourier_kernel,
        grid=(n_batch, nr),
        in_specs=in_specs,
        out_specs=pl.BlockSpec((rows, D_MODEL), lambda b, r: (first_row // rows + b * nr + r, 0)),
        out_shape=jax.ShapeDtypeStruct((T_ALL, D_MODEL), BF16),
        input_output_aliases=aliases,
        compiler_params=_params("arbitrary", "arbitrary"),
        name="fourier_mix",
    )(*args)


def _merge_kernel(uc_ref, at_ref, fo_ref, g0_ref, g1_ref, g2_ref, x_ref, wc_ref, wa_ref, wf_ref,
                  wo_ref, m_ref, gf_ref, xo_ref, h2_ref, h2t_ref):
    yc = jnp.dot(uc_ref[...], wc_ref[...], preferred_element_type=F32)
    ya = jnp.dot(at_ref[...], wa_ref[...], preferred_element_type=F32)
    yf = jnp.dot(fo_ref[...], wf_ref[...], preferred_element_type=F32)
    gate = lambda r: jax.nn.sigmoid(r[...].astype(F32))
    merged = gate(g0_ref) * yc + gate(g1_ref) * ya + gate(g2_ref) * yf
    y = jnp.dot(merged.astype(BF16), wo_ref[...], preferred_element_type=F32)
    x = x_ref[...] + m_ref[MOD_GATE1:MOD_GATE1 + 1, :] * y
    xo_ref[...] = x
    hn = _rms(x, gf_ref[...])
    h2 = hn * (1.0 + m_ref[MOD_SCALE2:MOD_SCALE2 + 1, :]) + m_ref[MOD_SHIFT2:MOD_SHIFT2 + 1, :]
    h2_ref[...] = h2.astype(BF16)
    h2t_ref[...] = h2.T.astype(BF16)


def _merge(uconv, att, four, z, x, wc, wa, wf, wo, mod_tok, g_ffn):
    per = TOKEN_BLOCK // MERGE_BLOCK
    row = pl.BlockSpec((MERGE_BLOCK, D_MODEL), lambda i: (i, 0))
    col = lambda c: pl.BlockSpec((MERGE_BLOCK, D_MODEL), lambda i: (i, c))
    wspec = pl.BlockSpec((D_MODEL, D_MODEL), lambda i: (0, 0))
    return pl.pallas_call(
        _merge_kernel,
        grid=(T_ALL // MERGE_BLOCK,),
        in_specs=[row, row, row, col(COL_G0), col(COL_G1), col(COL_G2), row,
                  wspec, wspec, wspec, wspec,
                  pl.BlockSpec((None, N_MOD, D_MODEL), lambda i: (i // per, 0, 0)),
                  pl.BlockSpec((1, D_MODEL), lambda i: (0, 0))],
        out_specs=[row, row, pl.BlockSpec((D_MODEL, MERGE_BLOCK), lambda i: (0, i))],
        out_shape=[jax.ShapeDtypeStruct((T_ALL, D_MODEL), F32),
                   jax.ShapeDtypeStruct((T_ALL, D_MODEL), BF16),
                   jax.ShapeDtypeStruct((D_MODEL, T_ALL), BF16)],
        compiler_params=_params("arbitrary"),
        name="branch_merge",
    )(uconv, att, four, z, z, z, x, wc, wa, wf, wo, mod_tok, g_ffn.reshape(1, D_MODEL))


def _candidate_cells():
    return [(a, b) for a in range(PEER_TOPK) for b in range(PEER_TOPK) if (a + 1) * (b + 1) <= PEER_TOPK]


RANK_CODE_BITS = 0xFF7F0000 - (1 << 32)
LOWEST_SCORE = -3.0e38
RANK_CODE_LIMIT = -3.2e38
NOT_TAKEN = 127
SMALL_COUNT_FROM = 4
assert (SMALL_COUNT_FROM + 1) * 4 > PEER_TOPK and 2 * PEER_TOPK <= 32


def _rank_code(k):
    return float(np.array(RANK_CODE_BITS | k, np.int32).view(np.float32))


def _store_sorted_value(val_scr, p, k, m):
    for h in range(PEER_HEADS):
        val_scr[p, k, h:h + 1, :] = m[h]


def _extract_topk_fast(p, orig_scr, s_scr, rank_scr, val_scr):
    s_scr[...] = jnp.maximum(orig_scr[p], LOWEST_SCORE)
    for k in range(PEER_TOPK):
        cur = s_scr[...]
        m = jnp.max(cur, axis=1, keepdims=True)
        s_scr[...] = jnp.where(cur == m, _rank_code(k), cur)
        _store_sorted_value(val_scr, p, k, m)
    coded = s_scr[...]
    taken = coded < RANK_CODE_LIMIT
    rank_scr[p] = jnp.where(taken, pltpu.bitcast(coded, jnp.int32) & 0xFF, NOT_TAKEN)
    count = jnp.sum(jnp.where(taken, 1.0, 0.0), axis=1)
    clamped = jnp.where(val_scr[p, PEER_TOPK - 1] <= LOWEST_SCORE, 1.0, 0.0)
    return jnp.max(jnp.abs(count - float(PEER_TOPK)) + clamped)


def _extract_topk_ties(p, orig_scr, s_scr, rank_scr, val_scr):
    shape = (PEER_HEADS, PEER_KEYS, ROUTE_BLOCK)
    key_iota = lax.broadcasted_iota(jnp.int32, shape, 1)
    s_scr[...] = orig_scr[p]
    rank_scr[p] = jnp.full(shape, NOT_TAKEN, jnp.int32)
    for k in range(PEER_TOPK):
        cur = s_scr[...]
        m = jnp.max(cur, axis=1, keepdims=True)
        first = jnp.min(jnp.where(cur == m, key_iota, PEER_KEYS), axis=1, keepdims=True)
        sel = key_iota == first
        s_scr[...] = jnp.where(sel, -jnp.inf, cur)
        rank_scr[p] = jnp.where(sel, k, rank_scr[p])
        _store_sorted_value(val_scr, p, k, m)


def _route_kernel(h_ref, wpq_ref, keys_ref, r2_ref, e2_ref, n1_ref, c1_ref,
                  s_scr, orig_scr, rank_scr, val_scr, n_scr, small_scr):
    tb = ROUTE_BLOCK
    q = jnp.dot(h_ref[...], wpq_ref[...], preferred_element_type=F32).astype(BF16)

    for p in range(2):
        piece = lambda h: slice((2 * h + p) * PEER_KEY_DIM, (2 * h + p + 1) * PEER_KEY_DIM)
        qp = jnp.concatenate([q[:, piece(h)] for h in range(PEER_HEADS)], axis=1)
        s = lax.dot_general(keys_ref[p], qp, NT_DIMS, preferred_element_type=F32)
        orig_scr[p] = s.reshape(PEER_HEADS, PEER_KEYS, tb)

    off = 0.0
    for p in range(2):
        off = jnp.maximum(off, _extract_topk_fast(p, orig_scr, s_scr, rank_scr, val_scr))

    @pl.when(off > 0.0)
    def _():
        for p in range(2):
            _extract_topk_ties(p, orig_scr, s_scr, rank_scr, val_scr)

    v1 = [val_scr[0, a] for a in range(PEER_TOPK)]
    v2 = [val_scr[1, b] for b in range(PEER_TOPK)]
    cells = _candidate_cells()
    sums = {c: v1[c[0]] + v2[c[1]] for c in cells}
    undecided = lambda c, d: not (d[0] >= c[0] and d[1] >= c[1])
    before = {c: float((c[0] + 1) * (c[1] + 1) - 1 + sum(undecided(c, d) for d in cells[ci + 1:]))
              for ci, c in enumerate(cells)}
    for ci, c in enumerate(cells):
        for d in cells[ci + 1:]:
            if undecided(c, d):
                first = jnp.where(sums[c] >= sums[d], 1.0, 0.0)
                before[d] = before[d] + first
                before[c] = before[c] - first
    e1 = [jnp.exp(v1[a] - v1[0]) for a in range(PEER_TOPK)]
    e2 = [jnp.exp(v2[b] - v2[0]) for b in range(PEER_TOPK)]
    zsum = 0.0
    count = [0.0] * PEER_TOPK
    for c in cells:
        chosen = before[c] < float(PEER_TOPK)
        count[c[0]] = count[c[0]] + jnp.where(chosen, 1.0, 0.0)
        zsum = zsum + jnp.where(chosen, e1[c[0]] * e2[c[1]], 0.0)
    small = jnp.zeros((PEER_HEADS, tb), jnp.int32)
    for a in range(PEER_TOPK):
        if a < SMALL_COUNT_FROM:
            n_scr[a] = count[a]
        else:
            small = small | (count[a].astype(jnp.int32) << (2 * a))
    small_scr[...] = small
    n_scr[PEER_TOPK] = 0.5 / zsum

    for h in range(PEER_HEADS):
        row = pl.ds(h, 1)
        rank1 = rank_scr[0, h]
        taken1 = rank1 < PEER_TOPK
        r = jnp.minimum(rank1, PEER_TOPK - 1)
        n1 = (lax.shift_right_logical(jnp.broadcast_to(small_scr[row, :], r.shape), r << 1) & 3).astype(F32)
        for a in reversed(range(SMALL_COUNT_FROM)):
            n1 = jnp.where(r == a, n_scr[a, row, :], n1)
        n1_ref[h] = jnp.where(taken1, n1, 0.0)
        w1 = jnp.where(taken1, jnp.exp(orig_scr[0, h] - val_scr[0, 0, row, :]), 0.0)
        c1_ref[h] = w1 * n_scr[PEER_TOPK, row, :]
        rank2 = rank_scr[1, h]
        r2_ref[h] = rank2.astype(F32).astype(BF16)
        w2 = jnp.where(rank2 < PEER_TOPK, jnp.exp(orig_scr[1, h] - val_scr[1, 0, row, :]), 0.0)
        e2_ref[h] = w2.astype(BF16)


def _route(h2, wpq, keys, layer):
    tb = ROUTE_BLOCK
    out = lambda dt: jax.ShapeDtypeStruct((PEER_HEADS, PEER_KEYS, T_ALL), dt)
    ospec = pl.BlockSpec((PEER_HEADS, PEER_KEYS, tb), lambda i: (0, 0, i))
    hk = PEER_HEADS * PEER_KEYS
    return pl.pallas_call(
        _route_kernel,
        grid=(T_ALL // tb,),
        in_specs=[pl.BlockSpec((tb, D_MODEL), lambda i: (i, 0)),
                  pl.BlockSpec((None, D_MODEL, 2 * hk), lambda i: (layer, 0, 0)),
                  pl.BlockSpec((None, 2, hk, hk), lambda i: (layer, 0, 0, 0))],
        out_specs=[ospec, ospec, ospec, ospec],
        out_shape=[out(BF16), out(BF16), out(F32), out(F32)],
        scratch_shapes=[pltpu.VMEM((PEER_HEADS, PEER_KEYS, tb), F32),
                        pltpu.VMEM((2, PEER_HEADS, PEER_KEYS, tb), F32),
                        pltpu.VMEM((2, PEER_HEADS, PEER_KEYS, tb), jnp.int32),
                        pltpu.VMEM((2, PEER_TOPK, PEER_HEADS, tb), F32),
                        pltpu.VMEM((PEER_TOPK + 1, PEER_HEADS, tb), F32),
                        pltpu.VMEM((PEER_HEADS, tb), jnp.int32)],
        compiler_params=_params("arbitrary"),
        name="peer_route",
    )(h2, wpq, keys)


SQRT_HALF = math.sqrt(0.5)


EXPERT_TOKEN_CHUNK = 1024
N_EXPERT_CHUNKS = TOKEN_BLOCK // EXPERT_TOKEN_CHUNK


def _experts_kernel(xt_ref, u_ref, vt_ref, r2_ref, e2_ref, n1_ref, c1_ref, res_ref, m_ref,
                    o_ref, acc_scr, act_scr, p_scr):
    eb = pl.program_id(1)

    @pl.when(eb == 0)
    def _():
        acc_scr[...] = jnp.zeros_like(acc_scr)

    cols = lambda j: slice(j * EXPERT_TOKEN_CHUNK, (j + 1) * EXPERT_TOKEN_CHUNK)
    def pre_activate(j):
        act_scr[j % 2] = jnp.dot(u_ref[...], xt_ref[:, cols(j)], preferred_element_type=F32)

    def activate(j):
        for c in range(I1_PER_BLOCK):
            i1 = pl.ds(eb * I1_PER_BLOCK + c, 1)
            tiles = (PEER_KEYS // BF16_TILE_ROWS, BF16_TILE_ROWS, EXPERT_TOKEN_CHUNK)
            tile_row = lambda ref, h: jnp.broadcast_to(ref[h, i1, cols(j)], tiles[1:]).astype(BF16)[None]
            gate = jnp.zeros(tiles, BF16)
            for h in range(PEER_HEADS):
                taken = r2_ref[h, :, cols(j)].reshape(tiles) < tile_row(n1_ref, h)
                e2 = e2_ref[h, :, cols(j)].reshape(tiles)
                gate = gate + jnp.where(taken, e2, jnp.zeros((), BF16)) * tile_row(c1_ref, h)
            keys = slice(c * PEER_KEYS, (c + 1) * PEER_KEYS)
            a = act_scr[j % 2, keys, :]
            erf1 = 1.0 + lax.erf(a * SQRT_HALF)
            p_scr[keys, cols(j)] = (a.astype(BF16) * erf1.astype(BF16)
                                    * gate.reshape(PEER_KEYS, EXPERT_TOKEN_CHUNK))

    def mix(j):
        acc_scr[:, cols(j)] += jnp.dot(vt_ref[...], p_scr[:, cols(j)], preferred_element_type=F32)

    for stage in range(N_EXPERT_CHUNKS + 2):
        if stage < N_EXPERT_CHUNKS:
            pre_activate(stage)
        if 1 <= stage <= N_EXPERT_CHUNKS:
            activate(stage - 1)
        if stage >= 2:
            mix(stage - 2)

    @pl.when(eb == pl.num_programs(1) - 1)
    def _():
        o_ref[...] = res_ref[...] + m_ref[MOD_GATE2:MOD_GATE2 + 1, :] * acc_scr[...].T


def _experts(h2t, u_bf, vt_bf, r2, e2, n1, c1, x_mid, mod_tok, layer):
    once = pl.Buffered(1)
    tok = pl.BlockSpec((PEER_HEADS, PEER_KEYS, TOKEN_BLOCK), lambda t, e: (0, 0, t), pipeline_mode=once)
    return pl.pallas_call(
        _experts_kernel,
        grid=(N_TOKEN_BLOCKS, PEER_EXPERTS // EXPERT_BLOCK),
        in_specs=[pl.BlockSpec((D_MODEL, TOKEN_BLOCK), lambda t, e: (0, t)),
                  pl.BlockSpec((None, EXPERT_BLOCK, D_MODEL), lambda t, e: (layer, e, 0)),
                  pl.BlockSpec((None, D_MODEL, EXPERT_BLOCK), lambda t, e: (layer, 0, e)),
                  tok, tok, tok, tok,
                  pl.BlockSpec((TOKEN_BLOCK, D_MODEL), lambda t, e: (t, 0), pipeline_mode=once),
                  pl.BlockSpec((None, N_MOD, D_MODEL), lambda t, e: (t, 0, 0))],
        out_specs=pl.BlockSpec((TOKEN_BLOCK, D_MODEL), lambda t, e: (t, 0)),
        out_shape=jax.ShapeDtypeStruct((T_ALL, D_MODEL), F32),
        scratch_shapes=[pltpu.VMEM((D_MODEL, TOKEN_BLOCK), F32),
                        pltpu.VMEM((2, EXPERT_BLOCK, EXPERT_TOKEN_CHUNK), F32),
                        pltpu.VMEM((EXPERT_BLOCK, TOKEN_BLOCK), BF16)],
        compiler_params=_params("arbitrary", "arbitrary"),
        name="peer_experts",
    )(h2t, u_bf, vt_bf, r2, e2, n1, c1, x_mid, mod_tok)


TRANSPOSE_ROWS = 512


def _transpose_cast_kernel(v_ref, o_ref):
    o_ref[...] = v_ref[...].T.astype(BF16)


def _transpose_cast(peer_v):
    return pl.pallas_call(
        _transpose_cast_kernel,
        grid=(DEPTH, PEER_EXPERTS // TRANSPOSE_ROWS),
        in_specs=[pl.BlockSpec((None, TRANSPOSE_ROWS, D_MODEL), lambda l, e: (l, e, 0))],
        out_specs=pl.BlockSpec((None, D_MODEL, TRANSPOSE_ROWS), lambda l, e: (l, 0, e)),
        out_shape=jax.ShapeDtypeStruct((DEPTH, D_MODEL, PEER_EXPERTS), BF16),
        compiler_params=_params("arbitrary", "arbitrary"),
        name="expert_table_transpose",
    )(peer_v)


def _block_diag_keys(sub_keys):
    eye = jnp.eye(PEER_HEADS, dtype=sub_keys.dtype)
    keys = jnp.einsum('lhpjd,hg->lphjgd', sub_keys, eye)
    return keys.reshape(DEPTH, 2, PEER_HEADS * PEER_KEYS, PEER_HEADS * PEER_KEY_DIM).astype(BF16)


def kernel(x_prompt, x_sample, cache_k, cache_v, c, c_ctx, w_in, conv_w, w_conv_out, rpb, w_attn_out,
           w_four_out, w_o, g_mix, g_ffn, w_mod, b_mod, w_pq, sub_keys, peer_u, peer_v, g_final):
    assert SEQ & (SEQ - 1) == 0 and DEC_SEQ & (DEC_SEQ - 1) == 0
    x = jnp.concatenate([x_prompt.reshape(T_CTX, D_MODEL), x_sample.reshape(T_LAT, D_MODEL)], axis=0)
    cvec = jnp.zeros((8, D_MODEL), F32).at[0].set(c_ctx).at[1:1 + DEC_BATCH].set(c)
    mod = _modulation(cvec, w_mod, b_mod)
    lat_per_block = DEC_SEQ // TOKEN_BLOCK
    block_row = np.array([0] * N_CTX_BLOCKS
                         + [1 + b for b in range(DEC_BATCH) for _ in range(lat_per_block)])
    mod_tok = mod[:, block_row].reshape(DEPTH, N_TOKEN_BLOCKS, N_MOD, D_MODEL)
    cache_k4 = cache_k.reshape(DEC_BATCH, DEPTH, PAST_LEN, D_MODEL)
    cache_v4 = cache_v.reshape(DEC_BATCH, DEPTH, PAST_LEN, D_MODEL)

    na_bias = _na_bias(rpb)
    keys = _block_diag_keys(sub_keys)
    w_pq_bf = w_pq.astype(BF16)
    u_bf = peer_u.astype(BF16)
    vt_bf = _transpose_cast(peer_v)

    new_k = new_v = None
    for l in range(DEPTH):
        h = _norm_mod(x, g_mix[l], mod_tok[l])
        z, new_k, new_v = _in_proj(h, w_in, l, new_k, new_v)
        uconv = _short_conv(z, conv_w[l])
        att = _ctx_attention(z)
        att = _na_attention(z, cache_k4, cache_v4, na_bias, att, l)
        four = _fourier(z, SEQ, BATCH, 0, None)
        four = _fourier(z, DEC_SEQ, DEC_BATCH, T_CTX, four)
        x_mid, h2, h2t = _merge(uconv, att, four, z, x,
                           w_conv_out[l].astype(BF16), w_attn_out[l].astype(BF16),
                           w_four_out[l].astype(BF16), w_o[l].astype(BF16), mod_tok[l], g_ffn[l])
        r2, e2, n1, c1 = _route(h2, w_pq_bf, keys, l)
        x = _experts(h2t, u_bf, vt_bf, r2, e2, n1, c1, x_mid, mod_tok[l], l)

    y_prompt = _final_norm(x, g_final, 0, N_CTX_BLOCKS).reshape(BATCH, SEQ, D_MODEL)
    y_sample = _final_norm(x, g_final, N_CTX_BLOCKS, N_TOKEN_BLOCKS - N_CTX_BLOCKS)
    y_sample = y_sample.reshape(DEC_BATCH, DEC_SEQ, D_MODEL)
    return (y_prompt, y_sample, new_k, new_v)
```

```python
import functools
import math

import numpy as np
import jax
import jax.numpy as jnp
from jax import lax
from jax.experimental import pallas as pl
from jax.experimental.pallas import tpu as pltpu

D_MODEL = 1024
BATCH = 16
SEQ = 256
DEPTH = 2
DEC_BATCH = 2
DEC_SEQ = 1024
PAST_LEN = 512
GRID_W = 64
CONV_K = 3
NA_HEADS = 8
NA_HEAD_DIM = D_MODEL // NA_HEADS
NA_MAX_ROWS = 8
NA_COLS = 16
FNET_GROUPS = 4
FNET_GROUP_DIM = D_MODEL // FNET_GROUPS
N_BRANCH = 3
IN_COLS = 10 * D_MODEL
PEER_HEADS = 8
PEER_KEYS = 128
PEER_EXPERTS = PEER_KEYS * PEER_KEYS
PEER_KEY_DIM = 128
PEER_TOPK = 16
N_MOD = 6
RMS_EPS = 1e-6
NEG_INF = -1e30

T_CTX = BATCH * SEQ
T_LAT = DEC_BATCH * DEC_SEQ
T_ALL = T_CTX + T_LAT

COL_CB, COL_CC, COL_CX, COL_Q, COL_K, COL_V, COL_F, COL_G0, COL_G1, COL_G2 = range(10)
MOD_SHIFT1, MOD_SCALE1, MOD_GATE1, MOD_SHIFT2, MOD_SCALE2, MOD_GATE2 = range(6)

TOKEN_BLOCK = 1024
N_TOKEN_BLOCKS = T_ALL // TOKEN_BLOCK
N_CTX_BLOCKS = T_CTX // TOKEN_BLOCK
MERGE_BLOCK = 512
ROUTE_BLOCK = 256
EXPERT_BLOCK = 1024
I1_PER_BLOCK = EXPERT_BLOCK // PEER_KEYS
VMEM_LIMIT = 56 * 1024 * 1024
BF16_TILE_ROWS = 16

F32 = jnp.float32
BF16 = jnp.bfloat16
NT_DIMS = (((1,), (1,)), ((), ()))


def _params(*semantics):
    return pltpu.CompilerParams(dimension_semantics=semantics, vmem_limit_bytes=VMEM_LIMIT)


def _mod_kernel(c_ref, w_ref, b_ref, o_ref):
    c = c_ref[...]
    s = c * jax.nn.sigmoid(c)
    o_ref[0] = jnp.dot(s, w_ref[0], preferred_element_type=F32,
                       precision=lax.Precision.HIGHEST) + b_ref[0]


def _modulation(cvec, w_mod, b_mod):
    tn = 1536
    ncol = N_MOD * D_MODEL
    return pl.pallas_call(
        _mod_kernel,
        grid=(DEPTH, ncol // tn),
        in_specs=[pl.BlockSpec((8, D_MODEL), lambda l, n: (0, 0)),
                  pl.BlockSpec((1, D_MODEL, tn), lambda l, n: (l, 0, n)),
                  pl.BlockSpec((1, 1, tn), lambda l, n: (l, 0, n))],
        out_specs=pl.BlockSpec((1, 8, tn), lambda l, n: (l, 0, n)),
        out_shape=jax.ShapeDtypeStruct((DEPTH, 8, ncol), F32),
        compiler_params=_params("arbitrary", "arbitrary"),
        name="adaln_table",
    )(cvec, w_mod, b_mod.reshape(DEPTH, 1, ncol))


def _rms(x, g):
    return x * lax.rsqrt(jnp.mean(x * x, axis=-1, keepdims=True) + RMS_EPS) * g


def _norm_mod_kernel(x_ref, g_ref, m_ref, o_ref):
    y = _rms(x_ref[...], g_ref[...])
    h = y * (1.0 + m_ref[MOD_SCALE1:MOD_SCALE1 + 1, :]) + m_ref[MOD_SHIFT1:MOD_SHIFT1 + 1, :]
    o_ref[...] = h.astype(BF16)


def _norm_mod(x, g, mod_tok):
    return pl.pallas_call(
        _norm_mod_kernel,
        grid=(N_TOKEN_BLOCKS,),
        in_specs=[pl.BlockSpec((TOKEN_BLOCK, D_MODEL), lambda i: (i, 0)),
                  pl.BlockSpec((1, D_MODEL), lambda i: (0, 0)),
                  pl.BlockSpec((None, N_MOD, D_MODEL), lambda i: (i, 0, 0))],
        out_specs=pl.BlockSpec((TOKEN_BLOCK, D_MODEL), lambda i: (i, 0)),
        out_shape=jax.ShapeDtypeStruct((T_ALL, D_MODEL), BF16),
        compiler_params=_params("arbitrary"),
        name="norm_modulate",
    )(x, g.reshape(1, D_MODEL), mod_tok)


def _in_proj_kernel(h_ref, w_ref, *rest):
    z_ref, k_ref, v_ref, wb_scr = rest[-4:]
    n = pl.program_id(0)
    i = pl.program_id(1)

    @pl.when(i == 0)
    def _():
        wb_scr[...] = w_ref[...].astype(BF16)

    z = jnp.dot(h_ref[...], wb_scr[...], preferred_element_type=F32)
    z_ref[...] = z.astype(BF16)
    per_block = TOKEN_BLOCK // SEQ

    @pl.when((n == COL_K) & (i < N_CTX_BLOCKS))
    def _():
        k_ref[...] = z.reshape(per_block, SEQ, NA_HEADS, NA_HEAD_DIM)

    @pl.when((n == COL_V) & (i < N_CTX_BLOCKS))
    def _():
        v_ref[...] = z.reshape(per_block, SEQ, NA_HEADS, NA_HEAD_DIM)


def _cache_block_index(col, layer, n, i):
    last = N_CTX_BLOCKS - 1
    blk = jnp.where(n < col, 0, jnp.where(n > col, last, jnp.minimum(i, last)))
    return (blk, layer, 0, 0, 0)


def _in_proj(h, w_in, layer, cache_k, cache_v):
    per_block = TOKEN_BLOCK // SEQ
    cache_shape = jax.ShapeDtypeStruct((BATCH, DEPTH, SEQ, NA_HEADS, NA_HEAD_DIM), F32)
    cache_spec = lambda col: pl.BlockSpec(
        (per_block, None, SEQ, NA_HEADS, NA_HEAD_DIM), functools.partial(_cache_block_index, col, layer))
    in_specs = [pl.BlockSpec((TOKEN_BLOCK, D_MODEL), lambda n, i: (i, 0)),
                pl.BlockSpec((None, D_MODEL, D_MODEL), lambda n, i: (layer, 0, n))]
    args = [h, w_in]
    aliases = {}
    if cache_k is not None:
        in_specs += [pl.BlockSpec(memory_space=pl.ANY)] * 2
        args += [cache_k, cache_v]
        aliases = {2: 1, 3: 2}
    return pl.pallas_call(
        _in_proj_kernel,
        grid=(IN_COLS // D_MODEL, N_TOKEN_BLOCKS),
        in_specs=in_specs,
        out_specs=[pl.BlockSpec((TOKEN_BLOCK, D_MODEL), lambda n, i: (i, n)),
                   cache_spec(COL_K), cache_spec(COL_V)],
        out_shape=[jax.ShapeDtypeStruct((T_ALL, IN_COLS), BF16), cache_shape, cache_shape],
        scratch_shapes=[pltpu.VMEM((D_MODEL, D_MODEL), BF16)],
        input_output_aliases=aliases,
        compiler_params=_params("arbitrary", "arbitrary"),
        name="in_proj",
    )(*args)


def _conv_kernel(cb_ref, cc_ref, cx_ref, w_ref, o_ref):
    i = pl.program_id(0)
    u = cc_ref[...].astype(F32) * cx_ref[...].astype(F32)
    seq = jnp.where(i < N_CTX_BLOCKS, SEQ, DEC_SEQ)
    pos = lax.broadcasted_iota(jnp.int32, (TOKEN_BLOCK, 1), 0) & (seq - 1)
    prev = jnp.where(pos == 0, 0.0, pltpu.roll(u, 1, 0))
    nxt = jnp.where(pos == seq - 1, 0.0, pltpu.roll(u, TOKEN_BLOCK - 1, 0))
    y = w_ref[0:1, :] * prev + w_ref[1:2, :] * u + w_ref[2:3, :] * nxt
    o_ref[...] = (cb_ref[...].astype(F32) * y).astype(BF16)


def _short_conv(z, conv_w):
    col = lambda c: pl.BlockSpec((TOKEN_BLOCK, D_MODEL), lambda i: (i, c))
    return pl.pallas_call(
        _conv_kernel,
        grid=(N_TOKEN_BLOCKS,),
        in_specs=[col(COL_CB), col(COL_CC), col(COL_CX),
                  pl.BlockSpec((CONV_K, D_MODEL), lambda i: (0, 0))],
        out_specs=pl.BlockSpec((TOKEN_BLOCK, D_MODEL), lambda i: (i, 0)),
        out_shape=jax.ShapeDtypeStruct((T_ALL, D_MODEL), BF16),
        compiler_params=_params("arbitrary"),
        name="short_conv",
    )(z, z, z, conv_w)


ATT_SCALE = NA_HEAD_DIM ** -0.5


def _ctx_attn_kernel(q_ref, k_ref, v_ref, o_ref):
    for h in range(NA_HEADS):
        sl = slice(h * NA_HEAD_DIM, (h + 1) * NA_HEAD_DIM)
        s = lax.dot_general(q_ref[:, sl], k_ref[:, sl], NT_DIMS,
                            preferred_element_type=F32) * ATT_SCALE
        p = jnp.exp(s - jnp.max(s, axis=-1, keepdims=True))
        o = jnp.dot(p.astype(BF16), v_ref[:, sl], preferred_element_type=F32)
        o_ref[:, sl] = (o / jnp.sum(p, axis=-1, keepdims=True)).astype(BF16)


def _ctx_attention(z):
    col = lambda c: pl.BlockSpec((SEQ, D_MODEL), lambda b: (b, c))
    return pl.pallas_call(
        _ctx_attn_kernel,
        grid=(BATCH,),
        in_specs=[col(COL_Q), col(COL_K), col(COL_V)],
        out_specs=pl.BlockSpec((SEQ, D_MODEL), lambda b: (b, 0)),
        out_shape=jax.ShapeDtypeStruct((T_ALL, D_MODEL), BF16),
        compiler_params=_params("arbitrary"),
        name="ctx_attention",
    )(z, z, z)


NA_GRID_ROWS = DEC_SEQ // GRID_W
NA_WIN_ROWS = min(NA_MAX_ROWS, NA_GRID_ROWS)


def _na_row_groups():
    groups = []
    for r in range(NA_GRID_ROWS):
        r0 = min(max(r - NA_WIN_ROWS // 2, 0), NA_GRID_ROWS - NA_WIN_ROWS)
        if groups and groups[-1][0] == r0:
            groups[-1][1].append(r)
        else:
            groups.append((r0, [r]))
    return groups


def _na_kernel(q_ref, k_ref, v_ref, ck_ref, cv_ref, b_ref, att_in_ref, o_ref):
    del att_in_ref
    pairs = NA_WIN_ROWS // 2
    ck = ck_ref[...].astype(BF16)
    cv = cv_ref[...].astype(BF16)
    for r0, q_rows in _na_row_groups():
        rows = slice(q_rows[0] * GRID_W, (q_rows[-1] + 1) * GRID_W)
        win = slice(r0 * GRID_W, (r0 + NA_WIN_ROWS) * GRID_W)
        q = q_ref[rows, :]
        slab = lambda d: jnp.concatenate([b_ref[d + 2 * m] for m in range(pairs)], axis=1)
        bias = jnp.concatenate([slab(r0 - r + NA_MAX_ROWS - 1) for r in q_rows], axis=0)
        s_loc = lax.dot_general(q, k_ref[win, :], NT_DIMS, preferred_element_type=F32) * ATT_SCALE + bias
        s_ctx = lax.dot_general(q, ck, NT_DIMS, preferred_element_type=F32) * ATT_SCALE
        m = jnp.maximum(jnp.max(s_loc, axis=-1, keepdims=True), jnp.max(s_ctx, axis=-1, keepdims=True))
        p_loc = jnp.exp(s_loc - m)
        p_ctx = jnp.exp(s_ctx - m)
        den = jnp.sum(p_loc, axis=-1, keepdims=True) + jnp.sum(p_ctx, axis=-1, keepdims=True)
        o = (jnp.dot(p_loc.astype(BF16), v_ref[win, :], preferred_element_type=F32)
             + jnp.dot(p_ctx.astype(BF16), cv, preferred_element_type=F32))
        o_ref[rows, :] = (o / den).astype(BF16)


NA_PAIR_TABLES = 2 * NA_MAX_ROWS - 2


def _na_bias(rpb):
    assert NA_WIN_ROWS % 2 == 0
    c = np.arange(GRID_W)
    c0 = np.clip(c - NA_COLS // 2, 0, GRID_W - NA_COLS)
    in_cols = (c[None, :] >= c0[:, None]) & (c[None, :] < c0[:, None] + NA_COLS)
    dc = c[None, :] - c[:, None] + (NA_COLS - 1)
    pick = (dc[None] == np.arange(2 * NA_COLS - 1)[:, None, None]) & in_cols[None]
    t = jnp.einsum('lhdj,jqk->lhdqk', rpb.astype(F32), jnp.asarray(pick, F32), precision=lax.Precision.HIGHEST)
    t = jnp.where(in_cols, t, NEG_INF)
    return jnp.concatenate([t[:, :, :-1], t[:, :, 1:]], axis=-1)


def _na_attention(z, cache_k, cache_v, bias, att, layer):
    lat0 = T_CTX // DEC_SEQ
    col = lambda c: pl.BlockSpec((DEC_SEQ, NA_HEAD_DIM), lambda h, b: (lat0 + b, c * NA_HEADS + h))
    cache = pl.BlockSpec((None, None, PAST_LEN, NA_HEAD_DIM), lambda h, b: (b, layer, 0, h))
    return pl.pallas_call(
        _na_kernel,
        grid=(NA_HEADS, DEC_BATCH),
        in_specs=[col(COL_Q), col(COL_K), col(COL_V), cache, cache,
                  pl.BlockSpec((None, None, NA_PAIR_TABLES, GRID_W, 2 * GRID_W), lambda h, b: (layer, h, 0, 0, 0)),
                  pl.BlockSpec(memory_space=pl.ANY)],
        out_specs=pl.BlockSpec((DEC_SEQ, NA_HEAD_DIM), lambda h, b: (lat0 + b, h)),
        out_shape=jax.ShapeDtypeStruct((T_ALL, D_MODEL), BF16),
        input_output_aliases={6: 0},
        compiler_params=_params("arbitrary", "arbitrary"),
        name="na_attention",
    )(z, z, z, cache_k, cache_v, bias, att)


def _dft_matrices(n):
    j = np.arange(n)
    ang = 2.0 * np.pi * ((j[:, None] * j[None, :]) % n) / n
    return np.cos(ang) / math.sqrt(n), np.sin(ang) / math.sqrt(n)


def _fourier_kernel(cs_ref, ss_ref, f_ref, w2_ref, *rest):
    o_ref = rest[-1]
    f = f_ref[...]
    cu = jnp.dot(cs_ref[...], f, preferred_element_type=F32).astype(BF16)
    su = jnp.dot(ss_ref[...], f, preferred_element_type=F32).astype(BF16)
    for g in range(FNET_GROUPS):
        sl = slice(g * FNET_GROUP_DIM, (g + 1) * FNET_GROUP_DIM)
        lhs = jnp.concatenate([cu[:, sl], su[:, sl]], axis=1)
        o_ref[:, sl] = jnp.dot(lhs, w2_ref[...], preferred_element_type=F32).astype(BF16)


FOURIER_ROWS = 256


def _fourier(z, seq, n_batch, first_row, prev):
    cs, ss = _dft_matrices(seq)
    cc, sc = _dft_matrices(FNET_GROUP_DIM)
    w2 = jnp.asarray(np.concatenate([cc, -sc], axis=0), F32).astype(BF16)
    rows = FOURIER_ROWS
    nr = seq // rows
    in_specs = [pl.BlockSpec((rows, seq), lambda b, r: (r, 0)),
                pl.BlockSpec((rows, seq), lambda b, r: (r, 0)),
                pl.BlockSpec((seq, D_MODEL), lambda b, r: (first_row // seq + b, COL_F)),
                pl.BlockSpec((2 * FNET_GROUP_DIM, FNET_GROUP_DIM), lambda b, r: (0, 0))]
    args = [jnp.asarray(cs, F32).astype(BF16), jnp.asarray(ss, F32).astype(BF16), z, w2]
    aliases = {}
    if prev is not None:
        in_specs.append(pl.BlockSpec(memory_space=pl.ANY))
        args.append(prev)
        aliases = {4: 0}
    return pl.pallas_call(
        _fourier_kernel,
        grid=(n_batch, nr),
        in_specs=in_specs,
        out_specs=pl.BlockSpec((rows, D_MODEL), lambda b, r: (first_row // rows + b * nr + r, 0)),
        out_shape=jax.ShapeDtypeStruct((T_ALL, D_MODEL), BF16),
        input_output_aliases=aliases,
        compiler_params=_params("arbitrary", "arbitrary"),
        name="fourier_mix",
    )(*args)


def _merge_kernel(uc_ref, at_ref, fo_ref, g0_ref, g1_ref, g2_ref, x_ref, wc_ref, wa_ref, wf_ref,
                  wo_ref, m_ref, gf_ref, xo_ref, h2_ref, h2t_ref):
    yc = jnp.dot(uc_ref[...], wc_ref[...], preferred_element_type=F32)
    ya = jnp.dot(at_ref[...], wa_ref[...], preferred_element_type=F32)
    yf = jnp.dot(fo_ref[...], wf_ref[...], preferred_element_type=F32)
    gate = lambda r: jax.nn.sigmoid(r[...].astype(F32))
    merged = gate(g0_ref) * yc + gate(g1_ref) * ya + gate(g2_ref) * yf
    y = jnp.dot(merged.astype(BF16), wo_ref[...], preferred_element_type=F32)
    x = x_ref[...] + m_ref[MOD_GATE1:MOD_GATE1 + 1, :] * y
    xo_ref[...] = x
    hn = _rms(x, gf_ref[...])
    h2 = hn * (1.0 + m_ref[MOD_SCALE2:MOD_SCALE2 + 1, :]) + m_ref[MOD_SHIFT2:MOD_SHIFT2 + 1, :]
    h2_ref[...] = h2.astype(BF16)
    h2t_ref[...] = h2.T.astype(BF16)


def _merge(uconv, att, four, z, x, wc, wa, wf, wo, mod_tok, g_ffn):
    per = TOKEN_BLOCK // MERGE_BLOCK
    row = pl.BlockSpec((MERGE_BLOCK, D_MODEL), lambda i: (i, 0))
    col = lambda c: pl.BlockSpec((MERGE_BLOCK, D_MODEL), lambda i: (i, c))
    wspec = pl.BlockSpec((D_MODEL, D_MODEL), lambda i: (0, 0))
    return pl.pallas_call(
        _merge_kernel,
        grid=(T_ALL // MERGE_BLOCK,),
        in_specs=[row, row, row, col(COL_G0), col(COL_G1), col(COL_G2), row,
                  wspec, wspec, wspec, wspec,
                  pl.BlockSpec((None, N_MOD, D_MODEL), lambda i: (i // per, 0, 0)),
                  pl.BlockSpec((1, D_MODEL), lambda i: (0, 0))],
        out_specs=[row, row, pl.BlockSpec((D_MODEL, MERGE_BLOCK), lambda i: (0, i))],
        out_shape=[jax.ShapeDtypeStruct((T_ALL, D_MODEL), F32),
                   jax.ShapeDtypeStruct((T_ALL, D_MODEL), BF16),
                   jax.ShapeDtypeStruct((D_MODEL, T_ALL), BF16)],
        compiler_params=_params("arbitrary"),
        name="branch_merge",
    )(uconv, att, four, z, z, z, x, wc, wa, wf, wo, mod_tok, g_ffn.reshape(1, D_MODEL))


def _candidate_cells():
    return [(a, b) for a in range(PEER_TOPK) for b in range(PEER_TOPK) if (a + 1) * (b + 1) <= PEER_TOPK]


RANK_CODE_BITS = 0xFF7F0000 - (1 << 32)
LOWEST_SCORE = -3.0e38
RANK_CODE_LIMIT = -3.2e38
NOT_TAKEN = 127
SMALL_COUNT_FROM = 4
assert (SMALL_COUNT_FROM + 1) * 4 > PEER_TOPK and 2 * PEER_TOPK <= 32


def _rank_code(k):
    return float(np.array(RANK_CODE_BITS | k, np.int32).view(np.float32))


def _store_sorted_value(val_scr, p, k, m):
    for h in range(PEER_HEADS):
        val_scr[p, k, h:h + 1, :] = m[h]


def _extract_topk_fast(p, orig_scr, s_scr, rank_scr, val_scr):
    s_scr[...] = jnp.maximum(orig_scr[p], LOWEST_SCORE)
    for k in range(PEER_TOPK):
        cur = s_scr[...]
        m = jnp.max(cur, axis=1, keepdims=True)
        s_scr[...] = jnp.where(cur == m, _rank_code(k), cur)
        _store_sorted_value(val_scr, p, k, m)
    coded = s_scr[...]
    taken = coded < RANK_CODE_LIMIT
    rank_scr[p] = jnp.where(taken, pltpu.bitcast(coded, jnp.int32) & 0xFF, NOT_TAKEN)
    count = jnp.sum(jnp.where(taken, 1.0, 0.0), axis=1)
    clamped = jnp.where(val_scr[p, PEER_TOPK - 1] <= LOWEST_SCORE, 1.0, 0.0)
    return jnp.max(jnp.abs(count - float(PEER_TOPK)) + clamped)


def _extract_topk_ties(p, orig_scr, s_scr, rank_scr, val_scr):
    shape = (PEER_HEADS, PEER_KEYS, ROUTE_BLOCK)
    key_iota = lax.broadcasted_iota(jnp.int32, shape, 1)
    s_scr[...] = orig_scr[p]
    rank_scr[p] = jnp.full(shape, NOT_TAKEN, jnp.int32)
    for k in range(PEER_TOPK):
        cur = s_scr[...]
        m = jnp.max(cur, axis=1, keepdims=True)
        first = jnp.min(jnp.where(cur == m, key_iota, PEER_KEYS), axis=1, keepdims=True)
        sel = key_iota == first
        s_scr[...] = jnp.where(sel, -jnp.inf, cur)
        rank_scr[p] = jnp.where(sel, k, rank_scr[p])
        _store_sorted_value(val_scr, p, k, m)


def _route_kernel(h_ref, wpq_ref, keys_ref, r2_ref, e2_ref, n1_ref, c1_ref,
                  s_scr, orig_scr, rank_scr, val_scr, n_scr, small_scr):
    tb = ROUTE_BLOCK
    q = jnp.dot(h_ref[...], wpq_ref[...], preferred_element_type=F32).astype(BF16)

    for p in range(2):
        piece = lambda h: slice((2 * h + p) * PEER_KEY_DIM, (2 * h + p + 1) * PEER_KEY_DIM)
        qp = jnp.concatenate([q[:, piece(h)] for h in range(PEER_HEADS)], axis=1)
        s = lax.dot_general(keys_ref[p], qp, NT_DIMS, preferred_element_type=F32)
        orig_scr[p] = s.reshape(PEER_HEADS, PEER_KEYS, tb)

    off = 0.0
    for p in range(2):
        off = jnp.maximum(off, _extract_topk_fast(p, orig_scr, s_scr, rank_scr, val_scr))

    @pl.when(off > 0.0)
    def _():
        for p in range(2):
            _extract_topk_ties(p, orig_scr, s_scr, rank_scr, val_scr)

    v1 = [val_scr[0, a] for a in range(PEER_TOPK)]
    v2 = [val_scr[1, b] for b in range(PEER_TOPK)]
    cells = _candidate_cells()
    sums = {c: v1[c[0]] + v2[c[1]] for c in cells}
    undecided = lambda c, d: not (d[0] >= c[0] and d[1] >= c[1])
    before = {c: float((c[0] + 1) * (c[1] + 1) - 1 + sum(undecided(c, d) for d in cells[ci + 1:]))
              for ci, c in enumerate(cells)}
    for ci, c in enumerate(cells):
        for d in cells[ci + 1:]:
            if undecided(c, d):
                first = jnp.where(sums[c] >= sums[d], 1.0, 0.0)
                before[d] = before[d] + first
                before[c] = before[c] - first
    e1 = [jnp.exp(v1[a] - v1[0]) for a in range(PEER_TOPK)]
    e2 = [jnp.exp(v2[b] - v2[0]) for b in range(PEER_TOPK)]
    zsum = 0.0
    count = [0.0] * PEER_TOPK
    for c in cells:
        chosen = before[c] < float(PEER_TOPK)
        count[c[0]] = count[c[0]] + jnp.where(chosen, 1.0, 0.0)
        zsum = zsum + jnp.where(chosen, e1[c[0]] * e2[c[1]], 0.0)
    small = jnp.zeros((PEER_HEADS, tb), jnp.int32)
    for a in range(PEER_TOPK):
        if a < SMALL_COUNT_FROM:
            n_scr[a] = count[a]
        else:
            small = small | (count[a].astype(jnp.int32) << (2 * a))
    small_scr[...] = small
    n_scr[PEER_TOPK] = 0.5 / zsum

    for h in range(PEER_HEADS):
        row = pl.ds(h, 1)
        rank1 = rank_scr[0, h]
        taken1 = rank1 < PEER_TOPK
        r = jnp.minimum(rank1, PEER_TOPK - 1)
        n1 = (lax.shift_right_logical(jnp.broadcast_to(small_scr[row, :], r.shape), r << 1) & 3).astype(F32)
        for a in reversed(range(SMALL_COUNT_FROM)):
            n1 = jnp.where(r == a, n_scr[a, row, :], n1)
        n1_ref[h] = jnp.where(taken1, n1, 0.0)
        w1 = jnp.where(taken1, jnp.exp(orig_scr[0, h] - val_scr[0, 0, row, :]), 0.0)
        c1_ref[h] = w1 * n_scr[PEER_TOPK, row, :]
        rank2 = rank_scr[1, h]
        r2_ref[h] = rank2.astype(F32).astype(BF16)
        w2 = jnp.where(rank2 < PEER_TOPK, jnp.exp(orig_scr[1, h] - val_scr[1, 0, row, :]), 0.0)
        e2_ref[h] = w2.astype(BF16)


def _route(h2, wpq, keys, layer):
    tb = ROUTE_BLOCK
    out = lambda dt: jax.ShapeDtypeStruct((PEER_HEADS, PEER_KEYS, T_ALL), dt)
    ospec = pl.BlockSpec((PEER_HEADS, PEER_KEYS, tb), lambda i: (0, 0, i))
    hk = PEER_HEADS * PEER_KEYS
    return pl.pallas_call(
        _route_kernel,
        grid=(T_ALL // tb,),
        in_specs=[pl.BlockSpec((tb, D_MODEL), lambda i: (i, 0)),
                  pl.BlockSpec((None, D_MODEL, 2 * hk), lambda i: (layer, 0, 0)),
                  pl.BlockSpec((None, 2, hk, hk), lambda i: (layer, 0, 0, 0))],
        out_specs=[ospec, ospec, ospec, ospec],
        out_shape=[out(BF16), out(BF16), out(F32), out(F32)],
        scratch_shapes=[pltpu.VMEM((PEER_HEADS, PEER_KEYS, tb), F32),
                        pltpu.VMEM((2, PEER_HEADS, PEER_KEYS, tb), F32),
                        pltpu.VMEM((2, PEER_HEADS, PEER_KEYS, tb), jnp.int32),
                        pltpu.VMEM((2, PEER_TOPK, PEER_HEADS, tb), F32),
                        pltpu.VMEM((PEER_TOPK + 1, PEER_HEADS, tb), F32),
                        pltpu.VMEM((PEER_HEADS, tb), jnp.int32)],
        compiler_params=_params("arbitrary"),
        name="peer_route",
    )(h2, wpq, keys)


SQRT_HALF = math.sqrt(0.5)


EXPERT_TOKEN_CHUNK = 1024
N_EXPERT_CHUNKS = TOKEN_BLOCK // EXPERT_TOKEN_CHUNK
ACT_SLOTS = min(2, N_EXPERT_CHUNKS)


def _experts_kernel(xt_ref, u_ref, vt_ref, r2_ref, e2_ref, n1_ref, c1_ref, res_ref, m_ref, g_ref, mn_ref,
                    o1_ref, o2_ref, acc_scr, act_scr, p_scr, *, last_layer):
    eb = pl.program_id(1)

    @pl.when(eb == 0)
    def _():
        acc_scr[...] = jnp.zeros_like(acc_scr)

    cols = lambda j: slice(j * EXPERT_TOKEN_CHUNK, (j + 1) * EXPERT_TOKEN_CHUNK)
    def pre_activate(j):
        act_scr[j % ACT_SLOTS] = jnp.dot(u_ref[...], xt_ref[:, cols(j)], preferred_element_type=F32)

    def activate(j):
        for c in range(I1_PER_BLOCK):
            i1 = pl.ds(eb * I1_PER_BLOCK + c, 1)
            tiles = (PEER_KEYS // BF16_TILE_ROWS, BF16_TILE_ROWS, EXPERT_TOKEN_CHUNK)
            tile_row = lambda ref, h: jnp.broadcast_to(ref[h, i1, cols(j)], tiles[1:]).astype(BF16)[None]
            gate = jnp.zeros(tiles, BF16)
            for h in range(PEER_HEADS):
                taken = r2_ref[h, :, cols(j)].reshape(tiles) < tile_row(n1_ref, h)
                e2 = e2_ref[h, :, cols(j)].reshape(tiles)
                gate = gate + jnp.where(taken, e2, jnp.zeros((), BF16)) * tile_row(c1_ref, h)
            keys = slice(c * PEER_KEYS, (c + 1) * PEER_KEYS)
            a = act_scr[j % ACT_SLOTS, keys, :]
            erf1 = 1.0 + lax.erf(a * SQRT_HALF)
            p_scr[keys, cols(j)] = (a.astype(BF16) * erf1.astype(BF16)
                                    * gate.reshape(PEER_KEYS, EXPERT_TOKEN_CHUNK))

    def mix(j):
        acc_scr[:, cols(j)] += jnp.dot(vt_ref[...], p_scr[:, cols(j)], preferred_element_type=F32)

    for stage in range(N_EXPERT_CHUNKS + 2):
        if stage < N_EXPERT_CHUNKS:
            pre_activate(stage)
        if 1 <= stage <= N_EXPERT_CHUNKS:
            activate(stage - 1)
        if stage >= 2:
            mix(stage - 2)

    @pl.when(eb == pl.num_programs(1) - 1)
    def _():
        x = res_ref[...] + m_ref[MOD_GATE2:MOD_GATE2 + 1, :] * acc_scr[...].T
        y = _rms(x, g_ref[...])
        if last_layer:
            t = pl.program_id(0)

            @pl.when(t < N_CTX_BLOCKS)
            def _():
                o1_ref[...] = y

            @pl.when(t >= N_CTX_BLOCKS)
            def _():
                o2_ref[...] = y
        else:
            o1_ref[...] = x
            h = y * (1.0 + mn_ref[MOD_SCALE1:MOD_SCALE1 + 1, :]) + mn_ref[MOD_SHIFT1:MOD_SHIFT1 + 1, :]
            o2_ref[...] = h.astype(BF16)


def _experts(h2t, u_bf, vt_bf, r2, e2, n1, c1, x_mid, mod_tok, g_next, mod_next, layer):
    once = pl.Buffered(1)
    last_layer = layer == DEPTH - 1
    tok = pl.BlockSpec((PEER_HEADS, PEER_KEYS, TOKEN_BLOCK), lambda t, e: (0, 0, t), pipeline_mode=once)
    block = (TOKEN_BLOCK, D_MODEL)
    if last_layer:
        out_specs = [pl.BlockSpec(block, lambda t, e: (jnp.minimum(t, N_CTX_BLOCKS - 1), 0), pipeline_mode=once),
                     pl.BlockSpec(block, lambda t, e: (jnp.maximum(t - N_CTX_BLOCKS, 0), 0), pipeline_mode=once)]
        out_shape = [jax.ShapeDtypeStruct((T_CTX, D_MODEL), F32), jax.ShapeDtypeStruct((T_LAT, D_MODEL), F32)]
    else:
        out_specs = [pl.BlockSpec(block, lambda t, e: (t, 0))] * 2
        out_shape = [jax.ShapeDtypeStruct((T_ALL, D_MODEL), F32), jax.ShapeDtypeStruct((T_ALL, D_MODEL), BF16)]
    return pl.pallas_call(
        functools.partial(_experts_kernel, last_layer=last_layer),
        grid=(N_TOKEN_BLOCKS, PEER_EXPERTS // EXPERT_BLOCK),
        in_specs=[pl.BlockSpec((D_MODEL, TOKEN_BLOCK), lambda t, e: (0, t)),
                  pl.BlockSpec((None, EXPERT_BLOCK, D_MODEL), lambda t, e: (layer, e, 0)),
                  pl.BlockSpec((None, D_MODEL, EXPERT_BLOCK), lambda t, e: (layer, 0, e)),
                  tok, tok, tok, tok,
                  pl.BlockSpec((TOKEN_BLOCK, D_MODEL), lambda t, e: (t, 0), pipeline_mode=once),
                  pl.BlockSpec((None, N_MOD, D_MODEL), lambda t, e: (t, 0, 0)),
                  pl.BlockSpec((1, D_MODEL), lambda t, e: (0, 0)),
                  pl.BlockSpec((None, N_MOD, D_MODEL), lambda t, e: (t, 0, 0))],
        out_specs=out_specs,
        out_shape=out_shape,
        scratch_shapes=[pltpu.VMEM((D_MODEL, TOKEN_BLOCK), F32),
                        pltpu.VMEM((ACT_SLOTS, EXPERT_BLOCK, EXPERT_TOKEN_CHUNK), F32),
                        pltpu.VMEM((EXPERT_BLOCK, TOKEN_BLOCK), BF16)],
        compiler_params=_params("arbitrary", "arbitrary"),
        name="peer_experts",
    )(h2t, u_bf, vt_bf, r2, e2, n1, c1, x_mid, mod_tok, g_next.reshape(1, D_MODEL), mod_next)


TRANSPOSE_ROWS = 512


def _transpose_cast_kernel(v_ref, o_ref):
    o_ref[...] = v_ref[...].T.astype(BF16)


def _transpose_cast(peer_v):
    return pl.pallas_call(
        _transpose_cast_kernel,
        grid=(DEPTH, PEER_EXPERTS // TRANSPOSE_ROWS),
        in_specs=[pl.BlockSpec((None, TRANSPOSE_ROWS, D_MODEL), lambda l, e: (l, e, 0))],
        out_specs=pl.BlockSpec((None, D_MODEL, TRANSPOSE_ROWS), lambda l, e: (l, 0, e)),
        out_shape=jax.ShapeDtypeStruct((DEPTH, D_MODEL, PEER_EXPERTS), BF16),
        compiler_params=_params("arbitrary", "arbitrary"),
        name="expert_table_transpose",
    )(peer_v)


def _block_diag_keys(sub_keys):
    eye = jnp.eye(PEER_HEADS, dtype=sub_keys.dtype)
    keys = jnp.einsum('lhpjd,hg->lphjgd', sub_keys, eye)
    return keys.reshape(DEPTH, 2, PEER_HEADS * PEER_KEYS, PEER_HEADS * PEER_KEY_DIM).astype(BF16)


def kernel(x_prompt, x_sample, cache_k, cache_v, c, c_ctx, w_in, conv_w, w_conv_out, rpb, w_attn_out,
           w_four_out, w_o, g_mix, g_ffn, w_mod, b_mod, w_pq, sub_keys, peer_u, peer_v, g_final):
    assert SEQ & (SEQ - 1) == 0 and DEC_SEQ & (DEC_SEQ - 1) == 0
    x = jnp.concatenate([x_prompt.reshape(T_CTX, D_MODEL), x_sample.reshape(T_LAT, D_MODEL)], axis=0)
    cvec = jnp.zeros((8, D_MODEL), F32).at[0].set(c_ctx).at[1:1 + DEC_BATCH].set(c)
    mod = _modulation(cvec, w_mod, b_mod)
    lat_per_block = DEC_SEQ // TOKEN_BLOCK
    block_row = np.array([0] * N_CTX_BLOCKS
                         + [1 + b for b in range(DEC_BATCH) for _ in range(lat_per_block)])
    mod_tok = mod[:, block_row].reshape(DEPTH, N_TOKEN_BLOCKS, N_MOD, D_MODEL)
    cache_k4 = cache_k.reshape(DEC_BATCH, DEPTH, PAST_LEN, D_MODEL)
    cache_v4 = cache_v.reshape(DEC_BATCH, DEPTH, PAST_LEN, D_MODEL)

    na_bias = _na_bias(rpb)
    keys = _block_diag_keys(sub_keys)
    w_pq_bf = w_pq.astype(BF16)
    u_bf = peer_u.astype(BF16)
    vt_bf = _transpose_cast(peer_v)

    new_k = new_v = None
    h = _norm_mod(x, g_mix[0], mod_tok[0])
    for l in range(DEPTH):
        z, new_k, new_v = _in_proj(h, w_in, l, new_k, new_v)
        uconv = _short_conv(z, conv_w[l])
        att = _ctx_attention(z)
        att = _na_attention(z, cache_k4, cache_v4, na_bias, att, l)
        four = _fourier(z, SEQ, BATCH, 0, None)
        four = _fourier(z, DEC_SEQ, DEC_BATCH, T_CTX, four)
        x_mid, h2, h2t = _merge(uconv, att, four, z, x,
                           w_conv_out[l].astype(BF16), w_attn_out[l].astype(BF16),
                           w_four_out[l].astype(BF16), w_o[l].astype(BF16), mod_tok[l], g_ffn[l])
        r2, e2, n1, c1 = _route(h2, w_pq_bf, keys, l)
        if l + 1 < DEPTH:
            x, h = _experts(h2t, u_bf, vt_bf, r2, e2, n1, c1, x_mid, mod_tok[l], g_mix[l + 1], mod_tok[l + 1], l)
        else:
            y_ctx, y_lat = _experts(h2t, u_bf, vt_bf, r2, e2, n1, c1, x_mid, mod_tok[l], g_final, mod_tok[l], l)

    y_prompt = y_ctx.reshape(BATCH, SEQ, D_MODEL)
    y_sample = y_lat.reshape(DEC_BATCH, DEC_SEQ, D_MODEL)
    return (y_prompt, y_sample, new_k, new_v)
```

```python
import functools
import math

import numpy as np
import jax
import jax.numpy as jnp
from jax import lax
from jax.experimental import pallas as pl
from jax.experimental.pallas import tpu as pltpu

D_MODEL = 1024
BATCH = 16
SEQ = 256
DEPTH = 2
DEC_BATCH = 2
DEC_SEQ = 1024
PAST_LEN = 512
GRID_W = 64
CONV_K = 3
NA_HEADS = 8
NA_HEAD_DIM = D_MODEL // NA_HEADS
NA_MAX_ROWS = 8
NA_COLS = 16
FNET_GROUPS = 4
FNET_GROUP_DIM = D_MODEL // FNET_GROUPS
N_BRANCH = 3
IN_COLS = 10 * D_MODEL
PEER_HEADS = 8
PEER_KEYS = 128
PEER_EXPERTS = PEER_KEYS * PEER_KEYS
PEER_KEY_DIM = 128
PEER_TOPK = 16
N_MOD = 6
RMS_EPS = 1e-6
NEG_INF = -1e30

T_CTX = BATCH * SEQ
T_LAT = DEC_BATCH * DEC_SEQ
T_ALL = T_CTX + T_LAT

COL_CB, COL_CC, COL_CX, COL_Q, COL_K, COL_V, COL_F, COL_G0, COL_G1, COL_G2 = range(10)
MOD_SHIFT1, MOD_SCALE1, MOD_GATE1, MOD_SHIFT2, MOD_SCALE2, MOD_GATE2 = range(6)

TOKEN_BLOCK = 1024
N_TOKEN_BLOCKS = T_ALL // TOKEN_BLOCK
N_CTX_BLOCKS = T_CTX // TOKEN_BLOCK
MERGE_BLOCK = 512
ROUTE_BLOCK = 256
EXPERT_BLOCK = 1024
I1_PER_BLOCK = EXPERT_BLOCK // PEER_KEYS
VMEM_LIMIT = 56 * 1024 * 1024
BF16_TILE_ROWS = 16
F32_TILE_ROWS = 8
COND_ROWS = F32_TILE_ROWS
assert 1 + DEC_BATCH <= COND_ROWS

F32 = jnp.float32
BF16 = jnp.bfloat16
NT_DIMS = (((1,), (1,)), ((), ()))


def _params(*semantics):
    return pltpu.CompilerParams(dimension_semantics=semantics, vmem_limit_bytes=VMEM_LIMIT)


def _mod_kernel(c_ref, w_ref, b_ref, o_ref):
    c = c_ref[...]
    s = c * jax.nn.sigmoid(c)
    o_ref[0] = jnp.dot(s, w_ref[0], preferred_element_type=F32,
                       precision=lax.Precision.HIGHEST) + b_ref[0]


MOD_COLS_PER_STEP = 3 * D_MODEL


def _modulation(cvec, w_mod, b_mod):
    tn = MOD_COLS_PER_STEP
    ncol = N_MOD * D_MODEL
    return pl.pallas_call(
        _mod_kernel,
        grid=(DEPTH, ncol // tn),
        in_specs=[pl.BlockSpec((COND_ROWS, D_MODEL), lambda l, n: (0, 0)),
                  pl.BlockSpec((1, D_MODEL, tn), lambda l, n: (l, 0, n)),
                  pl.BlockSpec((1, 1, tn), lambda l, n: (l, 0, n))],
        out_specs=pl.BlockSpec((1, COND_ROWS, tn), lambda l, n: (l, 0, n)),
        out_shape=jax.ShapeDtypeStruct((DEPTH, COND_ROWS, ncol), F32),
        compiler_params=_params("arbitrary", "arbitrary"),
        name="adaln_table",
    )(cvec, w_mod, b_mod.reshape(DEPTH, 1, ncol))


def _rms(x, g):
    return x * lax.rsqrt(jnp.mean(x * x, axis=-1, keepdims=True) + RMS_EPS) * g


def _norm_mod_kernel(x_ref, g_ref, m_ref, o_ref):
    y = _rms(x_ref[...], g_ref[...])
    h = y * (1.0 + m_ref[MOD_SCALE1:MOD_SCALE1 + 1, :]) + m_ref[MOD_SHIFT1:MOD_SHIFT1 + 1, :]
    o_ref[...] = h.astype(BF16)


def _norm_mod(x, g, mod_tok):
    return pl.pallas_call(
        _norm_mod_kernel,
        grid=(N_TOKEN_BLOCKS,),
        in_specs=[pl.BlockSpec((TOKEN_BLOCK, D_MODEL), lambda i: (i, 0)),
                  pl.BlockSpec((1, D_MODEL), lambda i: (0, 0)),
                  pl.BlockSpec((None, N_MOD, D_MODEL), lambda i: (i, 0, 0))],
        out_specs=pl.BlockSpec((TOKEN_BLOCK, D_MODEL), lambda i: (i, 0)),
        out_shape=jax.ShapeDtypeStruct((T_ALL, D_MODEL), BF16),
        compiler_params=_params("arbitrary"),
        name="norm_modulate",
    )(x, g.reshape(1, D_MODEL), mod_tok)


def _in_proj_kernel(h_ref, w_ref, *rest):
    z_ref, k_ref, v_ref, wb_scr = rest[-4:]
    n = pl.program_id(0)
    i = pl.program_id(1)

    @pl.when(i == 0)
    def _():
        wb_scr[...] = w_ref[...].astype(BF16)

    z = jnp.dot(h_ref[...], wb_scr[...], preferred_element_type=F32)
    z_ref[...] = z.astype(BF16)
    per_block = TOKEN_BLOCK // SEQ

    @pl.when((n == COL_K) & (i < N_CTX_BLOCKS))
    def _():
        k_ref[...] = z.reshape(per_block, SEQ, NA_HEADS, NA_HEAD_DIM)

    @pl.when((n == COL_V) & (i < N_CTX_BLOCKS))
    def _():
        v_ref[...] = z.reshape(per_block, SEQ, NA_HEADS, NA_HEAD_DIM)


def _cache_block_index(col, layer, n, i):
    last = N_CTX_BLOCKS - 1
    blk = jnp.where(n < col, 0, jnp.where(n > col, last, jnp.minimum(i, last)))
    return (blk, layer, 0, 0, 0)


def _in_proj(h, w_in, layer, cache_k, cache_v):
    per_block = TOKEN_BLOCK // SEQ
    cache_shape = jax.ShapeDtypeStruct((BATCH, DEPTH, SEQ, NA_HEADS, NA_HEAD_DIM), F32)
    cache_spec = lambda col: pl.BlockSpec(
        (per_block, None, SEQ, NA_HEADS, NA_HEAD_DIM), functools.partial(_cache_block_index, col, layer))
    in_specs = [pl.BlockSpec((TOKEN_BLOCK, D_MODEL), lambda n, i: (i, 0)),
                pl.BlockSpec((None, D_MODEL, D_MODEL), lambda n, i: (layer, 0, n))]
    args = [h, w_in]
    aliases = {}
    if cache_k is not None:
        in_specs += [pl.BlockSpec(memory_space=pl.ANY)] * 2
        args += [cache_k, cache_v]
        aliases = {2: 1, 3: 2}
    return pl.pallas_call(
        _in_proj_kernel,
        grid=(IN_COLS // D_MODEL, N_TOKEN_BLOCKS),
        in_specs=in_specs,
        out_specs=[pl.BlockSpec((TOKEN_BLOCK, D_MODEL), lambda n, i: (i, n)),
                   cache_spec(COL_K), cache_spec(COL_V)],
        out_shape=[jax.ShapeDtypeStruct((T_ALL, IN_COLS), BF16), cache_shape, cache_shape],
        scratch_shapes=[pltpu.VMEM((D_MODEL, D_MODEL), BF16)],
        input_output_aliases=aliases,
        compiler_params=_params("arbitrary", "arbitrary"),
        name="in_proj",
    )(*args)


def _conv_kernel(cb_ref, cc_ref, cx_ref, w_ref, o_ref):
    i = pl.program_id(0)
    u = cc_ref[...].astype(F32) * cx_ref[...].astype(F32)
    seq = jnp.where(i < N_CTX_BLOCKS, SEQ, DEC_SEQ)
    pos = lax.broadcasted_iota(jnp.int32, (TOKEN_BLOCK, 1), 0) & (seq - 1)
    prev = jnp.where(pos == 0, 0.0, pltpu.roll(u, 1, 0))
    nxt = jnp.where(pos == seq - 1, 0.0, pltpu.roll(u, TOKEN_BLOCK - 1, 0))
    y = w_ref[0:1, :] * prev + w_ref[1:2, :] * u + w_ref[2:3, :] * nxt
    o_ref[...] = (cb_ref[...].astype(F32) * y).astype(BF16)


def _short_conv(z, conv_w):
    col = lambda c: pl.BlockSpec((TOKEN_BLOCK, D_MODEL), lambda i: (i, c))
    return pl.pallas_call(
        _conv_kernel,
        grid=(N_TOKEN_BLOCKS,),
        in_specs=[col(COL_CB), col(COL_CC), col(COL_CX),
                  pl.BlockSpec((CONV_K, D_MODEL), lambda i: (0, 0))],
        out_specs=pl.BlockSpec((TOKEN_BLOCK, D_MODEL), lambda i: (i, 0)),
        out_shape=jax.ShapeDtypeStruct((T_ALL, D_MODEL), BF16),
        compiler_params=_params("arbitrary"),
        name="short_conv",
    )(z, z, z, conv_w)


ATT_SCALE = NA_HEAD_DIM ** -0.5


def _ctx_attn_kernel(q_ref, k_ref, v_ref, o_ref):
    for h in range(NA_HEADS):
        sl = slice(h * NA_HEAD_DIM, (h + 1) * NA_HEAD_DIM)
        s = lax.dot_general(q_ref[:, sl], k_ref[:, sl], NT_DIMS,
                            preferred_element_type=F32) * ATT_SCALE
        p = jnp.exp(s - jnp.max(s, axis=-1, keepdims=True))
        o = jnp.dot(p.astype(BF16), v_ref[:, sl], preferred_element_type=F32)
        o_ref[:, sl] = (o / jnp.sum(p, axis=-1, keepdims=True)).astype(BF16)


def _ctx_attention(z):
    col = lambda c: pl.BlockSpec((SEQ, D_MODEL), lambda b: (b, c))
    return pl.pallas_call(
        _ctx_attn_kernel,
        grid=(BATCH,),
        in_specs=[col(COL_Q), col(COL_K), col(COL_V)],
        out_specs=pl.BlockSpec((SEQ, D_MODEL), lambda b: (b, 0)),
        out_shape=jax.ShapeDtypeStruct((T_ALL, D_MODEL), BF16),
        compiler_params=_params("arbitrary"),
        name="ctx_attention",
    )(z, z, z)


NA_GRID_ROWS = DEC_SEQ // GRID_W
NA_WIN_ROWS = min(NA_MAX_ROWS, NA_GRID_ROWS)


def _na_row_groups():
    groups = []
    for r in range(NA_GRID_ROWS):
        r0 = min(max(r - NA_WIN_ROWS // 2, 0), NA_GRID_ROWS - NA_WIN_ROWS)
        if groups and groups[-1][0] == r0:
            groups[-1][1].append(r)
        else:
            groups.append((r0, [r]))
    return groups


def _na_kernel(q_ref, k_ref, v_ref, ck_ref, cv_ref, b_ref, att_in_ref, o_ref):
    del att_in_ref
    pairs = NA_WIN_ROWS // 2
    ck = ck_ref[...].astype(BF16)
    cv = cv_ref[...].astype(BF16)
    for r0, q_rows in _na_row_groups():
        rows = slice(q_rows[0] * GRID_W, (q_rows[-1] + 1) * GRID_W)
        win = slice(r0 * GRID_W, (r0 + NA_WIN_ROWS) * GRID_W)
        q = q_ref[rows, :]
        slab = lambda d: jnp.concatenate([b_ref[d + 2 * m] for m in range(pairs)], axis=1)
        bias = jnp.concatenate([slab(r0 - r + NA_MAX_ROWS - 1) for r in q_rows], axis=0)
        s_loc = lax.dot_general(q, k_ref[win, :], NT_DIMS, preferred_element_type=F32) * ATT_SCALE + bias
        s_ctx = lax.dot_general(q, ck, NT_DIMS, preferred_element_type=F32) * ATT_SCALE
        m = jnp.maximum(jnp.max(s_loc, axis=-1, keepdims=True), jnp.max(s_ctx, axis=-1, keepdims=True))
        p_loc = jnp.exp(s_loc - m)
        p_ctx = jnp.exp(s_ctx - m)
        den = jnp.sum(p_loc, axis=-1, keepdims=True) + jnp.sum(p_ctx, axis=-1, keepdims=True)
        o = (jnp.dot(p_loc.astype(BF16), v_ref[win, :], preferred_element_type=F32)
             + jnp.dot(p_ctx.astype(BF16), cv, preferred_element_type=F32))
        o_ref[rows, :] = (o / den).astype(BF16)


NA_PAIR_TABLES = 2 * NA_MAX_ROWS - 2


def _na_bias(rpb):
    assert NA_WIN_ROWS % 2 == 0
    c = np.arange(GRID_W)
    c0 = np.clip(c - NA_COLS // 2, 0, GRID_W - NA_COLS)
    in_cols = (c[None, :] >= c0[:, None]) & (c[None, :] < c0[:, None] + NA_COLS)
    dc = c[None, :] - c[:, None] + (NA_COLS - 1)
    pick = (dc[None] == np.arange(2 * NA_COLS - 1)[:, None, None]) & in_cols[None]
    t = jnp.einsum('lhdj,jqk->lhdqk', rpb.astype(F32), jnp.asarray(pick, F32), precision=lax.Precision.HIGHEST)
    t = jnp.where(in_cols, t, NEG_INF)
    return jnp.concatenate([t[:, :, :-1], t[:, :, 1:]], axis=-1)


def _na_attention(z, cache_k, cache_v, bias, att, layer):
    lat0 = T_CTX // DEC_SEQ
    col = lambda c: pl.BlockSpec((DEC_SEQ, NA_HEAD_DIM), lambda h, b: (lat0 + b, c * NA_HEADS + h))
    cache = pl.BlockSpec((None, None, PAST_LEN, NA_HEAD_DIM), lambda h, b: (b, layer, 0, h))
    return pl.pallas_call(
        _na_kernel,
        grid=(NA_HEADS, DEC_BATCH),
        in_specs=[col(COL_Q), col(COL_K), col(COL_V), cache, cache,
                  pl.BlockSpec((None, None, NA_PAIR_TABLES, GRID_W, 2 * GRID_W), lambda h, b: (layer, h, 0, 0, 0)),
                  pl.BlockSpec(memory_space=pl.ANY)],
        out_specs=pl.BlockSpec((DEC_SEQ, NA_HEAD_DIM), lambda h, b: (lat0 + b, h)),
        out_shape=jax.ShapeDtypeStruct((T_ALL, D_MODEL), BF16),
        input_output_aliases={6: 0},
        compiler_params=_params("arbitrary", "arbitrary"),
        name="na_attention",
    )(z, z, z, cache_k, cache_v, bias, att)


def _dft_matrices(n):
    j = np.arange(n)
    ang = 2.0 * np.pi * ((j[:, None] * j[None, :]) % n) / n
    return np.cos(ang) / math.sqrt(n), np.sin(ang) / math.sqrt(n)


def _fourier_kernel(cs_ref, ss_ref, f_ref, w2_ref, *rest):
    o_ref = rest[-1]
    f = f_ref[...]
    cu = jnp.dot(cs_ref[...], f, preferred_element_type=F32).astype(BF16)
    su = jnp.dot(ss_ref[...], f, preferred_element_type=F32).astype(BF16)
    for g in range(FNET_GROUPS):
        sl = slice(g * FNET_GROUP_DIM, (g + 1) * FNET_GROUP_DIM)
        lhs = jnp.concatenate([cu[:, sl], su[:, sl]], axis=1)
        o_ref[:, sl] = jnp.dot(lhs, w2_ref[...], preferred_element_type=F32).astype(BF16)


FOURIER_ROWS = 256


def _fourier(z, seq, n_batch, first_row, prev):
    cs, ss = _dft_matrices(seq)
    cc, sc = _dft_matrices(FNET_GROUP_DIM)
    w2 = jnp.asarray(np.concatenate([cc, -sc], axis=0), F32).astype(BF16)
    rows = FOURIER_ROWS
    nr = seq // rows
    in_specs = [pl.BlockSpec((rows, seq), lambda b, r: (r, 0)),
                pl.BlockSpec((rows, seq), lambda b, r: (r, 0)),
                pl.BlockSpec((seq, D_MODEL), lambda b, r: (first_row // seq + b, COL_F)),
                pl.BlockSpec((2 * FNET_GROUP_DIM, FNET_GROUP_DIM), lambda b, r: (0, 0))]
    args = [jnp.asarray(cs, F32).astype(BF16), jnp.asarray(ss, F32).astype(BF16), z, w2]
    aliases = {}
    if prev is not None:
        in_specs.append(pl.BlockSpec(memory_space=pl.ANY))
        args.append(prev)
        aliases = {4: 0}
    return pl.pallas_call(
        _fourier_kernel,
        grid=(n_batch, nr),
        in_specs=in_specs,
        out_specs=pl.BlockSpec((rows, D_MODEL), lambda b, r: (first_row // rows + b * nr + r, 0)),
        out_shape=jax.ShapeDtypeStruct((T_ALL, D_MODEL), BF16),
        input_output_aliases=aliases,
        compiler_params=_params("arbitrary", "arbitrary"),
        name="fourier_mix",
    )(*args)


def _merge_kernel(uc_ref, at_ref, fo_ref, g0_ref, g1_ref, g2_ref, x_ref, wc_ref, wa_ref, wf_ref,
                  wo_ref, m_ref, gf_ref, xo_ref, h2_ref, h2t_ref):
    yc = jnp.dot(uc_ref[...], wc_ref[...], preferred_element_type=F32)
    ya = jnp.dot(at_ref[...], wa_ref[...], preferred_element_type=F32)
    yf = jnp.dot(fo_ref[...], wf_ref[...], preferred_element_type=F32)
    gate = lambda r: jax.nn.sigmoid(r[...].astype(F32))
    merged = gate(g0_ref) * yc + gate(g1_ref) * ya + gate(g2_ref) * yf
    y = jnp.dot(merged.astype(BF16), wo_ref[...], preferred_element_type=F32)
    x = x_ref[...] + m_ref[MOD_GATE1:MOD_GATE1 + 1, :] * y
    xo_ref[...] = x
    hn = _rms(x, gf_ref[...])
    h2 = hn * (1.0 + m_ref[MOD_SCALE2:MOD_SCALE2 + 1, :]) + m_ref[MOD_SHIFT2:MOD_SHIFT2 + 1, :]
    h2_ref[...] = h2.astype(BF16)
    h2t_ref[...] = h2.T.astype(BF16)


def _merge(uconv, att, four, z, x, wc, wa, wf, wo, mod_tok, g_ffn):
    per = TOKEN_BLOCK // MERGE_BLOCK
    row = pl.BlockSpec((MERGE_BLOCK, D_MODEL), lambda i: (i, 0))
    col = lambda c: pl.BlockSpec((MERGE_BLOCK, D_MODEL), lambda i: (i, c))
    wspec = pl.BlockSpec((D_MODEL, D_MODEL), lambda i: (0, 0))
    return pl.pallas_call(
        _merge_kernel,
        grid=(T_ALL // MERGE_BLOCK,),
        in_specs=[row, row, row, col(COL_G0), col(COL_G1), col(COL_G2), row,
                  wspec, wspec, wspec, wspec,
                  pl.BlockSpec((None, N_MOD, D_MODEL), lambda i: (i // per, 0, 0)),
                  pl.BlockSpec((1, D_MODEL), lambda i: (0, 0))],
        out_specs=[row, row, pl.BlockSpec((D_MODEL, MERGE_BLOCK), lambda i: (0, i))],
        out_shape=[jax.ShapeDtypeStruct((T_ALL, D_MODEL), F32),
                   jax.ShapeDtypeStruct((T_ALL, D_MODEL), BF16),
                   jax.ShapeDtypeStruct((D_MODEL, T_ALL), BF16)],
        compiler_params=_params("arbitrary"),
        name="branch_merge",
    )(uconv, att, four, z, z, z, x, wc, wa, wf, wo, mod_tok, g_ffn.reshape(1, D_MODEL))


def _candidate_cells():
    return [(a, b) for a in range(PEER_TOPK) for b in range(PEER_TOPK) if (a + 1) * (b + 1) <= PEER_TOPK]


RANK_CODE_BITS = 0xFF7F0000 - (1 << 32)
LOWEST_SCORE = -3.0e38
RANK_CODE_LIMIT = -3.2e38
NOT_TAKEN = 127
RANK_MASK = 0xFF
SMALL_COUNT_FROM = 4
COUNT_BITS = 2
COUNT_SHIFT = COUNT_BITS.bit_length() - 1
assert 1 << COUNT_SHIFT == COUNT_BITS
assert (SMALL_COUNT_FROM + 1) * (1 << COUNT_BITS) > PEER_TOPK and COUNT_BITS * PEER_TOPK <= 32


def _rank_code(k):
    return float(np.array(RANK_CODE_BITS | k, np.int32).view(np.float32))


def _store_sorted_value(val_scr, p, k, m):
    for h in range(PEER_HEADS):
        val_scr[p, k, h:h + 1, :] = m[h]


def _extract_topk_fast(p, orig_scr, s_scr, rank_scr, val_scr):
    s_scr[...] = jnp.maximum(orig_scr[p], LOWEST_SCORE)
    for k in range(PEER_TOPK):
        cur = s_scr[...]
        m = jnp.max(cur, axis=1, keepdims=True)
        s_scr[...] = jnp.where(cur == m, _rank_code(k), cur)
        _store_sorted_value(val_scr, p, k, m)
    coded = s_scr[...]
    taken = coded < RANK_CODE_LIMIT
    rank_scr[p] = jnp.where(taken, pltpu.bitcast(coded, jnp.int32) & RANK_MASK, NOT_TAKEN)
    count = jnp.sum(jnp.where(taken, 1.0, 0.0), axis=1)
    clamped = jnp.where(val_scr[p, PEER_TOPK - 1] <= LOWEST_SCORE, 1.0, 0.0)
    return jnp.max(jnp.abs(count - float(PEER_TOPK)) + clamped)


def _extract_topk_ties(p, orig_scr, s_scr, rank_scr, val_scr):
    shape = (PEER_HEADS, PEER_KEYS, ROUTE_BLOCK)
    key_iota = lax.broadcasted_iota(jnp.int32, shape, 1)
    s_scr[...] = orig_scr[p]
    rank_scr[p] = jnp.full(shape, NOT_TAKEN, jnp.int32)
    for k in range(PEER_TOPK):
        cur = s_scr[...]
        m = jnp.max(cur, axis=1, keepdims=True)
        first = jnp.min(jnp.where(cur == m, key_iota, PEER_KEYS), axis=1, keepdims=True)
        sel = key_iota == first
        s_scr[...] = jnp.where(sel, -jnp.inf, cur)
        rank_scr[p] = jnp.where(sel, k, rank_scr[p])
        _store_sorted_value(val_scr, p, k, m)


def _route_kernel(h_ref, wpq_ref, keys_ref, r2_ref, e2_ref, n1_ref, c1_ref,
                  s_scr, orig_scr, rank_scr, val_scr, n_scr, small_scr):
    tb = ROUTE_BLOCK
    q = jnp.dot(h_ref[...], wpq_ref[...], preferred_element_type=F32).astype(BF16)

    for p in range(2):
        piece = lambda h: slice((2 * h + p) * PEER_KEY_DIM, (2 * h + p + 1) * PEER_KEY_DIM)
        qp = jnp.concatenate([q[:, piece(h)] for h in range(PEER_HEADS)], axis=1)
        s = lax.dot_general(keys_ref[p], qp, NT_DIMS, preferred_element_type=F32)
        orig_scr[p] = s.reshape(PEER_HEADS, PEER_KEYS, tb)

    off = 0.0
    for p in range(2):
        off = jnp.maximum(off, _extract_topk_fast(p, orig_scr, s_scr, rank_scr, val_scr))

    @pl.when(off > 0.0)
    def _():
        for p in range(2):
            _extract_topk_ties(p, orig_scr, s_scr, rank_scr, val_scr)

    v1 = [val_scr[0, a] for a in range(PEER_TOPK)]
    v2 = [val_scr[1, b] for b in range(PEER_TOPK)]
    cells = _candidate_cells()
    sums = {c: v1[c[0]] + v2[c[1]] for c in cells}
    undecided = lambda c, d: not (d[0] >= c[0] and d[1] >= c[1])
    before = {c: float((c[0] + 1) * (c[1] + 1) - 1 + sum(undecided(c, d) for d in cells[ci + 1:]))
              for ci, c in enumerate(cells)}
    for ci, c in enumerate(cells):
        for d in cells[ci + 1:]:
            if undecided(c, d):
                first = jnp.where(sums[c] >= sums[d], 1.0, 0.0)
                before[d] = before[d] + first
                before[c] = before[c] - first
    e1 = [jnp.exp(v1[a] - v1[0]) for a in range(PEER_TOPK)]
    e2 = [jnp.exp(v2[b] - v2[0]) for b in range(PEER_TOPK)]
    zsum = 0.0
    count = [0.0] * PEER_TOPK
    for c in cells:
        chosen = before[c] < float(PEER_TOPK)
        count[c[0]] = count[c[0]] + jnp.where(chosen, 1.0, 0.0)
        zsum = zsum + jnp.where(chosen, e1[c[0]] * e2[c[1]], 0.0)
    small = jnp.zeros((PEER_HEADS, tb), jnp.int32)
    for a in range(PEER_TOPK):
        if a < SMALL_COUNT_FROM:
            n_scr[a] = count[a]
        else:
            small = small | (count[a].astype(jnp.int32) << (COUNT_BITS * a))
    small_scr[...] = small
    n_scr[PEER_TOPK] = 0.5 / zsum

    for h in range(PEER_HEADS):
        row = pl.ds(h, 1)
        rank1 = rank_scr[0, h]
        taken1 = rank1 < PEER_TOPK
        r = jnp.minimum(rank1, PEER_TOPK - 1)
        packed = jnp.broadcast_to(small_scr[row, :], r.shape)
        n1 = (lax.shift_right_logical(packed, r << COUNT_SHIFT) & ((1 << COUNT_BITS) - 1)).astype(F32)
        for a in reversed(range(SMALL_COUNT_FROM)):
            n1 = jnp.where(r == a, n_scr[a, row, :], n1)
        n1_ref[h] = jnp.where(taken1, n1, 0.0)
        w1 = jnp.where(taken1, jnp.exp(orig_scr[0, h] - val_scr[0, 0, row, :]), 0.0)
        c1_ref[h] = w1 * n_scr[PEER_TOPK, row, :]
        rank2 = rank_scr[1, h]
        r2_ref[h] = rank2.astype(F32).astype(BF16)
        w2 = jnp.where(rank2 < PEER_TOPK, jnp.exp(orig_scr[1, h] - val_scr[1, 0, row, :]), 0.0)
        e2_ref[h] = w2.astype(BF16)


def _route(h2, wpq, keys, layer):
    tb = ROUTE_BLOCK
    out = lambda dt: jax.ShapeDtypeStruct((PEER_HEADS, PEER_KEYS, T_ALL), dt)
    ospec = pl.BlockSpec((PEER_HEADS, PEER_KEYS, tb), lambda i: (0, 0, i))
    hk = PEER_HEADS * PEER_KEYS
    return pl.pallas_call(
        _route_kernel,
        grid=(T_ALL // tb,),
        in_specs=[pl.BlockSpec((tb, D_MODEL), lambda i: (i, 0)),
                  pl.BlockSpec((None, D_MODEL, 2 * hk), lambda i: (layer, 0, 0)),
                  pl.BlockSpec((None, 2, hk, hk), lambda i: (layer, 0, 0, 0))],
        out_specs=[ospec, ospec, ospec, ospec],
        out_shape=[out(BF16), out(BF16), out(F32), out(F32)],
        scratch_shapes=[pltpu.VMEM((PEER_HEADS, PEER_KEYS, tb), F32),
                        pltpu.VMEM((2, PEER_HEADS, PEER_KEYS, tb), F32),
                        pltpu.VMEM((2, PEER_HEADS, PEER_KEYS, tb), jnp.int32),
                        pltpu.VMEM((2, PEER_TOPK, PEER_HEADS, tb), F32),
                        pltpu.VMEM((PEER_TOPK + 1, PEER_HEADS, tb), F32),
                        pltpu.VMEM((PEER_HEADS, tb), jnp.int32)],
        compiler_params=_params("arbitrary"),
        name="peer_route",
    )(h2, wpq, keys)


SQRT_HALF = math.sqrt(0.5)


EXPERT_TOKEN_CHUNK = 1024
N_EXPERT_CHUNKS = TOKEN_BLOCK // EXPERT_TOKEN_CHUNK
ACT_SLOTS = min(2, N_EXPERT_CHUNKS)


def _experts_kernel(xt_ref, u_ref, vt_ref, r2_ref, e2_ref, n1_ref, c1_ref, res_ref, m_ref, g_ref, mn_ref,
                    o1_ref, o2_ref, acc_scr, act_scr, p_scr, *, last_layer):
    eb = pl.program_id(1)

    @pl.when(eb == 0)
    def _():
        acc_scr[...] = jnp.zeros_like(acc_scr)

    cols = lambda j: slice(j * EXPERT_TOKEN_CHUNK, (j + 1) * EXPERT_TOKEN_CHUNK)
    def pre_activate(j):
        act_scr[j % ACT_SLOTS] = jnp.dot(u_ref[...], xt_ref[:, cols(j)], preferred_element_type=F32)

    def activate(j):
        for c in range(I1_PER_BLOCK):
            i1 = pl.ds(eb * I1_PER_BLOCK + c, 1)
            tiles = (PEER_KEYS // BF16_TILE_ROWS, BF16_TILE_ROWS, EXPERT_TOKEN_CHUNK)
            tile_row = lambda ref, h: jnp.broadcast_to(ref[h, i1, cols(j)], tiles[1:]).astype(BF16)[None]
            gate = jnp.zeros(tiles, BF16)
            for h in range(PEER_HEADS):
                taken = r2_ref[h, :, cols(j)].reshape(tiles) < tile_row(n1_ref, h)
                e2 = e2_ref[h, :, cols(j)].reshape(tiles)
                gate = gate + jnp.where(taken, e2, jnp.zeros((), BF16)) * tile_row(c1_ref, h)
            keys = slice(c * PEER_KEYS, (c + 1) * PEER_KEYS)
            a = act_scr[j % ACT_SLOTS, keys, :]
            erf1 = 1.0 + lax.erf(a * SQRT_HALF)
            p_scr[keys, cols(j)] = (a.astype(BF16) * erf1.astype(BF16)
                                    * gate.reshape(PEER_KEYS, EXPERT_TOKEN_CHUNK))

    def mix(j):
        acc_scr[:, cols(j)] += jnp.dot(vt_ref[...], p_scr[:, cols(j)], preferred_element_type=F32)

    for stage in range(N_EXPERT_CHUNKS + 2):
        if stage < N_EXPERT_CHUNKS:
            pre_activate(stage)
        if 1 <= stage <= N_EXPERT_CHUNKS:
            activate(stage - 1)
        if stage >= 2:
            mix(stage - 2)

    @pl.when(eb == pl.num_programs(1) - 1)
    def _():
        x = res_ref[...] + m_ref[MOD_GATE2:MOD_GATE2 + 1, :] * acc_scr[...].T
        y = _rms(x, g_ref[...])
        if last_layer:
            t = pl.program_id(0)

            @pl.when(t < N_CTX_BLOCKS)
            def _():
                o1_ref[...] = y

            @pl.when(t >= N_CTX_BLOCKS)
            def _():
                o2_ref[...] = y
        else:
            o1_ref[...] = x
            h = y * (1.0 + mn_ref[MOD_SCALE1:MOD_SCALE1 + 1, :]) + mn_ref[MOD_SHIFT1:MOD_SHIFT1 + 1, :]
            o2_ref[...] = h.astype(BF16)


def _experts(h2t, u_bf, vt_bf, r2, e2, n1, c1, x_mid, mod_tok, g_next, mod_next, layer):
    once = pl.Buffered(1)
    last_layer = layer == DEPTH - 1
    tok = pl.BlockSpec((PEER_HEADS, PEER_KEYS, TOKEN_BLOCK), lambda t, e: (0, 0, t), pipeline_mode=once)
    block = (TOKEN_BLOCK, D_MODEL)
    if last_layer:
        out_specs = [pl.BlockSpec(block, lambda t, e: (jnp.minimum(t, N_CTX_BLOCKS - 1), 0), pipeline_mode=once),
                     pl.BlockSpec(block, lambda t, e: (jnp.maximum(t - N_CTX_BLOCKS, 0), 0), pipeline_mode=once)]
        out_shape = [jax.ShapeDtypeStruct((T_CTX, D_MODEL), F32), jax.ShapeDtypeStruct((T_LAT, D_MODEL), F32)]
    else:
        out_specs = [pl.BlockSpec(block, lambda t, e: (t, 0))] * 2
        out_shape = [jax.ShapeDtypeStruct((T_ALL, D_MODEL), F32), jax.ShapeDtypeStruct((T_ALL, D_MODEL), BF16)]
    return pl.pallas_call(
        functools.partial(_experts_kernel, last_layer=last_layer),
        grid=(N_TOKEN_BLOCKS, PEER_EXPERTS // EXPERT_BLOCK),
        in_specs=[pl.BlockSpec((D_MODEL, TOKEN_BLOCK), lambda t, e: (0, t)),
                  pl.BlockSpec((None, EXPERT_BLOCK, D_MODEL), lambda t, e: (layer, e, 0)),
                  pl.BlockSpec((None, D_MODEL, EXPERT_BLOCK), lambda t, e: (layer, 0, e)),
                  tok, tok, tok, tok,
                  pl.BlockSpec((TOKEN_BLOCK, D_MODEL), lambda t, e: (t, 0), pipeline_mode=once),
                  pl.BlockSpec((None, N_MOD, D_MODEL), lambda t, e: (t, 0, 0)),
                  pl.BlockSpec((1, D_MODEL), lambda t, e: (0, 0)),
                  pl.BlockSpec((None, N_MOD, D_MODEL), lambda t, e: (t, 0, 0))],
        out_specs=out_specs,
        out_shape=out_shape,
        scratch_shapes=[pltpu.VMEM((D_MODEL, TOKEN_BLOCK), F32),
                        pltpu.VMEM((ACT_SLOTS, EXPERT_BLOCK, EXPERT_TOKEN_CHUNK), F32),
                        pltpu.VMEM((EXPERT_BLOCK, TOKEN_BLOCK), BF16)],
        compiler_params=_params("arbitrary", "arbitrary"),
        name="peer_experts",
    )(h2t, u_bf, vt_bf, r2, e2, n1, c1, x_mid, mod_tok, g_next.reshape(1, D_MODEL), mod_next)


TRANSPOSE_ROWS = 1024


def _transpose_cast_kernel(v_ref, o_ref):
    o_ref[...] = v_ref[...].T.astype(BF16)


def _transpose_cast(peer_v):
    return pl.pallas_call(
        _transpose_cast_kernel,
        grid=(DEPTH, PEER_EXPERTS // TRANSPOSE_ROWS),
        in_specs=[pl.BlockSpec((None, TRANSPOSE_ROWS, D_MODEL), lambda l, e: (l, e, 0))],
        out_specs=pl.BlockSpec((None, D_MODEL, TRANSPOSE_ROWS), lambda l, e: (l, 0, e)),
        out_shape=jax.ShapeDtypeStruct((DEPTH, D_MODEL, PEER_EXPERTS), BF16),
        compiler_params=_params("arbitrary", "arbitrary"),
        name="expert_table_transpose",
    )(peer_v)


def _block_diag_keys(sub_keys):
    eye = jnp.eye(PEER_HEADS, dtype=sub_keys.dtype)
    keys = jnp.einsum('lhpjd,hg->lphjgd', sub_keys, eye)
    return keys.reshape(DEPTH, 2, PEER_HEADS * PEER_KEYS, PEER_HEADS * PEER_KEY_DIM).astype(BF16)


def kernel(x_prompt, x_sample, cache_k, cache_v, c, c_ctx, w_in, conv_w, w_conv_out, rpb, w_attn_out,
           w_four_out, w_o, g_mix, g_ffn, w_mod, b_mod, w_pq, sub_keys, peer_u, peer_v, g_final):
    assert SEQ & (SEQ - 1) == 0 and DEC_SEQ & (DEC_SEQ - 1) == 0
    x = jnp.concatenate([x_prompt.reshape(T_CTX, D_MODEL), x_sample.reshape(T_LAT, D_MODEL)], axis=0)
    cvec = jnp.zeros((COND_ROWS, D_MODEL), F32).at[0].set(c_ctx).at[1:1 + DEC_BATCH].set(c)
    mod = _modulation(cvec, w_mod, b_mod)
    lat_per_block = DEC_SEQ // TOKEN_BLOCK
    block_row = np.array([0] * N_CTX_BLOCKS
                         + [1 + b for b in range(DEC_BATCH) for _ in range(lat_per_block)])
    mod_tok = mod[:, block_row].reshape(DEPTH, N_TOKEN_BLOCKS, N_MOD, D_MODEL)
    cache_k4 = cache_k.reshape(DEC_BATCH, DEPTH, PAST_LEN, D_MODEL)
    cache_v4 = cache_v.reshape(DEC_BATCH, DEPTH, PAST_LEN, D_MODEL)

    na_bias = _na_bias(rpb)
    keys = _block_diag_keys(sub_keys)
    w_pq_bf = w_pq.astype(BF16)
    u_bf = peer_u.astype(BF16)
    vt_bf = _transpose_cast(peer_v)

    new_k = new_v = None
    h = _norm_mod(x, g_mix[0], mod_tok[0])
    for l in range(DEPTH):
        z, new_k, new_v = _in_proj(h, w_in, l, new_k, new_v)
        uconv = _short_conv(z, conv_w[l])
        att = _ctx_attention(z)
        att = _na_attention(z, cache_k4, cache_v4, na_bias, att, l)
        four = _fourier(z, SEQ, BATCH, 0, None)
        four = _fourier(z, DEC_SEQ, DEC_BATCH, T_CTX, four)
        x_mid, h2, h2t = _merge(uconv, att, four, z, x,
                           w_conv_out[l].astype(BF16), w_attn_out[l].astype(BF16),
                           w_four_out[l].astype(BF16), w_o[l].astype(BF16), mod_tok[l], g_ffn[l])
        r2, e2, n1, c1 = _route(h2, w_pq_bf, keys, l)
        if l + 1 < DEPTH:
            x, h = _experts(h2t, u_bf, vt_bf, r2, e2, n1, c1, x_mid, mod_tok[l], g_mix[l + 1], mod_tok[l + 1], l)
        else:
            y_ctx, y_lat = _experts(h2t, u_bf, vt_bf, r2, e2, n1, c1, x_mid, mod_tok[l], g_final, mod_tok[l], l)

    y_prompt = y_ctx.reshape(BATCH, SEQ, D_MODEL)
    y_sample = y_lat.reshape(DEC_BATCH, DEC_SEQ, D_MODEL)
    return (y_prompt, y_sample, new_k, new_v)
```

```python
import functools
import math

import numpy as np
import jax
import jax.numpy as jnp
from jax import lax
from jax.experimental import pallas as pl
from jax.experimental.pallas import tpu as pltpu

D_MODEL = 1024
BATCH = 16
SEQ = 256
DEPTH = 2
DEC_BATCH = 2
DEC_SEQ = 1024
PAST_LEN = 512
GRID_W = 64
CONV_K = 3
NA_HEADS = 8
NA_HEAD_DIM = D_MODEL // NA_HEADS
NA_MAX_ROWS = 8
NA_COLS = 16
FNET_GROUPS = 4
FNET_GROUP_DIM = D_MODEL // FNET_GROUPS
N_BRANCH = 3
IN_COLS = 10 * D_MODEL
PEER_HEADS = 8
PEER_KEYS = 128
PEER_EXPERTS = PEER_KEYS * PEER_KEYS
PEER_KEY_DIM = 128
PEER_TOPK = 16
N_MOD = 6
RMS_EPS = 1e-6
NEG_INF = -1e30

T_CTX = BATCH * SEQ
T_LAT = DEC_BATCH * DEC_SEQ
T_ALL = T_CTX + T_LAT

COL_CB, COL_CC, COL_CX, COL_Q, COL_K, COL_V, COL_F, COL_G0, COL_G1, COL_G2 = range(10)
MOD_SHIFT1, MOD_SCALE1, MOD_GATE1, MOD_SHIFT2, MOD_SCALE2, MOD_GATE2 = range(6)

TOKEN_BLOCK = 1024
N_TOKEN_BLOCKS = T_ALL // TOKEN_BLOCK
N_CTX_BLOCKS = T_CTX // TOKEN_BLOCK
MERGE_BLOCK = 512
ROUTE_BLOCK = 256
EXPERT_BLOCK = 1024
I1_PER_BLOCK = EXPERT_BLOCK // PEER_KEYS
VMEM_LIMIT = 56 * 1024 * 1024
BF16_TILE_ROWS = 16
F32_TILE_ROWS = 8
COND_ROWS = F32_TILE_ROWS
assert 1 + DEC_BATCH <= COND_ROWS

F32 = jnp.float32
BF16 = jnp.bfloat16
NT_DIMS = (((1,), (1,)), ((), ()))


def _params(*semantics):
    return pltpu.CompilerParams(dimension_semantics=semantics, vmem_limit_bytes=VMEM_LIMIT)


def _mod_kernel(c_ref, w_ref, b_ref, o_ref):
    c = c_ref[...]
    s = c * jax.nn.sigmoid(c)
    o_ref[0] = jnp.dot(s, w_ref[0], preferred_element_type=F32,
                       precision=lax.Precision.HIGHEST) + b_ref[0]


MOD_COLS_PER_STEP = 3 * D_MODEL // 2


def _modulation(cvec, w_mod, b_mod):
    tn = MOD_COLS_PER_STEP
    ncol = N_MOD * D_MODEL
    return pl.pallas_call(
        _mod_kernel,
        grid=(DEPTH, ncol // tn),
        in_specs=[pl.BlockSpec((COND_ROWS, D_MODEL), lambda l, n: (0, 0)),
                  pl.BlockSpec((1, D_MODEL, tn), lambda l, n: (l, 0, n)),
                  pl.BlockSpec((1, 1, tn), lambda l, n: (l, 0, n))],
        out_specs=pl.BlockSpec((1, COND_ROWS, tn), lambda l, n: (l, 0, n)),
        out_shape=jax.ShapeDtypeStruct((DEPTH, COND_ROWS, ncol), F32),
        compiler_params=_params("arbitrary", "arbitrary"),
        name="adaln_table",
    )(cvec, w_mod, b_mod.reshape(DEPTH, 1, ncol))


def _rms(x, g):
    return x * lax.rsqrt(jnp.mean(x * x, axis=-1, keepdims=True) + RMS_EPS) * g


def _norm_mod_kernel(x_ref, g_ref, m_ref, o_ref):
    y = _rms(x_ref[...], g_ref[...])
    h = y * (1.0 + m_ref[MOD_SCALE1:MOD_SCALE1 + 1, :]) + m_ref[MOD_SHIFT1:MOD_SHIFT1 + 1, :]
    o_ref[...] = h.astype(BF16)


def _norm_mod(x, g, mod_tok):
    return pl.pallas_call(
        _norm_mod_kernel,
        grid=(N_TOKEN_BLOCKS,),
        in_specs=[pl.BlockSpec((TOKEN_BLOCK, D_MODEL), lambda i: (i, 0)),
                  pl.BlockSpec((1, D_MODEL), lambda i: (0, 0)),
                  pl.BlockSpec((None, N_MOD, D_MODEL), lambda i: (i, 0, 0))],
        out_specs=pl.BlockSpec((TOKEN_BLOCK, D_MODEL), lambda i: (i, 0)),
        out_shape=jax.ShapeDtypeStruct((T_ALL, D_MODEL), BF16),
        compiler_params=_params("arbitrary"),
        name="norm_modulate",
    )(x, g.reshape(1, D_MODEL), mod_tok)


def _in_proj_kernel(h_ref, w_ref, *rest):
    z_ref, k_ref, v_ref, wb_scr = rest[-4:]
    n = pl.program_id(0)
    i = pl.program_id(1)

    @pl.when(i == 0)
    def _():
        wb_scr[...] = w_ref[...].astype(BF16)

    z = jnp.dot(h_ref[...], wb_scr[...], preferred_element_type=F32)
    z_ref[...] = z.astype(BF16)
    per_block = TOKEN_BLOCK // SEQ

    @pl.when((n == COL_K) & (i < N_CTX_BLOCKS))
    def _():
        k_ref[...] = z.reshape(per_block, SEQ, NA_HEADS, NA_HEAD_DIM)

    @pl.when((n == COL_V) & (i < N_CTX_BLOCKS))
    def _():
        v_ref[...] = z.reshape(per_block, SEQ, NA_HEADS, NA_HEAD_DIM)


def _cache_block_index(col, layer, n, i):
    last = N_CTX_BLOCKS - 1
    blk = jnp.where(n < col, 0, jnp.where(n > col, last, jnp.minimum(i, last)))
    return (blk, layer, 0, 0, 0)


def _in_proj(h, w_in, layer, cache_k, cache_v):
    per_block = TOKEN_BLOCK // SEQ
    cache_shape = jax.ShapeDtypeStruct((BATCH, DEPTH, SEQ, NA_HEADS, NA_HEAD_DIM), F32)
    cache_spec = lambda col: pl.BlockSpec(
        (per_block, None, SEQ, NA_HEADS, NA_HEAD_DIM), functools.partial(_cache_block_index, col, layer))
    in_specs = [pl.BlockSpec((TOKEN_BLOCK, D_MODEL), lambda n, i: (i, 0)),
                pl.BlockSpec((None, D_MODEL, D_MODEL), lambda n, i: (layer, 0, n))]
    args = [h, w_in]
    aliases = {}
    if cache_k is not None:
        in_specs += [pl.BlockSpec(memory_space=pl.ANY)] * 2
        args += [cache_k, cache_v]
        aliases = {2: 1, 3: 2}
    return pl.pallas_call(
        _in_proj_kernel,
        grid=(IN_COLS // D_MODEL, N_TOKEN_BLOCKS),
        in_specs=in_specs,
        out_specs=[pl.BlockSpec((TOKEN_BLOCK, D_MODEL), lambda n, i: (i, n)),
                   cache_spec(COL_K), cache_spec(COL_V)],
        out_shape=[jax.ShapeDtypeStruct((T_ALL, IN_COLS), BF16), cache_shape, cache_shape],
        scratch_shapes=[pltpu.VMEM((D_MODEL, D_MODEL), BF16)],
        input_output_aliases=aliases,
        compiler_params=_params("arbitrary", "arbitrary"),
        name="in_proj",
    )(*args)


def _conv_kernel(cb_ref, cc_ref, cx_ref, w_ref, o_ref):
    i = pl.program_id(0)
    u = cc_ref[...].astype(F32) * cx_ref[...].astype(F32)
    seq = jnp.where(i < N_CTX_BLOCKS, SEQ, DEC_SEQ)
    pos = lax.broadcasted_iota(jnp.int32, (TOKEN_BLOCK, 1), 0) & (seq - 1)
    prev = jnp.where(pos == 0, 0.0, pltpu.roll(u, 1, 0))
    nxt = jnp.where(pos == seq - 1, 0.0, pltpu.roll(u, TOKEN_BLOCK - 1, 0))
    y = w_ref[0:1, :] * prev + w_ref[1:2, :] * u + w_ref[2:3, :] * nxt
    o_ref[...] = (cb_ref[...].astype(F32) * y).astype(BF16)


def _short_conv(z, conv_w):
    col = lambda c: pl.BlockSpec((TOKEN_BLOCK, D_MODEL), lambda i: (i, c))
    return pl.pallas_call(
        _conv_kernel,
        grid=(N_TOKEN_BLOCKS,),
        in_specs=[col(COL_CB), col(COL_CC), col(COL_CX),
                  pl.BlockSpec((CONV_K, D_MODEL), lambda i: (0, 0))],
        out_specs=pl.BlockSpec((TOKEN_BLOCK, D_MODEL), lambda i: (i, 0)),
        out_shape=jax.ShapeDtypeStruct((T_ALL, D_MODEL), BF16),
        compiler_params=_params("arbitrary"),
        name="short_conv",
    )(z, z, z, conv_w)


ATT_SCALE = NA_HEAD_DIM ** -0.5


def _ctx_attn_kernel(q_ref, k_ref, v_ref, o_ref):
    for h in range(NA_HEADS):
        sl = slice(h * NA_HEAD_DIM, (h + 1) * NA_HEAD_DIM)
        s = lax.dot_general(q_ref[:, sl], k_ref[:, sl], NT_DIMS,
                            preferred_element_type=F32) * ATT_SCALE
        p = jnp.exp(s - jnp.max(s, axis=-1, keepdims=True))
        o = jnp.dot(p.astype(BF16), v_ref[:, sl], preferred_element_type=F32)
        o_ref[:, sl] = (o / jnp.sum(p, axis=-1, keepdims=True)).astype(BF16)


def _ctx_attention(z):
    col = lambda c: pl.BlockSpec((SEQ, D_MODEL), lambda b: (b, c))
    return pl.pallas_call(
        _ctx_attn_kernel,
        grid=(BATCH,),
        in_specs=[col(COL_Q), col(COL_K), col(COL_V)],
        out_specs=pl.BlockSpec((SEQ, D_MODEL), lambda b: (b, 0)),
        out_shape=jax.ShapeDtypeStruct((T_ALL, D_MODEL), BF16),
        compiler_params=_params("arbitrary"),
        name="ctx_attention",
    )(z, z, z)


NA_GRID_ROWS = DEC_SEQ // GRID_W
NA_WIN_ROWS = min(NA_MAX_ROWS, NA_GRID_ROWS)


def _na_row_groups():
    groups = []
    for r in range(NA_GRID_ROWS):
        r0 = min(max(r - NA_WIN_ROWS // 2, 0), NA_GRID_ROWS - NA_WIN_ROWS)
        if groups and groups[-1][0] == r0:
            groups[-1][1].append(r)
        else:
            groups.append((r0, [r]))
    return groups


def _na_kernel(q_ref, k_ref, v_ref, ck_ref, cv_ref, b_ref, att_in_ref, o_ref):
    del att_in_ref
    pairs = NA_WIN_ROWS // 2
    ck = ck_ref[...].astype(BF16)
    cv = cv_ref[...].astype(BF16)
    for r0, q_rows in _na_row_groups():
        rows = slice(q_rows[0] * GRID_W, (q_rows[-1] + 1) * GRID_W)
        win = slice(r0 * GRID_W, (r0 + NA_WIN_ROWS) * GRID_W)
        q = q_ref[rows, :]
        slab = lambda d: jnp.concatenate([b_ref[d + 2 * m] for m in range(pairs)], axis=1)
        bias = jnp.concatenate([slab(r0 - r + NA_MAX_ROWS - 1) for r in q_rows], axis=0)
        s_loc = lax.dot_general(q, k_ref[win, :], NT_DIMS, preferred_element_type=F32) * ATT_SCALE + bias
        s_ctx = lax.dot_general(q, ck, NT_DIMS, preferred_element_type=F32) * ATT_SCALE
        m = jnp.maximum(jnp.max(s_loc, axis=-1, keepdims=True), jnp.max(s_ctx, axis=-1, keepdims=True))
        p_loc = jnp.exp(s_loc - m)
        p_ctx = jnp.exp(s_ctx - m)
        den = jnp.sum(p_loc, axis=-1, keepdims=True) + jnp.sum(p_ctx, axis=-1, keepdims=True)
        o = (jnp.dot(p_loc.astype(BF16), v_ref[win, :], preferred_element_type=F32)
             + jnp.dot(p_ctx.astype(BF16), cv, preferred_element_type=F32))
        o_ref[rows, :] = (o / den).astype(BF16)


NA_PAIR_TABLES = 2 * NA_MAX_ROWS - 2


def _na_bias(rpb):
    assert NA_WIN_ROWS % 2 == 0
    c = np.arange(GRID_W)
    c0 = np.clip(c - NA_COLS // 2, 0, GRID_W - NA_COLS)
    in_cols = (c[None, :] >= c0[:, None]) & (c[None, :] < c0[:, None] + NA_COLS)
    dc = c[None, :] - c[:, None] + (NA_COLS - 1)
    pick = (dc[None] == np.arange(2 * NA_COLS - 1)[:, None, None]) & in_cols[None]
    t = jnp.einsum('lhdj,jqk->lhdqk', rpb.astype(F32), jnp.asarray(pick, F32), precision=lax.Precision.HIGHEST)
    t = jnp.where(in_cols, t, NEG_INF)
    return jnp.concatenate([t[:, :, :-1], t[:, :, 1:]], axis=-1)


def _na_attention(z, cache_k, cache_v, bias, att, layer):
    lat0 = T_CTX // DEC_SEQ
    col = lambda c: pl.BlockSpec((DEC_SEQ, NA_HEAD_DIM), lambda h, b: (lat0 + b, c * NA_HEADS + h))
    cache = pl.BlockSpec((None, None, PAST_LEN, NA_HEAD_DIM), lambda h, b: (b, layer, 0, h))
    return pl.pallas_call(
        _na_kernel,
        grid=(NA_HEADS, DEC_BATCH),
        in_specs=[col(COL_Q), col(COL_K), col(COL_V), cache, cache,
                  pl.BlockSpec((None, None, NA_PAIR_TABLES, GRID_W, 2 * GRID_W), lambda h, b: (layer, h, 0, 0, 0)),
                  pl.BlockSpec(memory_space=pl.ANY)],
        out_specs=pl.BlockSpec((DEC_SEQ, NA_HEAD_DIM), lambda h, b: (lat0 + b, h)),
        out_shape=jax.ShapeDtypeStruct((T_ALL, D_MODEL), BF16),
        input_output_aliases={6: 0},
        compiler_params=_params("arbitrary", "arbitrary"),
        name="na_attention",
    )(z, z, z, cache_k, cache_v, bias, att)


def _dft_matrices(n):
    j = np.arange(n)
    ang = 2.0 * np.pi * ((j[:, None] * j[None, :]) % n) / n
    return np.cos(ang) / math.sqrt(n), np.sin(ang) / math.sqrt(n)


def _fourier_kernel(cs_ref, ss_ref, f_ref, w2_ref, *rest):
    o_ref = rest[-1]
    f = f_ref[...]
    cu = jnp.dot(cs_ref[...], f, preferred_element_type=F32).astype(BF16)
    su = jnp.dot(ss_ref[...], f, preferred_element_type=F32).astype(BF16)
    for g in range(FNET_GROUPS):
        sl = slice(g * FNET_GROUP_DIM, (g + 1) * FNET_GROUP_DIM)
        lhs = jnp.concatenate([cu[:, sl], su[:, sl]], axis=1)
        o_ref[:, sl] = jnp.dot(lhs, w2_ref[...], preferred_element_type=F32).astype(BF16)


FOURIER_ROWS = 256


def _fourier(z, seq, n_batch, first_row, prev):
    cs, ss = _dft_matrices(seq)
    cc, sc = _dft_matrices(FNET_GROUP_DIM)
    w2 = jnp.asarray(np.concatenate([cc, -sc], axis=0), F32).astype(BF16)
    rows = FOURIER_ROWS
    nr = seq // rows
    in_specs = [pl.BlockSpec((rows, seq), lambda b, r: (r, 0)),
                pl.BlockSpec((rows, seq), lambda b, r: (r, 0)),
                pl.BlockSpec((seq, D_MODEL), lambda b, r: (first_row // seq + b, COL_F)),
                pl.BlockSpec((2 * FNET_GROUP_DIM, FNET_GROUP_DIM), lambda b, r: (0, 0))]
    args = [jnp.asarray(cs, F32).astype(BF16), jnp.asarray(ss, F32).astype(BF16), z, w2]
    aliases = {}
    if prev is not None:
        in_specs.append(pl.BlockSpec(memory_space=pl.ANY))
        args.append(prev)
        aliases = {4: 0}
    return pl.pallas_call(
        _fourier_kernel,
        grid=(n_batch, nr),
        in_specs=in_specs,
        out_specs=pl.BlockSpec((rows, D_MODEL), lambda b, r: (first_row // rows + b * nr + r, 0)),
        out_shape=jax.ShapeDtypeStruct((T_ALL, D_MODEL), BF16),
        input_output_aliases=aliases,
        compiler_params=_params("arbitrary", "arbitrary"),
        name="fourier_mix",
    )(*args)


def _merge_kernel(uc_ref, at_ref, fo_ref, g0_ref, g1_ref, g2_ref, x_ref, wc_ref, wa_ref, wf_ref,
                  wo_ref, m_ref, gf_ref, xo_ref, h2_ref, h2t_ref):
    yc = jnp.dot(uc_ref[...], wc_ref[...], preferred_element_type=F32)
    ya = jnp.dot(at_ref[...], wa_ref[...], preferred_element_type=F32)
    yf = jnp.dot(fo_ref[...], wf_ref[...], preferred_element_type=F32)
    gate = lambda r: jax.nn.sigmoid(r[...].astype(F32))
    merged = gate(g0_ref) * yc + gate(g1_ref) * ya + gate(g2_ref) * yf
    y = jnp.dot(merged.astype(BF16), wo_ref[...], preferred_element_type=F32)
    x = x_ref[...] + m_ref[MOD_GATE1:MOD_GATE1 + 1, :] * y
    xo_ref[...] = x
    hn = _rms(x, gf_ref[...])
    h2 = hn * (1.0 + m_ref[MOD_SCALE2:MOD_SCALE2 + 1, :]) + m_ref[MOD_SHIFT2:MOD_SHIFT2 + 1, :]
    h2_ref[...] = h2.astype(BF16)
    h2t_ref[...] = h2.T.astype(BF16)


def _merge(uconv, att, four, z, x, wc, wa, wf, wo, mod_tok, g_ffn):
    per = TOKEN_BLOCK // MERGE_BLOCK
    row = pl.BlockSpec((MERGE_BLOCK, D_MODEL), lambda i: (i, 0))
    col = lambda c: pl.BlockSpec((MERGE_BLOCK, D_MODEL), lambda i: (i, c))
    wspec = pl.BlockSpec((D_MODEL, D_MODEL), lambda i: (0, 0))
    return pl.pallas_call(
        _merge_kernel,
        grid=(T_ALL // MERGE_BLOCK,),
        in_specs=[row, row, row, col(COL_G0), col(COL_G1), col(COL_G2), row,
                  wspec, wspec, wspec, wspec,
                  pl.BlockSpec((None, N_MOD, D_MODEL), lambda i: (i // per, 0, 0)),
                  pl.BlockSpec((1, D_MODEL), lambda i: (0, 0))],
        out_specs=[row, row, pl.BlockSpec((D_MODEL, MERGE_BLOCK), lambda i: (0, i))],
        out_shape=[jax.ShapeDtypeStruct((T_ALL, D_MODEL), F32),
                   jax.ShapeDtypeStruct((T_ALL, D_MODEL), BF16),
                   jax.ShapeDtypeStruct((D_MODEL, T_ALL), BF16)],
        compiler_params=_params("arbitrary"),
        name="branch_merge",
    )(uconv, att, four, z, z, z, x, wc, wa, wf, wo, mod_tok, g_ffn.reshape(1, D_MODEL))


def _candidate_cells():
    return [(a, b) for a in range(PEER_TOPK) for b in range(PEER_TOPK) if (a + 1) * (b + 1) <= PEER_TOPK]


RANK_CODE_BITS = 0xFF7F0000 - (1 << 32)
LOWEST_SCORE = -3.0e38
RANK_CODE_LIMIT = -3.2e38
NOT_TAKEN = 127
RANK_MASK = 0xFF
SMALL_COUNT_FROM = 4
COUNT_BITS = 2
COUNT_SHIFT = COUNT_BITS.bit_length() - 1
assert 1 << COUNT_SHIFT == COUNT_BITS
assert (SMALL_COUNT_FROM + 1) * (1 << COUNT_BITS) > PEER_TOPK and COUNT_BITS * PEER_TOPK <= 32


def _rank_code(k):
    return float(np.array(RANK_CODE_BITS | k, np.int32).view(np.float32))


def _store_sorted_value(val_scr, p, k, m):
    for h in range(PEER_HEADS):
        val_scr[p, k, h:h + 1, :] = m[h]


def _extract_topk_fast(p, orig_scr, s_scr, rank_scr, val_scr):
    s_scr[...] = jnp.maximum(orig_scr[p], LOWEST_SCORE)
    for k in range(PEER_TOPK):
        cur = s_scr[...]
        m = jnp.max(cur, axis=1, keepdims=True)
        s_scr[...] = jnp.where(cur == m, _rank_code(k), cur)
        _store_sorted_value(val_scr, p, k, m)
    coded = s_scr[...]
    taken = coded < RANK_CODE_LIMIT
    rank_scr[p] = jnp.where(taken, pltpu.bitcast(coded, jnp.int32) & RANK_MASK, NOT_TAKEN)
    count = jnp.sum(jnp.where(taken, 1.0, 0.0), axis=1)
    clamped = jnp.where(val_scr[p, PEER_TOPK - 1] <= LOWEST_SCORE, 1.0, 0.0)
    return jnp.max(jnp.abs(count - float(PEER_TOPK)) + clamped)


def _extract_topk_ties(p, orig_scr, s_scr, rank_scr, val_scr):
    shape = (PEER_HEADS, PEER_KEYS, ROUTE_BLOCK)
    key_iota = lax.broadcasted_iota(jnp.int32, shape, 1)
    s_scr[...] = orig_scr[p]
    rank_scr[p] = jnp.full(shape, NOT_TAKEN, jnp.int32)
    for k in range(PEER_TOPK):
        cur = s_scr[...]
        m = jnp.max(cur, axis=1, keepdims=True)
        first = jnp.min(jnp.where(cur == m, key_iota, PEER_KEYS), axis=1, keepdims=True)
        sel = key_iota == first
        s_scr[...] = jnp.where(sel, -jnp.inf, cur)
        rank_scr[p] = jnp.where(sel, k, rank_scr[p])
        _store_sorted_value(val_scr, p, k, m)


def _route_kernel(h_ref, wpq_ref, keys_ref, r2_ref, e2_ref, n1_ref, c1_ref,
                  s_scr, orig_scr, rank_scr, val_scr, n_scr, small_scr):
    tb = ROUTE_BLOCK
    q = jnp.dot(h_ref[...], wpq_ref[...], preferred_element_type=F32).astype(BF16)

    for p in range(2):
        piece = lambda h: slice((2 * h + p) * PEER_KEY_DIM, (2 * h + p + 1) * PEER_KEY_DIM)
        qp = jnp.concatenate([q[:, piece(h)] for h in range(PEER_HEADS)], axis=1)
        s = lax.dot_general(keys_ref[p], qp, NT_DIMS, preferred_element_type=F32)
        orig_scr[p] = s.reshape(PEER_HEADS, PEER_KEYS, tb)

    off = 0.0
    for p in range(2):
        off = jnp.maximum(off, _extract_topk_fast(p, orig_scr, s_scr, rank_scr, val_scr))

    @pl.when(off > 0.0)
    def _():
        for p in range(2):
            _extract_topk_ties(p, orig_scr, s_scr, rank_scr, val_scr)

    v1 = [val_scr[0, a] for a in range(PEER_TOPK)]
    v2 = [val_scr[1, b] for b in range(PEER_TOPK)]
    cells = _candidate_cells()
    sums = {c: v1[c[0]] + v2[c[1]] for c in cells}
    undecided = lambda c, d: not (d[0] >= c[0] and d[1] >= c[1])
    before = {c: float((c[0] + 1) * (c[1] + 1) - 1 + sum(undecided(c, d) for d in cells[ci + 1:]))
              for ci, c in enumerate(cells)}
    for ci, c in enumerate(cells):
        for d in cells[ci + 1:]:
            if undecided(c, d):
                first = jnp.where(sums[c] >= sums[d], 1.0, 0.0)
                before[d] = before[d] + first
                before[c] = before[c] - first
    e1 = [jnp.exp(v1[a] - v1[0]) for a in range(PEER_TOPK)]
    e2 = [jnp.exp(v2[b] - v2[0]) for b in range(PEER_TOPK)]
    zsum = 0.0
    count = [0.0] * PEER_TOPK
    for c in cells:
        chosen = before[c] < float(PEER_TOPK)
        count[c[0]] = count[c[0]] + jnp.where(chosen, 1.0, 0.0)
        zsum = zsum + jnp.where(chosen, e1[c[0]] * e2[c[1]], 0.0)
    small = jnp.zeros((PEER_HEADS, tb), jnp.int32)
    for a in range(PEER_TOPK):
        if a < SMALL_COUNT_FROM:
            n_scr[a] = count[a]
        else:
            small = small | (count[a].astype(jnp.int32) << (COUNT_BITS * a))
    small_scr[...] = small
    n_scr[PEER_TOPK] = 0.5 / zsum

    for h in range(PEER_HEADS):
        row = pl.ds(h, 1)
        rank1 = rank_scr[0, h]
        taken1 = rank1 < PEER_TOPK
        r = jnp.minimum(rank1, PEER_TOPK - 1)
        packed = jnp.broadcast_to(small_scr[row, :], r.shape)
        n1 = (lax.shift_right_logical(packed, r << COUNT_SHIFT) & ((1 << COUNT_BITS) - 1)).astype(F32)
        for a in reversed(range(SMALL_COUNT_FROM)):
            n1 = jnp.where(r == a, n_scr[a, row, :], n1)
        n1_ref[h] = jnp.where(taken1, n1, 0.0)
        w1 = jnp.where(taken1, jnp.exp(orig_scr[0, h] - val_scr[0, 0, row, :]), 0.0)
        c1_ref[h] = w1 * n_scr[PEER_TOPK, row, :]
        rank2 = rank_scr[1, h]
        r2_ref[h] = rank2.astype(F32).astype(BF16)
        w2 = jnp.where(rank2 < PEER_TOPK, jnp.exp(orig_scr[1, h] - val_scr[1, 0, row, :]), 0.0)
        e2_ref[h] = w2.astype(BF16)


def _route(h2, wpq, keys, layer):
    tb = ROUTE_BLOCK
    out = lambda dt: jax.ShapeDtypeStruct((PEER_HEADS, PEER_KEYS, T_ALL), dt)
    ospec = pl.BlockSpec((PEER_HEADS, PEER_KEYS, tb), lambda i: (0, 0, i))
    hk = PEER_HEADS * PEER_KEYS
    return pl.pallas_call(
        _route_kernel,
        grid=(T_ALL // tb,),
        in_specs=[pl.BlockSpec((tb, D_MODEL), lambda i: (i, 0)),
                  pl.BlockSpec((None, D_MODEL, 2 * hk), lambda i: (layer, 0, 0)),
                  pl.BlockSpec((None, 2, hk, hk), lambda i: (layer, 0, 0, 0))],
        out_specs=[ospec, ospec, ospec, ospec],
        out_shape=[out(BF16), out(BF16), out(F32), out(F32)],
        scratch_shapes=[pltpu.VMEM((PEER_HEADS, PEER_KEYS, tb), F32),
                        pltpu.VMEM((2, PEER_HEADS, PEER_KEYS, tb), F32),
                        pltpu.VMEM((2, PEER_HEADS, PEER_KEYS, tb), jnp.int32),
                        pltpu.VMEM((2, PEER_TOPK, PEER_HEADS, tb), F32),
                        pltpu.VMEM((PEER_TOPK + 1, PEER_HEADS, tb), F32),
                        pltpu.VMEM((PEER_HEADS, tb), jnp.int32)],
        compiler_params=_params("arbitrary"),
        name="peer_route",
    )(h2, wpq, keys)


SQRT_HALF = math.sqrt(0.5)


EXPERT_TOKEN_CHUNK = 1024
N_EXPERT_CHUNKS = TOKEN_BLOCK // EXPERT_TOKEN_CHUNK
ACT_SLOTS = min(2, N_EXPERT_CHUNKS)


def _experts_kernel(xt_ref, u_ref, vt_ref, r2_ref, e2_ref, n1_ref, c1_ref, res_ref, m_ref, g_ref, mn_ref,
                    o1_ref, o2_ref, acc_scr, act_scr, p_scr, *, last_layer):
    eb = pl.program_id(1)

    @pl.when(eb == 0)
    def _():
        acc_scr[...] = jnp.zeros_like(acc_scr)

    cols = lambda j: slice(j * EXPERT_TOKEN_CHUNK, (j + 1) * EXPERT_TOKEN_CHUNK)
    def pre_activate(j):
        act_scr[j % ACT_SLOTS] = jnp.dot(u_ref[...], xt_ref[:, cols(j)], preferred_element_type=F32)

    def activate(j):
        for c in range(I1_PER_BLOCK):
            i1 = pl.ds(eb * I1_PER_BLOCK + c, 1)
            tiles = (PEER_KEYS // BF16_TILE_ROWS, BF16_TILE_ROWS, EXPERT_TOKEN_CHUNK)
            tile_row = lambda ref, h: jnp.broadcast_to(ref[h, i1, cols(j)], tiles[1:]).astype(BF16)[None]
            gate = jnp.zeros(tiles, BF16)
            for h in range(PEER_HEADS):
                taken = r2_ref[h, :, cols(j)].reshape(tiles) < tile_row(n1_ref, h)
                e2 = e2_ref[h, :, cols(j)].reshape(tiles)
                gate = gate + jnp.where(taken, e2, jnp.zeros((), BF16)) * tile_row(c1_ref, h)
            keys = slice(c * PEER_KEYS, (c + 1) * PEER_KEYS)
            a = act_scr[j % ACT_SLOTS, keys, :]
            erf1 = 1.0 + lax.erf(a * SQRT_HALF)
            p_scr[keys, cols(j)] = (a.astype(BF16) * erf1.astype(BF16)
                                    * gate.reshape(PEER_KEYS, EXPERT_TOKEN_CHUNK))

    def mix(j):
        acc_scr[:, cols(j)] += jnp.dot(vt_ref[...], p_scr[:, cols(j)], preferred_element_type=F32)

    for stage in range(N_EXPERT_CHUNKS + 2):
        if stage < N_EXPERT_CHUNKS:
            pre_activate(stage)
        if 1 <= stage <= N_EXPERT_CHUNKS:
            activate(stage - 1)
        if stage >= 2:
            mix(stage - 2)

    @pl.when(eb == pl.num_programs(1) - 1)
    def _():
        x = res_ref[...] + m_ref[MOD_GATE2:MOD_GATE2 + 1, :] * acc_scr[...].T
        y = _rms(x, g_ref[...])
        if last_layer:
            t = pl.program_id(0)

            @pl.when(t < N_CTX_BLOCKS)
            def _():
                o1_ref[...] = y

            @pl.when(t >= N_CTX_BLOCKS)
            def _():
                o2_ref[...] = y
        else:
            o1_ref[...] = x
            h = y * (1.0 + mn_ref[MOD_SCALE1:MOD_SCALE1 + 1, :]) + mn_ref[MOD_SHIFT1:MOD_SHIFT1 + 1, :]
            o2_ref[...] = h.astype(BF16)


def _experts(h2t, u_bf, vt_bf, r2, e2, n1, c1, x_mid, mod_tok, g_next, mod_next, layer):
    once = pl.Buffered(1)
    last_layer = layer == DEPTH - 1
    tok = pl.BlockSpec((PEER_HEADS, PEER_KEYS, TOKEN_BLOCK), lambda t, e: (0, 0, t), pipeline_mode=once)
    block = (TOKEN_BLOCK, D_MODEL)
    if last_layer:
        out_specs = [pl.BlockSpec(block, lambda t, e: (jnp.minimum(t, N_CTX_BLOCKS - 1), 0), pipeline_mode=once),
                     pl.BlockSpec(block, lambda t, e: (jnp.maximum(t - N_CTX_BLOCKS, 0), 0), pipeline_mode=once)]
        out_shape = [jax.ShapeDtypeStruct((T_CTX, D_MODEL), F32), jax.ShapeDtypeStruct((T_LAT, D_MODEL), F32)]
    else:
        out_specs = [pl.BlockSpec(block, lambda t, e: (t, 0))] * 2
        out_shape = [jax.ShapeDtypeStruct((T_ALL, D_MODEL), F32), jax.ShapeDtypeStruct((T_ALL, D_MODEL), BF16)]
    return pl.pallas_call(
        functools.partial(_experts_kernel, last_layer=last_layer),
        grid=(N_TOKEN_BLOCKS, PEER_EXPERTS // EXPERT_BLOCK),
        in_specs=[pl.BlockSpec((D_MODEL, TOKEN_BLOCK), lambda t, e: (0, t)),
                  pl.BlockSpec((None, EXPERT_BLOCK, D_MODEL), lambda t, e: (layer, e, 0)),
                  pl.BlockSpec((None, D_MODEL, EXPERT_BLOCK), lambda t, e: (layer, 0, e)),
                  tok, tok, tok, tok,
                  pl.BlockSpec((TOKEN_BLOCK, D_MODEL), lambda t, e: (t, 0), pipeline_mode=once),
                  pl.BlockSpec((None, N_MOD, D_MODEL), lambda t, e: (t, 0, 0)),
                  pl.BlockSpec((1, D_MODEL), lambda t, e: (0, 0)),
                  pl.BlockSpec((None, N_MOD, D_MODEL), lambda t, e: (t, 0, 0))],
        out_specs=out_specs,
        out_shape=out_shape,
        scratch_shapes=[pltpu.VMEM((D_MODEL, TOKEN_BLOCK), F32),
                        pltpu.VMEM((ACT_SLOTS, EXPERT_BLOCK, EXPERT_TOKEN_CHUNK), F32),
                        pltpu.VMEM((EXPERT_BLOCK, TOKEN_BLOCK), BF16)],
        compiler_params=_params("arbitrary", "arbitrary"),
        name="peer_experts",
    )(h2t, u_bf, vt_bf, r2, e2, n1, c1, x_mid, mod_tok, g_next.reshape(1, D_MODEL), mod_next)


TRANSPOSE_ROWS = 2048


def _transpose_cast_kernel(v_ref, o_ref):
    o_ref[...] = v_ref[...].T.astype(BF16)


def _transpose_cast(peer_v):
    return pl.pallas_call(
        _transpose_cast_kernel,
        grid=(DEPTH, PEER_EXPERTS // TRANSPOSE_ROWS),
        in_specs=[pl.BlockSpec((None, TRANSPOSE_ROWS, D_MODEL), lambda l, e: (l, e, 0))],
        out_specs=pl.BlockSpec((None, D_MODEL, TRANSPOSE_ROWS), lambda l, e: (l, 0, e)),
        out_shape=jax.ShapeDtypeStruct((DEPTH, D_MODEL, PEER_EXPERTS), BF16),
        compiler_params=_params("arbitrary", "arbitrary"),
        name="expert_table_transpose",
    )(peer_v)


def _block_diag_keys(sub_keys):
    eye = jnp.eye(PEER_HEADS, dtype=sub_keys.dtype)
    keys = jnp.einsum('lhpjd,hg->lphjgd', sub_keys, eye)
    return keys.reshape(DEPTH, 2, PEER_HEADS * PEER_KEYS, PEER_HEADS * PEER_KEY_DIM).astype(BF16)


def kernel(x_prompt, x_sample, cache_k, cache_v, c, c_ctx, w_in, conv_w, w_conv_out, rpb, w_attn_out,
           w_four_out, w_o, g_mix, g_ffn, w_mod, b_mod, w_pq, sub_keys, peer_u, peer_v, g_final):
    assert SEQ & (SEQ - 1) == 0 and DEC_SEQ & (DEC_SEQ - 1) == 0
    x = jnp.concatenate([x_prompt.reshape(T_CTX, D_MODEL), x_sample.reshape(T_LAT, D_MODEL)], axis=0)
    cvec = jnp.zeros((COND_ROWS, D_MODEL), F32).at[0].set(c_ctx).at[1:1 + DEC_BATCH].set(c)
    mod = _modulation(cvec, w_mod, b_mod)
    lat_per_block = DEC_SEQ // TOKEN_BLOCK
    block_row = np.array([0] * N_CTX_BLOCKS
                         + [1 + b for b in range(DEC_BATCH) for _ in range(lat_per_block)])
    mod_tok = mod[:, block_row].reshape(DEPTH, N_TOKEN_BLOCKS, N_MOD, D_MODEL)
    cache_k4 = cache_k.reshape(DEC_BATCH, DEPTH, PAST_LEN, D_MODEL)
    cache_v4 = cache_v.reshape(DEC_BATCH, DEPTH, PAST_LEN, D_MODEL)

    na_bias = _na_bias(rpb)
    keys = _block_diag_keys(sub_keys)
    w_pq_bf = w_pq.astype(BF16)
    u_bf = peer_u.astype(BF16)
    vt_bf = _transpose_cast(peer_v)

    new_k = new_v = None
    h = _norm_mod(x, g_mix[0], mod_tok[0])
    for l in range(DEPTH):
        z, new_k, new_v = _in_proj(h, w_in, l, new_k, new_v)
        uconv = _short_conv(z, conv_w[l])
        att = _ctx_attention(z)
        att = _na_attention(z, cache_k4, cache_v4, na_bias, att, l)
        four = _fourier(z, SEQ, BATCH, 0, None)
        four = _fourier(z, DEC_SEQ, DEC_BATCH, T_CTX, four)
        x_mid, h2, h2t = _merge(uconv, att, four, z, x,
                           w_conv_out[l].astype(BF16), w_attn_out[l].astype(BF16),
                           w_four_out[l].astype(BF16), w_o[l].astype(BF16), mod_tok[l], g_ffn[l])
        r2, e2, n1, c1 = _route(h2, w_pq_bf, keys, l)
        if l + 1 < DEPTH:
            x, h = _experts(h2t, u_bf, vt_bf, r2, e2, n1, c1, x_mid, mod_tok[l], g_mix[l + 1], mod_tok[l + 1], l)
        else:
            y_ctx, y_lat = _experts(h2t, u_bf, vt_bf, r2, e2, n1, c1, x_mid, mod_tok[l], g_final, mod_tok[l], l)

    y_prompt = y_ctx.reshape(BATCH, SEQ, D_MODEL)
    y_sample = y_lat.reshape(DEC_BATCH, DEC_SEQ, D_MODEL)
    return (y_prompt, y_sample, new_k, new_v)
```

```python
import functools
import math

import numpy as np
import jax
import jax.numpy as jnp
from jax import lax
from jax.experimental import pallas as pl
from jax.experimental.pallas import tpu as pltpu

D_MODEL = 1024
BATCH = 16
SEQ = 256
DEPTH = 2
DEC_BATCH = 2
DEC_SEQ = 1024
PAST_LEN = 512
GRID_W = 64
CONV_K = 3
NA_HEADS = 8
NA_HEAD_DIM = D_MODEL // NA_HEADS
NA_MAX_ROWS = 8
NA_COLS = 16
FNET_GROUPS = 4
FNET_GROUP_DIM = D_MODEL // FNET_GROUPS
N_BRANCH = 3
IN_COLS = 10 * D_MODEL
PEER_HEADS = 8
PEER_KEYS = 128
PEER_EXPERTS = PEER_KEYS * PEER_KEYS
PEER_KEY_DIM = 128
PEER_TOPK = 16
N_MOD = 6
RMS_EPS = 1e-6
NEG_INF = -1e30

T_CTX = BATCH * SEQ
T_LAT = DEC_BATCH * DEC_SEQ
T_ALL = T_CTX + T_LAT

COL_CB, COL_CC, COL_CX, COL_Q, COL_K, COL_V, COL_F, COL_G0, COL_G1, COL_G2 = range(10)
MOD_SHIFT1, MOD_SCALE1, MOD_GATE1, MOD_SHIFT2, MOD_SCALE2, MOD_GATE2 = range(6)

TOKEN_BLOCK = 1024
N_TOKEN_BLOCKS = T_ALL // TOKEN_BLOCK
N_CTX_BLOCKS = T_CTX // TOKEN_BLOCK
MERGE_BLOCK = 512
ROUTE_BLOCK = 256
EXPERT_BLOCK = 1024
I1_PER_BLOCK = EXPERT_BLOCK // PEER_KEYS
VMEM_LIMIT = 56 * 1024 * 1024
BF16_TILE_ROWS = 16
F32_TILE_ROWS = 8
COND_ROWS = F32_TILE_ROWS
assert 1 + DEC_BATCH <= COND_ROWS

F32 = jnp.float32
BF16 = jnp.bfloat16
NT_DIMS = (((1,), (1,)), ((), ()))


def _params(*semantics):
    return pltpu.CompilerParams(dimension_semantics=semantics, vmem_limit_bytes=VMEM_LIMIT)


def _mod_kernel(c_ref, w_ref, b_ref, o_ref):
    c = c_ref[...]
    s = c * jax.nn.sigmoid(c)
    o_ref[0] = jnp.dot(s, w_ref[0], preferred_element_type=F32,
                       precision=lax.Precision.HIGHEST) + b_ref[0]


MOD_COLS_PER_STEP = 3 * D_MODEL // 2


def _modulation(cvec, w_mod, b_mod):
    tn = MOD_COLS_PER_STEP
    ncol = N_MOD * D_MODEL
    return pl.pallas_call(
        _mod_kernel,
        grid=(DEPTH, ncol // tn),
        in_specs=[pl.BlockSpec((COND_ROWS, D_MODEL), lambda l, n: (0, 0)),
                  pl.BlockSpec((1, D_MODEL, tn), lambda l, n: (l, 0, n)),
                  pl.BlockSpec((1, 1, tn), lambda l, n: (l, 0, n))],
        out_specs=pl.BlockSpec((1, COND_ROWS, tn), lambda l, n: (l, 0, n)),
        out_shape=jax.ShapeDtypeStruct((DEPTH, COND_ROWS, ncol), F32),
        compiler_params=_params("arbitrary", "arbitrary"),
        name="adaln_table",
    )(cvec, w_mod, b_mod.reshape(DEPTH, 1, ncol))


def _rms(x, g):
    return x * lax.rsqrt(jnp.mean(x * x, axis=-1, keepdims=True) + RMS_EPS) * g


def _norm_mod_kernel(x_ref, g_ref, m_ref, o_ref):
    y = _rms(x_ref[...], g_ref[...])
    h = y * (1.0 + m_ref[MOD_SCALE1:MOD_SCALE1 + 1, :]) + m_ref[MOD_SHIFT1:MOD_SHIFT1 + 1, :]
    o_ref[...] = h.astype(BF16)


def _norm_mod(x, g, mod_tok):
    return pl.pallas_call(
        _norm_mod_kernel,
        grid=(N_TOKEN_BLOCKS,),
        in_specs=[pl.BlockSpec((TOKEN_BLOCK, D_MODEL), lambda i: (i, 0)),
                  pl.BlockSpec((1, D_MODEL), lambda i: (0, 0)),
                  pl.BlockSpec((None, N_MOD, D_MODEL), lambda i: (i, 0, 0))],
        out_specs=pl.BlockSpec((TOKEN_BLOCK, D_MODEL), lambda i: (i, 0)),
        out_shape=jax.ShapeDtypeStruct((T_ALL, D_MODEL), BF16),
        compiler_params=_params("arbitrary"),
        name="norm_modulate",
    )(x, g.reshape(1, D_MODEL), mod_tok)


def _in_proj_kernel(h_ref, w_ref, *rest):
    z_ref, k_ref, v_ref, wb_scr = rest[-4:]
    n = pl.program_id(0)
    i = pl.program_id(1)

    @pl.when(i == 0)
    def _():
        wb_scr[...] = w_ref[...].astype(BF16)

    z = jnp.dot(h_ref[...], wb_scr[...], preferred_element_type=F32)
    z_ref[...] = z.astype(BF16)
    per_block = TOKEN_BLOCK // SEQ

    @pl.when((n == COL_K) & (i < N_CTX_BLOCKS))
    def _():
        k_ref[...] = z.reshape(per_block, SEQ, NA_HEADS, NA_HEAD_DIM)

    @pl.when((n == COL_V) & (i < N_CTX_BLOCKS))
    def _():
        v_ref[...] = z.reshape(per_block, SEQ, NA_HEADS, NA_HEAD_DIM)


def _cache_block_index(col, layer, n, i):
    last = N_CTX_BLOCKS - 1
    blk = jnp.where(n < col, 0, jnp.where(n > col, last, jnp.minimum(i, last)))
    return (blk, layer, 0, 0, 0)


def _in_proj(h, w_in, layer, cache_k, cache_v):
    per_block = TOKEN_BLOCK // SEQ
    cache_shape = jax.ShapeDtypeStruct((BATCH, DEPTH, SEQ, NA_HEADS, NA_HEAD_DIM), F32)
    cache_spec = lambda col: pl.BlockSpec(
        (per_block, None, SEQ, NA_HEADS, NA_HEAD_DIM), functools.partial(_cache_block_index, col, layer))
    in_specs = [pl.BlockSpec((TOKEN_BLOCK, D_MODEL), lambda n, i: (i, 0)),
                pl.BlockSpec((None, D_MODEL, D_MODEL), lambda n, i: (layer, 0, n))]
    args = [h, w_in]
    aliases = {}
    if cache_k is not None:
        in_specs += [pl.BlockSpec(memory_space=pl.ANY)] * 2
        args += [cache_k, cache_v]
        aliases = {2: 1, 3: 2}
    return pl.pallas_call(
        _in_proj_kernel,
        grid=(IN_COLS // D_MODEL, N_TOKEN_BLOCKS),
        in_specs=in_specs,
        out_specs=[pl.BlockSpec((TOKEN_BLOCK, D_MODEL), lambda n, i: (i, n)),
                   cache_spec(COL_K), cache_spec(COL_V)],
        out_shape=[jax.ShapeDtypeStruct((T_ALL, IN_COLS), BF16), cache_shape, cache_shape],
        scratch_shapes=[pltpu.VMEM((D_MODEL, D_MODEL), BF16)],
        input_output_aliases=aliases,
        compiler_params=_params("arbitrary", "arbitrary"),
        name="in_proj",
    )(*args)


def _conv_kernel(cb_ref, cc_ref, cx_ref, w_ref, o_ref):
    i = pl.program_id(0)
    u = cc_ref[...].astype(F32) * cx_ref[...].astype(F32)
    seq = jnp.where(i < N_CTX_BLOCKS, SEQ, DEC_SEQ)
    pos = lax.broadcasted_iota(jnp.int32, (TOKEN_BLOCK, 1), 0) & (seq - 1)
    prev = jnp.where(pos == 0, 0.0, pltpu.roll(u, 1, 0))
    nxt = jnp.where(pos == seq - 1, 0.0, pltpu.roll(u, TOKEN_BLOCK - 1, 0))
    y = w_ref[0:1, :] * prev + w_ref[1:2, :] * u + w_ref[2:3, :] * nxt
    o_ref[...] = (cb_ref[...].astype(F32) * y).astype(BF16)


def _short_conv(z, conv_w):
    col = lambda c: pl.BlockSpec((TOKEN_BLOCK, D_MODEL), lambda i: (i, c))
    return pl.pallas_call(
        _conv_kernel,
        grid=(N_TOKEN_BLOCKS,),
        in_specs=[col(COL_CB), col(COL_CC), col(COL_CX),
                  pl.BlockSpec((CONV_K, D_MODEL), lambda i: (0, 0))],
        out_specs=pl.BlockSpec((TOKEN_BLOCK, D_MODEL), lambda i: (i, 0)),
        out_shape=jax.ShapeDtypeStruct((T_ALL, D_MODEL), BF16),
        compiler_params=_params("arbitrary"),
        name="short_conv",
    )(z, z, z, conv_w)


ATT_SCALE = NA_HEAD_DIM ** -0.5


CTX_SEQS_PER_STEP = 2


def _ctx_attn_kernel(q_ref, k_ref, v_ref, o_ref):
    for b in range(CTX_SEQS_PER_STEP):
        rows = slice(b * SEQ, (b + 1) * SEQ)
        for h in range(NA_HEADS):
            sl = slice(h * NA_HEAD_DIM, (h + 1) * NA_HEAD_DIM)
            s = lax.dot_general(q_ref[rows, sl], k_ref[rows, sl], NT_DIMS,
                                preferred_element_type=F32) * ATT_SCALE
            p = jnp.exp(s - jnp.max(s, axis=-1, keepdims=True))
            o = jnp.dot(p.astype(BF16), v_ref[rows, sl], preferred_element_type=F32)
            o_ref[rows, sl] = (o / jnp.sum(p, axis=-1, keepdims=True)).astype(BF16)


def _ctx_attention(z):
    block = (CTX_SEQS_PER_STEP * SEQ, D_MODEL)
    col = lambda c: pl.BlockSpec(block, lambda b: (b, c))
    return pl.pallas_call(
        _ctx_attn_kernel,
        grid=(BATCH // CTX_SEQS_PER_STEP,),
        in_specs=[col(COL_Q), col(COL_K), col(COL_V)],
        out_specs=pl.BlockSpec(block, lambda b: (b, 0)),
        out_shape=jax.ShapeDtypeStruct((T_ALL, D_MODEL), BF16),
        compiler_params=_params("arbitrary"),
        name="ctx_attention",
    )(z, z, z)


NA_GRID_ROWS = DEC_SEQ // GRID_W
NA_WIN_ROWS = min(NA_MAX_ROWS, NA_GRID_ROWS)


def _na_row_groups():
    groups = []
    for r in range(NA_GRID_ROWS):
        r0 = min(max(r - NA_WIN_ROWS // 2, 0), NA_GRID_ROWS - NA_WIN_ROWS)
        if groups and groups[-1][0] == r0:
            groups[-1][1].append(r)
        else:
            groups.append((r0, [r]))
    return groups


def _na_kernel(q_ref, k_ref, v_ref, ck_ref, cv_ref, b_ref, att_in_ref, o_ref):
    del att_in_ref
    pairs = NA_WIN_ROWS // 2
    ck = ck_ref[...].astype(BF16)
    cv = cv_ref[...].astype(BF16)
    for r0, q_rows in _na_row_groups():
        rows = slice(q_rows[0] * GRID_W, (q_rows[-1] + 1) * GRID_W)
        win = slice(r0 * GRID_W, (r0 + NA_WIN_ROWS) * GRID_W)
        q = q_ref[rows, :]
        slab = lambda d: jnp.concatenate([b_ref[d + 2 * m] for m in range(pairs)], axis=1)
        bias = jnp.concatenate([slab(r0 - r + NA_MAX_ROWS - 1) for r in q_rows], axis=0)
        s_loc = lax.dot_general(q, k_ref[win, :], NT_DIMS, preferred_element_type=F32) * ATT_SCALE + bias
        s_ctx = lax.dot_general(q, ck, NT_DIMS, preferred_element_type=F32) * ATT_SCALE
        m = jnp.maximum(jnp.max(s_loc, axis=-1, keepdims=True), jnp.max(s_ctx, axis=-1, keepdims=True))
        p_loc = jnp.exp(s_loc - m)
        p_ctx = jnp.exp(s_ctx - m)
        den = jnp.sum(p_loc, axis=-1, keepdims=True) + jnp.sum(p_ctx, axis=-1, keepdims=True)
        o = (jnp.dot(p_loc.astype(BF16), v_ref[win, :], preferred_element_type=F32)
             + jnp.dot(p_ctx.astype(BF16), cv, preferred_element_type=F32))
        o_ref[rows, :] = (o / den).astype(BF16)


NA_PAIR_TABLES = 2 * NA_MAX_ROWS - 2


def _na_bias(rpb):
    assert NA_WIN_ROWS % 2 == 0
    c = np.arange(GRID_W)
    c0 = np.clip(c - NA_COLS // 2, 0, GRID_W - NA_COLS)
    in_cols = (c[None, :] >= c0[:, None]) & (c[None, :] < c0[:, None] + NA_COLS)
    dc = c[None, :] - c[:, None] + (NA_COLS - 1)
    pick = (dc[None] == np.arange(2 * NA_COLS - 1)[:, None, None]) & in_cols[None]
    t = jnp.einsum('lhdj,jqk->lhdqk', rpb.astype(F32), jnp.asarray(pick, F32), precision=lax.Precision.HIGHEST)
    t = jnp.where(in_cols, t, NEG_INF)
    return jnp.concatenate([t[:, :, :-1], t[:, :, 1:]], axis=-1)


def _na_attention(z, cache_k, cache_v, bias, att, layer):
    lat0 = T_CTX // DEC_SEQ
    col = lambda c: pl.BlockSpec((DEC_SEQ, NA_HEAD_DIM), lambda h, b: (lat0 + b, c * NA_HEADS + h))
    cache = pl.BlockSpec((None, None, PAST_LEN, NA_HEAD_DIM), lambda h, b: (b, layer, 0, h))
    return pl.pallas_call(
        _na_kernel,
        grid=(NA_HEADS, DEC_BATCH),
        in_specs=[col(COL_Q), col(COL_K), col(COL_V), cache, cache,
                  pl.BlockSpec((None, None, NA_PAIR_TABLES, GRID_W, 2 * GRID_W), lambda h, b: (layer, h, 0, 0, 0)),
                  pl.BlockSpec(memory_space=pl.ANY)],
        out_specs=pl.BlockSpec((DEC_SEQ, NA_HEAD_DIM), lambda h, b: (lat0 + b, h)),
        out_shape=jax.ShapeDtypeStruct((T_ALL, D_MODEL), BF16),
        input_output_aliases={6: 0},
        compiler_params=_params("arbitrary", "arbitrary"),
        name="na_attention",
    )(z, z, z, cache_k, cache_v, bias, att)


def _dft_matrices(n):
    j = np.arange(n)
    ang = 2.0 * np.pi * ((j[:, None] * j[None, :]) % n) / n
    return np.cos(ang) / math.sqrt(n), np.sin(ang) / math.sqrt(n)


def _fourier_kernel(cs_ref, ss_ref, f_ref, w2_ref, *rest):
    o_ref = rest[-1]
    f = f_ref[...]
    cu = jnp.dot(cs_ref[...], f, preferred_element_type=F32).astype(BF16)
    su = jnp.dot(ss_ref[...], f, preferred_element_type=F32).astype(BF16)
    for g in range(FNET_GROUPS):
        sl = slice(g * FNET_GROUP_DIM, (g + 1) * FNET_GROUP_DIM)
        lhs = jnp.concatenate([cu[:, sl], su[:, sl]], axis=1)
        o_ref[:, sl] = jnp.dot(lhs, w2_ref[...], preferred_element_type=F32).astype(BF16)


FOURIER_ROWS = 256


def _fourier(z, seq, n_batch, first_row, prev):
    cs, ss = _dft_matrices(seq)
    cc, sc = _dft_matrices(FNET_GROUP_DIM)
    w2 = jnp.asarray(np.concatenate([cc, -sc], axis=0), F32).astype(BF16)
    rows = FOURIER_ROWS
    nr = seq // rows
    in_specs = [pl.BlockSpec((rows, seq), lambda b, r: (r, 0)),
                pl.BlockSpec((rows, seq), lambda b, r: (r, 0)),
                pl.BlockSpec((seq, D_MODEL), lambda b, r: (first_row // seq + b, COL_F)),
                pl.BlockSpec((2 * FNET_GROUP_DIM, FNET_GROUP_DIM), lambda b, r: (0, 0))]
    args = [jnp.asarray(cs, F32).astype(BF16), jnp.asarray(ss, F32).astype(BF16), z, w2]
    aliases = {}
    if prev is not None:
        in_specs.append(pl.BlockSpec(memory_space=pl.ANY))
        args.append(prev)
        aliases = {4: 0}
    return pl.pallas_call(
        _fourier_kernel,
        grid=(n_batch, nr),
        in_specs=in_specs,
        out_specs=pl.BlockSpec((rows, D_MODEL), lambda b, r: (first_row // rows + b * nr + r, 0)),
        out_shape=jax.ShapeDtypeStruct((T_ALL, D_MODEL), BF16),
        input_output_aliases=aliases,
        compiler_params=_params("arbitrary", "arbitrary"),
        name="fourier_mix",
    )(*args)


def _merge_kernel(uc_ref, at_ref, fo_ref, g0_ref, g1_ref, g2_ref, x_ref, wc_ref, wa_ref, wf_ref,
                  wo_ref, m_ref, gf_ref, xo_ref, h2_ref, h2t_ref):
    yc = jnp.dot(uc_ref[...], wc_ref[...], preferred_element_type=F32)
    ya = jnp.dot(at_ref[...], wa_ref[...], preferred_element_type=F32)
    yf = jnp.dot(fo_ref[...], wf_ref[...], preferred_element_type=F32)
    gate = lambda r: jax.nn.sigmoid(r[...].astype(F32))
    merged = gate(g0_ref) * yc + gate(g1_ref) * ya + gate(g2_ref) * yf
    y = jnp.dot(merged.astype(BF16), wo_ref[...], preferred_element_type=F32)
    x = x_ref[...] + m_ref[MOD_GATE1:MOD_GATE1 + 1, :] * y
    xo_ref[...] = x
    hn = _rms(x, gf_ref[...])
    h2 = hn * (1.0 + m_ref[MOD_SCALE2:MOD_SCALE2 + 1, :]) + m_ref[MOD_SHIFT2:MOD_SHIFT2 + 1, :]
    h2_ref[...] = h2.astype(BF16)
    h2t_ref[...] = h2.T.astype(BF16)


def _merge(uconv, att, four, z, x, wc, wa, wf, wo, mod_tok, g_ffn):
    per = TOKEN_BLOCK // MERGE_BLOCK
    row = pl.BlockSpec((MERGE_BLOCK, D_MODEL), lambda i: (i, 0))
    col = lambda c: pl.BlockSpec((MERGE_BLOCK, D_MODEL), lambda i: (i, c))
    wspec = pl.BlockSpec((D_MODEL, D_MODEL), lambda i: (0, 0))
    return pl.pallas_call(
        _merge_kernel,
        grid=(T_ALL // MERGE_BLOCK,),
        in_specs=[row, row, row, col(COL_G0), col(COL_G1), col(COL_G2), row,
                  wspec, wspec, wspec, wspec,
                  pl.BlockSpec((None, N_MOD, D_MODEL), lambda i: (i // per, 0, 0)),
                  pl.BlockSpec((1, D_MODEL), lambda i: (0, 0))],
        out_specs=[row, row, pl.BlockSpec((D_MODEL, MERGE_BLOCK), lambda i: (0, i))],
        out_shape=[jax.ShapeDtypeStruct((T_ALL, D_MODEL), F32),
                   jax.ShapeDtypeStruct((T_ALL, D_MODEL), BF16),
                   jax.ShapeDtypeStruct((D_MODEL, T_ALL), BF16)],
        compiler_params=_params("arbitrary"),
        name="branch_merge",
    )(uconv, att, four, z, z, z, x, wc, wa, wf, wo, mod_tok, g_ffn.reshape(1, D_MODEL))


def _candidate_cells():
    return [(a, b) for a in range(PEER_TOPK) for b in range(PEER_TOPK) if (a + 1) * (b + 1) <= PEER_TOPK]


RANK_CODE_BITS = 0xFF7F0000 - (1 << 32)
LOWEST_SCORE = -3.0e38
RANK_CODE_LIMIT = -3.2e38
NOT_TAKEN = 127
RANK_MASK = 0xFF
SMALL_COUNT_FROM = 4
COUNT_BITS = 2
COUNT_SHIFT = COUNT_BITS.bit_length() - 1
assert 1 << COUNT_SHIFT == COUNT_BITS
assert (SMALL_COUNT_FROM + 1) * (1 << COUNT_BITS) > PEER_TOPK and COUNT_BITS * PEER_TOPK <= 32


def _rank_code(k):
    return float(np.array(RANK_CODE_BITS | k, np.int32).view(np.float32))


def _store_sorted_value(val_scr, p, k, m):
    for h in range(PEER_HEADS):
        val_scr[p, k, h:h + 1, :] = m[h]


def _extract_topk_fast(p, orig_scr, s_scr, rank_scr, val_scr):
    s_scr[...] = jnp.maximum(orig_scr[p], LOWEST_SCORE)
    for k in range(PEER_TOPK):
        cur = s_scr[...]
        m = jnp.max(cur, axis=1, keepdims=True)
        s_scr[...] = jnp.where(cur == m, _rank_code(k), cur)
        _store_sorted_value(val_scr, p, k, m)
    coded = s_scr[...]
    taken = coded < RANK_CODE_LIMIT
    rank_scr[p] = jnp.where(taken, pltpu.bitcast(coded, jnp.int32) & RANK_MASK, NOT_TAKEN)
    count = jnp.sum(jnp.where(taken, 1.0, 0.0), axis=1)
    clamped = jnp.where(val_scr[p, PEER_TOPK - 1] <= LOWEST_SCORE, 1.0, 0.0)
    return jnp.max(jnp.abs(count - float(PEER_TOPK)) + clamped)


def _extract_topk_ties(p, orig_scr, s_scr, rank_scr, val_scr):
    shape = (PEER_HEADS, PEER_KEYS, ROUTE_BLOCK)
    key_iota = lax.broadcasted_iota(jnp.int32, shape, 1)
    s_scr[...] = orig_scr[p]
    rank_scr[p] = jnp.full(shape, NOT_TAKEN, jnp.int32)
    for k in range(PEER_TOPK):
        cur = s_scr[...]
        m = jnp.max(cur, axis=1, keepdims=True)
        first = jnp.min(jnp.where(cur == m, key_iota, PEER_KEYS), axis=1, keepdims=True)
        sel = key_iota == first
        s_scr[...] = jnp.where(sel, -jnp.inf, cur)
        rank_scr[p] = jnp.where(sel, k, rank_scr[p])
        _store_sorted_value(val_scr, p, k, m)


def _route_kernel(h_ref, wpq_ref, keys_ref, r2_ref, e2_ref, n1_ref, c1_ref,
                  s_scr, orig_scr, rank_scr, val_scr, n_scr, small_scr):
    tb = ROUTE_BLOCK
    q = jnp.dot(h_ref[...], wpq_ref[...], preferred_element_type=F32).astype(BF16)

    for p in range(2):
        piece = lambda h: slice((2 * h + p) * PEER_KEY_DIM, (2 * h + p + 1) * PEER_KEY_DIM)
        qp = jnp.concatenate([q[:, piece(h)] for h in range(PEER_HEADS)], axis=1)
        s = lax.dot_general(keys_ref[p], qp, NT_DIMS, preferred_element_type=F32)
        orig_scr[p] = s.reshape(PEER_HEADS, PEER_KEYS, tb)

    off = 0.0
    for p in range(2):
        off = jnp.maximum(off, _extract_topk_fast(p, orig_scr, s_scr, rank_scr, val_scr))

    @pl.when(off > 0.0)
    def _():
        for p in range(2):
            _extract_topk_ties(p, orig_scr, s_scr, rank_scr, val_scr)

    v1 = [val_scr[0, a] for a in range(PEER_TOPK)]
    v2 = [val_scr[1, b] for b in range(PEER_TOPK)]
    cells = _candidate_cells()
    sums = {c: v1[c[0]] + v2[c[1]] for c in cells}
    undecided = lambda c, d: not (d[0] >= c[0] and d[1] >= c[1])
    before = {c: float((c[0] + 1) * (c[1] + 1) - 1 + sum(undecided(c, d) for d in cells[ci + 1:]))
              for ci, c in enumerate(cells)}
    for ci, c in enumerate(cells):
        for d in cells[ci + 1:]:
            if undecided(c, d):
                first = jnp.where(sums[c] >= sums[d], 1.0, 0.0)
                before[d] = before[d] + first
                before[c] = before[c] - first
    e1 = [jnp.exp(v1[a] - v1[0]) for a in range(PEER_TOPK)]
    e2 = [jnp.exp(v2[b] - v2[0]) for b in range(PEER_TOPK)]
    zsum = 0.0
    count = [0.0] * PEER_TOPK
    for c in cells:
        chosen = before[c] < float(PEER_TOPK)
        count[c[0]] = count[c[0]] + jnp.where(chosen, 1.0, 0.0)
        zsum = zsum + jnp.where(chosen, e1[c[0]] * e2[c[1]], 0.0)
    small = jnp.zeros((PEER_HEADS, tb), jnp.int32)
    for a in range(PEER_TOPK):
        if a < SMALL_COUNT_FROM:
            n_scr[a] = count[a]
        else:
            small = small | (count[a].astype(jnp.int32) << (COUNT_BITS * a))
    small_scr[...] = small
    n_scr[PEER_TOPK] = 0.5 / zsum

    for h in range(PEER_HEADS):
        row = pl.ds(h, 1)
        rank1 = rank_scr[0, h]
        taken1 = rank1 < PEER_TOPK
        r = jnp.minimum(rank1, PEER_TOPK - 1)
        packed = jnp.broadcast_to(small_scr[row, :], r.shape)
        n1 = (lax.shift_right_logical(packed, r << COUNT_SHIFT) & ((1 << COUNT_BITS) - 1)).astype(F32)
        for a in reversed(range(SMALL_COUNT_FROM)):
            n1 = jnp.where(r == a, n_scr[a, row, :], n1)
        n1_ref[h] = jnp.where(taken1, n1, 0.0)
        w1 = jnp.where(taken1, jnp.exp(orig_scr[0, h] - val_scr[0, 0, row, :]), 0.0)
        c1_ref[h] = w1 * n_scr[PEER_TOPK, row, :]
        rank2 = rank_scr[1, h]
        r2_ref[h] = rank2.astype(F32).astype(BF16)
        w2 = jnp.where(rank2 < PEER_TOPK, jnp.exp(orig_scr[1, h] - val_scr[1, 0, row, :]), 0.0)
        e2_ref[h] = w2.astype(BF16)


def _route(h2, wpq, keys, layer):
    tb = ROUTE_BLOCK
    out = lambda dt: jax.ShapeDtypeStruct((PEER_HEADS, PEER_KEYS, T_ALL), dt)
    ospec = pl.BlockSpec((PEER_HEADS, PEER_KEYS, tb), lambda i: (0, 0, i))
    hk = PEER_HEADS * PEER_KEYS
    return pl.pallas_call(
        _route_kernel,
        grid=(T_ALL // tb,),
        in_specs=[pl.BlockSpec((tb, D_MODEL), lambda i: (i, 0)),
                  pl.BlockSpec((None, D_MODEL, 2 * hk), lambda i: (layer, 0, 0)),
                  pl.BlockSpec((None, 2, hk, hk), lambda i: (layer, 0, 0, 0))],
        out_specs=[ospec, ospec, ospec, ospec],
        out_shape=[out(BF16), out(BF16), out(F32), out(F32)],
        scratch_shapes=[pltpu.VMEM((PEER_HEADS, PEER_KEYS, tb), F32),
                        pltpu.VMEM((2, PEER_HEADS, PEER_KEYS, tb), F32),
                        pltpu.VMEM((2, PEER_HEADS, PEER_KEYS, tb), jnp.int32),
                        pltpu.VMEM((2, PEER_TOPK, PEER_HEADS, tb), F32),
                        pltpu.VMEM((PEER_TOPK + 1, PEER_HEADS, tb), F32),
                        pltpu.VMEM((PEER_HEADS, tb), jnp.int32)],
        compiler_params=_params("arbitrary"),
        name="peer_route",
    )(h2, wpq, keys)


SQRT_HALF = math.sqrt(0.5)


EXPERT_TOKEN_CHUNK = 1024
N_EXPERT_CHUNKS = TOKEN_BLOCK // EXPERT_TOKEN_CHUNK
ACT_SLOTS = min(2, N_EXPERT_CHUNKS)


def _experts_kernel(xt_ref, u_ref, vt_ref, r2_ref, e2_ref, n1_ref, c1_ref, res_ref, m_ref, g_ref, mn_ref,
                    o1_ref, o2_ref, acc_scr, act_scr, p_scr, *, last_layer):
    eb = pl.program_id(1)

    @pl.when(eb == 0)
    def _():
        acc_scr[...] = jnp.zeros_like(acc_scr)

    cols = lambda j: slice(j * EXPERT_TOKEN_CHUNK, (j + 1) * EXPERT_TOKEN_CHUNK)
    def pre_activate(j):
        act_scr[j % ACT_SLOTS] = jnp.dot(u_ref[...], xt_ref[:, cols(j)], preferred_element_type=F32)

    def activate(j):
        for c in range(I1_PER_BLOCK):
            i1 = pl.ds(eb * I1_PER_BLOCK + c, 1)
            tiles = (PEER_KEYS // BF16_TILE_ROWS, BF16_TILE_ROWS, EXPERT_TOKEN_CHUNK)
            tile_row = lambda ref, h: jnp.broadcast_to(ref[h, i1, cols(j)], tiles[1:]).astype(BF16)[None]
            gate = jnp.zeros(tiles, BF16)
            for h in range(PEER_HEADS):
                taken = r2_ref[h, :, cols(j)].reshape(tiles) < tile_row(n1_ref, h)
                e2 = e2_ref[h, :, cols(j)].reshape(tiles)
                gate = gate + jnp.where(taken, e2, jnp.zeros((), BF16)) * tile_row(c1_ref, h)
            keys = slice(c * PEER_KEYS, (c + 1) * PEER_KEYS)
            a = act_scr[j % ACT_SLOTS, keys, :]
            erf1 = 1.0 + lax.erf(a * SQRT_HALF)
            p_scr[keys, cols(j)] = (a.astype(BF16) * erf1.astype(BF16)
                                    * gate.reshape(PEER_KEYS, EXPERT_TOKEN_CHUNK))

    def mix(j):
        acc_scr[:, cols(j)] += jnp.dot(vt_ref[...], p_scr[:, cols(j)], preferred_element_type=F32)

    for stage in range(N_EXPERT_CHUNKS + 2):
        if stage < N_EXPERT_CHUNKS:
            pre_activate(stage)
        if 1 <= stage <= N_EXPERT_CHUNKS:
            activate(stage - 1)
        if stage >= 2:
            mix(stage - 2)

    @pl.when(eb == pl.num_programs(1) - 1)
    def _():
        x = res_ref[...] + m_ref[MOD_GATE2:MOD_GATE2 + 1, :] * acc_scr[...].T
        y = _rms(x, g_ref[...])
        if last_layer:
            t = pl.program_id(0)

            @pl.when(t < N_CTX_BLOCKS)
            def _():
                o1_ref[...] = y

            @pl.when(t >= N_CTX_BLOCKS)
            def _():
                o2_ref[...] = y
        else:
            o1_ref[...] = x
            h = y * (1.0 + mn_ref[MOD_SCALE1:MOD_SCALE1 + 1, :]) + mn_ref[MOD_SHIFT1:MOD_SHIFT1 + 1, :]
            o2_ref[...] = h.astype(BF16)


def _experts(h2t, u_bf, vt_bf, r2, e2, n1, c1, x_mid, mod_tok, g_next, mod_next, layer):
    once = pl.Buffered(1)
    last_layer = layer == DEPTH - 1
    tok = pl.BlockSpec((PEER_HEADS, PEER_KEYS, TOKEN_BLOCK), lambda t, e: (0, 0, t), pipeline_mode=once)
    block = (TOKEN_BLOCK, D_MODEL)
    if last_layer:
        out_specs = [pl.BlockSpec(block, lambda t, e: (jnp.minimum(t, N_CTX_BLOCKS - 1), 0), pipeline_mode=once),
                     pl.BlockSpec(block, lambda t, e: (jnp.maximum(t - N_CTX_BLOCKS, 0), 0), pipeline_mode=once)]
        out_shape = [jax.ShapeDtypeStruct((T_CTX, D_MODEL), F32), jax.ShapeDtypeStruct((T_LAT, D_MODEL), F32)]
    else:
        out_specs = [pl.BlockSpec(block, lambda t, e: (t, 0))] * 2
        out_shape = [jax.ShapeDtypeStruct((T_ALL, D_MODEL), F32), jax.ShapeDtypeStruct((T_ALL, D_MODEL), BF16)]
    return pl.pallas_call(
        functools.partial(_experts_kernel, last_layer=last_layer),
        grid=(N_TOKEN_BLOCKS, PEER_EXPERTS // EXPERT_BLOCK),
        in_specs=[pl.BlockSpec((D_MODEL, TOKEN_BLOCK), lambda t, e: (0, t)),
                  pl.BlockSpec((None, EXPERT_BLOCK, D_MODEL), lambda t, e: (layer, e, 0)),
                  pl.BlockSpec((None, D_MODEL, EXPERT_BLOCK), lambda t, e: (layer, 0, e)),
                  tok, tok, tok, tok,
                  pl.BlockSpec((TOKEN_BLOCK, D_MODEL), lambda t, e: (t, 0), pipeline_mode=once),
                  pl.BlockSpec((None, N_MOD, D_MODEL), lambda t, e: (t, 0, 0)),
                  pl.BlockSpec((1, D_MODEL), lambda t, e: (0, 0)),
                  pl.BlockSpec((None, N_MOD, D_MODEL), lambda t, e: (t, 0, 0))],
        out_specs=out_specs,
        out_shape=out_shape,
        scratch_shapes=[pltpu.VMEM((D_MODEL, TOKEN_BLOCK), F32),
                        pltpu.VMEM((ACT_SLOTS, EXPERT_BLOCK, EXPERT_TOKEN_CHUNK), F32),
                        pltpu.VMEM((EXPERT_BLOCK, TOKEN_BLOCK), BF16)],
        compiler_params=_params("arbitrary", "arbitrary"),
        name="peer_experts",
    )(h2t, u_bf, vt_bf, r2, e2, n1, c1, x_mid, mod_tok, g_next.reshape(1, D_MODEL), mod_next)


TRANSPOSE_ROWS = 2048


def _transpose_cast_kernel(v_ref, o_ref):
    o_ref[...] = v_ref[...].T.astype(BF16)


def _transpose_cast(peer_v):
    return pl.pallas_call(
        _transpose_cast_kernel,
        grid=(DEPTH, PEER_EXPERTS // TRANSPOSE_ROWS),
        in_specs=[pl.BlockSpec((None, TRANSPOSE_ROWS, D_MODEL), lambda l, e: (l, e, 0))],
        out_specs=pl.BlockSpec((None, D_MODEL, TRANSPOSE_ROWS), lambda l, e: (l, 0, e)),
        out_shape=jax.ShapeDtypeStruct((DEPTH, D_MODEL, PEER_EXPERTS), BF16),
        compiler_params=_params("arbitrary", "arbitrary"),
        name="expert_table_transpose",
    )(peer_v)


def _block_diag_keys(sub_keys):
    eye = jnp.eye(PEER_HEADS, dtype=sub_keys.dtype)
    keys = jnp.einsum('lhpjd,hg->lphjgd', sub_keys, eye)
    return keys.reshape(DEPTH, 2, PEER_HEADS * PEER_KEYS, PEER_HEADS * PEER_KEY_DIM).astype(BF16)


def kernel(x_prompt, x_sample, cache_k, cache_v, c, c_ctx, w_in, conv_w, w_conv_out, rpb, w_attn_out,
           w_four_out, w_o, g_mix, g_ffn, w_mod, b_mod, w_pq, sub_keys, peer_u, peer_v, g_final):
    assert SEQ & (SEQ - 1) == 0 and DEC_SEQ & (DEC_SEQ - 1) == 0
    x = jnp.concatenate([x_prompt.reshape(T_CTX, D_MODEL), x_sample.reshape(T_LAT, D_MODEL)], axis=0)
    cvec = jnp.zeros((COND_ROWS, D_MODEL), F32).at[0].set(c_ctx).at[1:1 + DEC_BATCH].set(c)
    mod = _modulation(cvec, w_mod, b_mod)
    lat_per_block = DEC_SEQ // TOKEN_BLOCK
    block_row = np.array([0] * N_CTX_BLOCKS
                         + [1 + b for b in range(DEC_BATCH) for _ in range(lat_per_block)])
    mod_tok = mod[:, block_row].reshape(DEPTH, N_TOKEN_BLOCKS, N_MOD, D_MODEL)
    cache_k4 = cache_k.reshape(DEC_BATCH, DEPTH, PAST_LEN, D_MODEL)
    cache_v4 = cache_v.reshape(DEC_BATCH, DEPTH, PAST_LEN, D_MODEL)

    na_bias = _na_bias(rpb)
    keys = _block_diag_keys(sub_keys)
    w_pq_bf = w_pq.astype(BF16)
    u_bf = peer_u.astype(BF16)
    vt_bf = _transpose_cast(peer_v)

    new_k = new_v = None
    h = _norm_mod(x, g_mix[0], mod_tok[0])
    for l in range(DEPTH):
        z, new_k, new_v = _in_proj(h, w_in, l, new_k, new_v)
        uconv = _short_conv(z, conv_w[l])
        att = _ctx_attention(z)
        att = _na_attention(z, cache_k4, cache_v4, na_bias, att, l)
        four = _fourier(z, SEQ, BATCH, 0, None)
        four = _fourier(z, DEC_SEQ, DEC_BATCH, T_CTX, four)
        x_mid, h2, h2t = _merge(uconv, att, four, z, x,
                           w_conv_out[l].astype(BF16), w_attn_out[l].astype(BF16),
                           w_four_out[l].astype(BF16), w_o[l].astype(BF16), mod_tok[l], g_ffn[l])
        r2, e2, n1, c1 = _route(h2, w_pq_bf, keys, l)
        if l + 1 < DEPTH:
            x, h = _experts(h2t, u_bf, vt_bf, r2, e2, n1, c1, x_mid, mod_tok[l], g_mix[l + 1], mod_tok[l + 1], l)
        else:
            y_ctx, y_lat = _experts(h2t, u_bf, vt_bf, r2, e2, n1, c1, x_mid, mod_tok[l], g_final, mod_tok[l], l)

    y_prompt = y_ctx.reshape(BATCH, SEQ, D_MODEL)
    y_sample = y_lat.reshape(DEC_BATCH, DEC_SEQ, D_MODEL)
    return (y_prompt, y_sample, new_k, new_v)
```

```python
import functools
import math

import numpy as np
import jax
import jax.numpy as jnp
from jax import lax
from jax.experimental import pallas as pl
from jax.experimental.pallas import tpu as pltpu

D_MODEL = 1024
BATCH = 16
SEQ = 256
DEPTH = 2
DEC_BATCH = 2
DEC_SEQ = 1024
PAST_LEN = 512
GRID_W = 64
CONV_K = 3
NA_HEADS = 8
NA_HEAD_DIM = D_MODEL // NA_HEADS
NA_MAX_ROWS = 8
NA_COLS = 16
FNET_GROUPS = 4
FNET_GROUP_DIM = D_MODEL // FNET_GROUPS
N_BRANCH = 3
IN_COLS = 10 * D_MODEL
PEER_HEADS = 8
PEER_KEYS = 128
PEER_EXPERTS = PEER_KEYS * PEER_KEYS
PEER_KEY_DIM = 128
PEER_TOPK = 16
N_MOD = 6
RMS_EPS = 1e-6
NEG_INF = -1e30

T_CTX = BATCH * SEQ
T_LAT = DEC_BATCH * DEC_SEQ
T_ALL = T_CTX + T_LAT

COL_CB, COL_CC, COL_CX, COL_Q, COL_K, COL_V, COL_F, COL_G0, COL_G1, COL_G2 = range(10)
MOD_SHIFT1, MOD_SCALE1, MOD_GATE1, MOD_SHIFT2, MOD_SCALE2, MOD_GATE2 = range(6)

TOKEN_BLOCK = 1024
N_TOKEN_BLOCKS = T_ALL // TOKEN_BLOCK
N_CTX_BLOCKS = T_CTX // TOKEN_BLOCK
MERGE_BLOCK = 512
ROUTE_BLOCK = 256
EXPERT_BLOCK = 1024
I1_PER_BLOCK = EXPERT_BLOCK // PEER_KEYS
VMEM_LIMIT = 56 * 1024 * 1024
BF16_TILE_ROWS = 16
F32_TILE_ROWS = 8
COND_ROWS = F32_TILE_ROWS
assert 1 + DEC_BATCH <= COND_ROWS

F32 = jnp.float32
BF16 = jnp.bfloat16
NT_DIMS = (((1,), (1,)), ((), ()))


def _params(*semantics):
    return pltpu.CompilerParams(dimension_semantics=semantics, vmem_limit_bytes=VMEM_LIMIT)


def _mod_kernel(c_ref, w_ref, b_ref, o_ref):
    c = c_ref[...]
    s = c * jax.nn.sigmoid(c)
    o_ref[0] = jnp.dot(s, w_ref[0], preferred_element_type=F32,
                       precision=lax.Precision.HIGHEST) + b_ref[0]


MOD_COLS_PER_STEP = 3 * D_MODEL // 2


def _modulation(cvec, w_mod, b_mod):
    tn = MOD_COLS_PER_STEP
    ncol = N_MOD * D_MODEL
    return pl.pallas_call(
        _mod_kernel,
        grid=(DEPTH, ncol // tn),
        in_specs=[pl.BlockSpec((COND_ROWS, D_MODEL), lambda l, n: (0, 0)),
                  pl.BlockSpec((1, D_MODEL, tn), lambda l, n: (l, 0, n)),
                  pl.BlockSpec((1, 1, tn), lambda l, n: (l, 0, n))],
        out_specs=pl.BlockSpec((1, COND_ROWS, tn), lambda l, n: (l, 0, n)),
        out_shape=jax.ShapeDtypeStruct((DEPTH, COND_ROWS, ncol), F32),
        compiler_params=_params("arbitrary", "arbitrary"),
        name="adaln_table",
    )(cvec, w_mod, b_mod.reshape(DEPTH, 1, ncol))


def _rms(x, g):
    return x * lax.rsqrt(jnp.mean(x * x, axis=-1, keepdims=True) + RMS_EPS) * g


def _norm_mod_kernel(x_ref, g_ref, m_ref, o_ref):
    y = _rms(x_ref[...], g_ref[...])
    h = y * (1.0 + m_ref[MOD_SCALE1:MOD_SCALE1 + 1, :]) + m_ref[MOD_SHIFT1:MOD_SHIFT1 + 1, :]
    o_ref[...] = h.astype(BF16)


def _norm_mod(x, g, mod_tok):
    return pl.pallas_call(
        _norm_mod_kernel,
        grid=(N_TOKEN_BLOCKS,),
        in_specs=[pl.BlockSpec((TOKEN_BLOCK, D_MODEL), lambda i: (i, 0)),
                  pl.BlockSpec((1, D_MODEL), lambda i: (0, 0)),
                  pl.BlockSpec((None, N_MOD, D_MODEL), lambda i: (i, 0, 0))],
        out_specs=pl.BlockSpec((TOKEN_BLOCK, D_MODEL), lambda i: (i, 0)),
        out_shape=jax.ShapeDtypeStruct((T_ALL, D_MODEL), BF16),
        compiler_params=_params("arbitrary"),
        name="norm_modulate",
    )(x, g.reshape(1, D_MODEL), mod_tok)


def _in_proj_kernel(h_ref, w_ref, *rest):
    z_ref, k_ref, v_ref, wb_scr = rest[-4:]
    n = pl.program_id(0)
    i = pl.program_id(1)

    @pl.when(i == 0)
    def _():
        wb_scr[...] = w_ref[...].astype(BF16)

    z = jnp.dot(h_ref[...], wb_scr[...], preferred_element_type=F32)
    z_ref[...] = z.astype(BF16)
    per_block = TOKEN_BLOCK // SEQ

    @pl.when((n == COL_K) & (i < N_CTX_BLOCKS))
    def _():
        k_ref[...] = z.reshape(per_block, SEQ, NA_HEADS, NA_HEAD_DIM)

    @pl.when((n == COL_V) & (i < N_CTX_BLOCKS))
    def _():
        v_ref[...] = z.reshape(per_block, SEQ, NA_HEADS, NA_HEAD_DIM)


def _cache_block_index(col, layer, n, i):
    last = N_CTX_BLOCKS - 1
    blk = jnp.where(n < col, 0, jnp.where(n > col, last, jnp.minimum(i, last)))
    return (blk, layer, 0, 0, 0)


def _in_proj(h, w_in, layer, cache_k, cache_v):
    per_block = TOKEN_BLOCK // SEQ
    cache_shape = jax.ShapeDtypeStruct((BATCH, DEPTH, SEQ, NA_HEADS, NA_HEAD_DIM), F32)
    cache_spec = lambda col: pl.BlockSpec(
        (per_block, None, SEQ, NA_HEADS, NA_HEAD_DIM), functools.partial(_cache_block_index, col, layer))
    in_specs = [pl.BlockSpec((TOKEN_BLOCK, D_MODEL), lambda n, i: (i, 0)),
                pl.BlockSpec((None, D_MODEL, D_MODEL), lambda n, i: (layer, 0, n))]
    args = [h, w_in]
    aliases = {}
    if cache_k is not None:
        in_specs += [pl.BlockSpec(memory_space=pl.ANY)] * 2
        args += [cache_k, cache_v]
        aliases = {2: 1, 3: 2}
    return pl.pallas_call(
        _in_proj_kernel,
        grid=(IN_COLS // D_MODEL, N_TOKEN_BLOCKS),
        in_specs=in_specs,
        out_specs=[pl.BlockSpec((TOKEN_BLOCK, D_MODEL), lambda n, i: (i, n)),
                   cache_spec(COL_K), cache_spec(COL_V)],
        out_shape=[jax.ShapeDtypeStruct((T_ALL, IN_COLS), BF16), cache_shape, cache_shape],
        scratch_shapes=[pltpu.VMEM((D_MODEL, D_MODEL), BF16)],
        input_output_aliases=aliases,
        compiler_params=_params("arbitrary", "arbitrary"),
        name="in_proj",
    )(*args)


def _conv_kernel(cb_ref, cc_ref, cx_ref, w_ref, o_ref):
    i = pl.program_id(0)
    u = cc_ref[...].astype(F32) * cx_ref[...].astype(F32)
    seq = jnp.where(i < N_CTX_BLOCKS, SEQ, DEC_SEQ)
    pos = lax.broadcasted_iota(jnp.int32, (TOKEN_BLOCK, 1), 0) & (seq - 1)
    prev = jnp.where(pos == 0, 0.0, pltpu.roll(u, 1, 0))
    nxt = jnp.where(pos == seq - 1, 0.0, pltpu.roll(u, TOKEN_BLOCK - 1, 0))
    y = w_ref[0:1, :] * prev + w_ref[1:2, :] * u + w_ref[2:3, :] * nxt
    o_ref[...] = (cb_ref[...].astype(F32) * y).astype(BF16)


def _short_conv(z, conv_w):
    col = lambda c: pl.BlockSpec((TOKEN_BLOCK, D_MODEL), lambda i: (i, c))
    return pl.pallas_call(
        _conv_kernel,
        grid=(N_TOKEN_BLOCKS,),
        in_specs=[col(COL_CB), col(COL_CC), col(COL_CX),
                  pl.BlockSpec((CONV_K, D_MODEL), lambda i: (0, 0))],
        out_specs=pl.BlockSpec((TOKEN_BLOCK, D_MODEL), lambda i: (i, 0)),
        out_shape=jax.ShapeDtypeStruct((T_ALL, D_MODEL), BF16),
        compiler_params=_params("arbitrary"),
        name="short_conv",
    )(z, z, z, conv_w)


ATT_SCALE = NA_HEAD_DIM ** -0.5


def _ctx_attn_kernel(q_ref, k_ref, v_ref, o_ref):
    for h in range(NA_HEADS):
        sl = slice(h * NA_HEAD_DIM, (h + 1) * NA_HEAD_DIM)
        s = lax.dot_general(q_ref[:, sl], k_ref[:, sl], NT_DIMS,
                            preferred_element_type=F32) * ATT_SCALE
        p = jnp.exp(s - jnp.max(s, axis=-1, keepdims=True))
        o = jnp.dot(p.astype(BF16), v_ref[:, sl], preferred_element_type=F32)
        o_ref[:, sl] = (o / jnp.sum(p, axis=-1, keepdims=True)).astype(BF16)


def _ctx_attention(z):
    col = lambda c: pl.BlockSpec((SEQ, D_MODEL), lambda b: (b, c))
    return pl.pallas_call(
        _ctx_attn_kernel,
        grid=(BATCH,),
        in_specs=[col(COL_Q), col(COL_K), col(COL_V)],
        out_specs=pl.BlockSpec((SEQ, D_MODEL), lambda b: (b, 0)),
        out_shape=jax.ShapeDtypeStruct((T_ALL, D_MODEL), BF16),
        compiler_params=_params("arbitrary"),
        name="ctx_attention",
    )(z, z, z)


NA_GRID_ROWS = DEC_SEQ // GRID_W
NA_WIN_ROWS = min(NA_MAX_ROWS, NA_GRID_ROWS)


def _na_row_groups():
    groups = []
    for r in range(NA_GRID_ROWS):
        r0 = min(max(r - NA_WIN_ROWS // 2, 0), NA_GRID_ROWS - NA_WIN_ROWS)
        if groups and groups[-1][0] == r0:
            groups[-1][1].append(r)
        else:
            groups.append((r0, [r]))
    return groups


def _na_kernel(q_ref, k_ref, v_ref, ck_ref, cv_ref, b_ref, att_in_ref, o_ref):
    del att_in_ref
    pairs = NA_WIN_ROWS // 2
    ck = ck_ref[...].astype(BF16)
    cv = cv_ref[...].astype(BF16)
    for r0, q_rows in _na_row_groups():
        rows = slice(q_rows[0] * GRID_W, (q_rows[-1] + 1) * GRID_W)
        win = slice(r0 * GRID_W, (r0 + NA_WIN_ROWS) * GRID_W)
        q = q_ref[rows, :]
        slab = lambda d: jnp.concatenate([b_ref[d + 2 * m] for m in range(pairs)], axis=1)
        bias = jnp.concatenate([slab(r0 - r + NA_MAX_ROWS - 1) for r in q_rows], axis=0)
        s_loc = lax.dot_general(q, k_ref[win, :], NT_DIMS, preferred_element_type=F32) * ATT_SCALE + bias
        s_ctx = lax.dot_general(q, ck, NT_DIMS, preferred_element_type=F32) * ATT_SCALE
        m = jnp.maximum(jnp.max(s_loc, axis=-1, keepdims=True), jnp.max(s_ctx, axis=-1, keepdims=True))
        p_loc = jnp.exp(s_loc - m)
        p_ctx = jnp.exp(s_ctx - m)
        den = jnp.sum(p_loc, axis=-1, keepdims=True) + jnp.sum(p_ctx, axis=-1, keepdims=True)
        o = (jnp.dot(p_loc.astype(BF16), v_ref[win, :], preferred_element_type=F32)
             + jnp.dot(p_ctx.astype(BF16), cv, preferred_element_type=F32))
        o_ref[rows, :] = (o / den).astype(BF16)


NA_PAIR_TABLES = 2 * NA_MAX_ROWS - 2


def _na_bias(rpb):
    assert NA_WIN_ROWS % 2 == 0
    c = np.arange(GRID_W)
    c0 = np.clip(c - NA_COLS // 2, 0, GRID_W - NA_COLS)
    in_cols = (c[None, :] >= c0[:, None]) & (c[None, :] < c0[:, None] + NA_COLS)
    dc = c[None, :] - c[:, None] + (NA_COLS - 1)
    pick = (dc[None] == np.arange(2 * NA_COLS - 1)[:, None, None]) & in_cols[None]
    t = jnp.einsum('lhdj,jqk->lhdqk', rpb.astype(F32), jnp.asarray(pick, F32), precision=lax.Precision.HIGHEST)
    t = jnp.where(in_cols, t, NEG_INF)
    return jnp.concatenate([t[:, :, :-1], t[:, :, 1:]], axis=-1)


def _na_attention(z, cache_k, cache_v, bias, att, layer):
    lat0 = T_CTX // DEC_SEQ
    col = lambda c: pl.BlockSpec((DEC_SEQ, NA_HEAD_DIM), lambda h, b: (lat0 + b, c * NA_HEADS + h))
    cache = pl.BlockSpec((None, None, PAST_LEN, NA_HEAD_DIM), lambda h, b: (b, layer, 0, h))
    return pl.pallas_call(
        _na_kernel,
        grid=(NA_HEADS, DEC_BATCH),
        in_specs=[col(COL_Q), col(COL_K), col(COL_V), cache, cache,
                  pl.BlockSpec((None, None, NA_PAIR_TABLES, GRID_W, 2 * GRID_W), lambda h, b: (layer, h, 0, 0, 0)),
                  pl.BlockSpec(memory_space=pl.ANY)],
        out_specs=pl.BlockSpec((DEC_SEQ, NA_HEAD_DIM), lambda h, b: (lat0 + b, h)),
        out_shape=jax.ShapeDtypeStruct((T_ALL, D_MODEL), BF16),
        input_output_aliases={6: 0},
        compiler_params=_params("arbitrary", "arbitrary"),
        name="na_attention",
    )(z, z, z, cache_k, cache_v, bias, att)


def _dft_matrices(n):
    j = np.arange(n)
    ang = 2.0 * np.pi * ((j[:, None] * j[None, :]) % n) / n
    return np.cos(ang) / math.sqrt(n), np.sin(ang) / math.sqrt(n)


def _fourier_kernel(cs_ref, ss_ref, f_ref, w2_ref, *rest):
    o_ref = rest[-1]
    f = f_ref[...]
    cu = jnp.dot(cs_ref[...], f, preferred_element_type=F32).astype(BF16)
    su = jnp.dot(ss_ref[...], f, preferred_element_type=F32).astype(BF16)
    for g in range(FNET_GROUPS):
        sl = slice(g * FNET_GROUP_DIM, (g + 1) * FNET_GROUP_DIM)
        lhs = jnp.concatenate([cu[:, sl], su[:, sl]], axis=1)
        o_ref[:, sl] = jnp.dot(lhs, w2_ref[...], preferred_element_type=F32).astype(BF16)


FOURIER_ROWS = 256


def _fourier(z, seq, n_batch, first_row, prev):
    cs, ss = _dft_matrices(seq)
    cc, sc = _dft_matrices(FNET_GROUP_DIM)
    w2 = jnp.asarray(np.concatenate([cc, -sc], axis=0), F32).astype(BF16)
    rows = FOURIER_ROWS
    nr = seq // rows
    in_specs = [pl.BlockSpec((rows, seq), lambda b, r: (r, 0)),
                pl.BlockSpec((rows, seq), lambda b, r: (r, 0)),
                pl.BlockSpec((seq, D_MODEL), lambda b, r: (first_row // seq + b, COL_F)),
                pl.BlockSpec((2 * FNET_GROUP_DIM, FNET_GROUP_DIM), lambda b, r: (0, 0))]
    args = [jnp.asarray(cs, F32).astype(BF16), jnp.asarray(ss, F32).astype(BF16), z, w2]
    aliases = {}
    if prev is not None:
        in_specs.append(pl.BlockSpec(memory_space=pl.ANY))
        args.append(prev)
        aliases = {4: 0}
    return pl.pallas_call(
        _fourier_kernel,
        grid=(n_batch, nr),
        in_specs=in_specs,
        out_specs=pl.BlockSpec((rows, D_MODEL), lambda b, r: (first_row // rows + b * nr + r, 0)),
        out_shape=jax.ShapeDtypeStruct((T_ALL, D_MODEL), BF16),
        input_output_aliases=aliases,
        compiler_params=_params("arbitrary", "arbitrary"),
        name="fourier_mix",
    )(*args)


def _merge_kernel(uc_ref, at_ref, fo_ref, g0_ref, g1_ref, g2_ref, x_ref, wc_ref, wa_ref, wf_ref,
                  wo_ref, m_ref, gf_ref, xo_ref, h2_ref, h2t_ref):
    yc = jnp.dot(uc_ref[...], wc_ref[...], preferred_element_type=F32)
    ya = jnp.dot(at_ref[...], wa_ref[...], preferred_element_type=F32)
    yf = jnp.dot(fo_ref[...], wf_ref[...], preferred_element_type=F32)
    gate = lambda r: jax.nn.sigmoid(r[...].astype(F32))
    merged = gate(g0_ref) * yc + gate(g1_ref) * ya + gate(g2_ref) * yf
    y = jnp.dot(merged.astype(BF16), wo_ref[...], preferred_element_type=F32)
    x = x_ref[...] + m_ref[MOD_GATE1:MOD_GATE1 + 1, :] * y
    xo_ref[...] = x
    hn = _rms(x, gf_ref[...])
    h2 = hn * (1.0 + m_ref[MOD_SCALE2:MOD_SCALE2 + 1, :]) + m_ref[MOD_SHIFT2:MOD_SHIFT2 + 1, :]
    h2_ref[...] = h2.astype(BF16)
    h2t_ref[...] = h2.T.astype(BF16)


def _merge(uconv, att, four, z, x, wc, wa, wf, wo, mod_tok, g_ffn):
    per = TOKEN_BLOCK // MERGE_BLOCK
    row = pl.BlockSpec((MERGE_BLOCK, D_MODEL), lambda i: (i, 0))
    col = lambda c: pl.BlockSpec((MERGE_BLOCK, D_MODEL), lambda i: (i, c))
    wspec = pl.BlockSpec((D_MODEL, D_MODEL), lambda i: (0, 0))
    return pl.pallas_call(
        _merge_kernel,
        grid=(T_ALL // MERGE_BLOCK,),
        in_specs=[row, row, row, col(COL_G0), col(COL_G1), col(COL_G2), row,
                  wspec, wspec, wspec, wspec,
                  pl.BlockSpec((None, N_MOD, D_MODEL), lambda i: (i // per, 0, 0)),
                  pl.BlockSpec((1, D_MODEL), lambda i: (0, 0))],
        out_specs=[row, row, pl.BlockSpec((D_MODEL, MERGE_BLOCK), lambda i: (0, i))],
        out_shape=[jax.ShapeDtypeStruct((T_ALL, D_MODEL), F32),
                   jax.ShapeDtypeStruct((T_ALL, D_MODEL), BF16),
                   jax.ShapeDtypeStruct((D_MODEL, T_ALL), BF16)],
        compiler_params=_params("arbitrary"),
        name="branch_merge",
    )(uconv, att, four, z, z, z, x, wc, wa, wf, wo, mod_tok, g_ffn.reshape(1, D_MODEL))


def _candidate_cells():
    return [(a, b) for a in range(PEER_TOPK) for b in range(PEER_TOPK) if (a + 1) * (b + 1) <= PEER_TOPK]


RANK_CODE_BITS = 0xFF7F0000 - (1 << 32)
LOWEST_SCORE = -3.0e38
RANK_CODE_LIMIT = -3.2e38
NOT_TAKEN = 127
RANK_MASK = 0xFF
SMALL_COUNT_FROM = 4
COUNT_BITS = 2
COUNT_SHIFT = COUNT_BITS.bit_length() - 1
assert 1 << COUNT_SHIFT == COUNT_BITS
assert (SMALL_COUNT_FROM + 1) * (1 << COUNT_BITS) > PEER_TOPK and COUNT_BITS * PEER_TOPK <= 32


def _rank_code(k):
    return float(np.array(RANK_CODE_BITS | k, np.int32).view(np.float32))


def _store_sorted_value(val_scr, p, k, m):
    for h in range(PEER_HEADS):
        val_scr[p, k, h:h + 1, :] = m[h]


def _extract_topk_fast(p, orig_scr, s_scr, rank_scr, val_scr):
    s_scr[...] = jnp.maximum(orig_scr[p], LOWEST_SCORE)
    for k in range(PEER_TOPK):
        cur = s_scr[...]
        m = jnp.max(cur, axis=1, keepdims=True)
        s_scr[...] = jnp.where(cur == m, _rank_code(k), cur)
        _store_sorted_value(val_scr, p, k, m)
    coded = s_scr[...]
    taken = coded < RANK_CODE_LIMIT
    rank_scr[p] = jnp.where(taken, pltpu.bitcast(coded, jnp.int32) & RANK_MASK, NOT_TAKEN)
    count = jnp.sum(jnp.where(taken, 1.0, 0.0), axis=1)
    clamped = jnp.where(val_scr[p, PEER_TOPK - 1] <= LOWEST_SCORE, 1.0, 0.0)
    return jnp.max(jnp.abs(count - float(PEER_TOPK)) + clamped)


def _extract_topk_ties(p, orig_scr, s_scr, rank_scr, val_scr):
    shape = (PEER_HEADS, PEER_KEYS, ROUTE_BLOCK)
    key_iota = lax.broadcasted_iota(jnp.int32, shape, 1)
    s_scr[...] = orig_scr[p]
    rank_scr[p] = jnp.full(shape, NOT_TAKEN, jnp.int32)
    for k in range(PEER_TOPK):
        cur = s_scr[...]
        m = jnp.max(cur, axis=1, keepdims=True)
        first = jnp.min(jnp.where(cur == m, key_iota, PEER_KEYS), axis=1, keepdims=True)
        sel = key_iota == first
        s_scr[...] = jnp.where(sel, -jnp.inf, cur)
        rank_scr[p] = jnp.where(sel, k, rank_scr[p])
        _store_sorted_value(val_scr, p, k, m)


def _route_kernel(h_ref, wpq_ref, keys_ref, r2_ref, e2_ref, n1_ref, c1_ref,
                  s_scr, orig_scr, rank_scr, val_scr, n_scr, small_scr):
    tb = ROUTE_BLOCK
    q = jnp.dot(h_ref[...], wpq_ref[...], preferred_element_type=F32).astype(BF16)

    for p in range(2):
        piece = lambda h: slice((2 * h + p) * PEER_KEY_DIM, (2 * h + p + 1) * PEER_KEY_DIM)
        qp = jnp.concatenate([q[:, piece(h)] for h in range(PEER_HEADS)], axis=1)
        s = lax.dot_general(keys_ref[p], qp, NT_DIMS, preferred_element_type=F32)
        orig_scr[p] = s.reshape(PEER_HEADS, PEER_KEYS, tb)

    off = 0.0
    for p in range(2):
        off = jnp.maximum(off, _extract_topk_fast(p, orig_scr, s_scr, rank_scr, val_scr))

    @pl.when(off > 0.0)
    def _():
        for p in range(2):
            _extract_topk_ties(p, orig_scr, s_scr, rank_scr, val_scr)

    v1 = [val_scr[0, a] for a in range(PEER_TOPK)]
    v2 = [val_scr[1, b] for b in range(PEER_TOPK)]
    cells = _candidate_cells()
    sums = {c: v1[c[0]] + v2[c[1]] for c in cells}
    undecided = lambda c, d: not (d[0] >= c[0] and d[1] >= c[1])
    before = {c: float((c[0] + 1) * (c[1] + 1) - 1 + sum(undecided(c, d) for d in cells[ci + 1:]))
              for ci, c in enumerate(cells)}
    for ci, c in enumerate(cells):
        for d in cells[ci + 1:]:
            if undecided(c, d):
                first = jnp.where(sums[c] >= sums[d], 1.0, 0.0)
                before[d] = before[d] + first
                before[c] = before[c] - first
    e1 = [jnp.exp(v1[a] - v1[0]) for a in range(PEER_TOPK)]
    e2 = [jnp.exp(v2[b] - v2[0]) for b in range(PEER_TOPK)]
    zsum = 0.0
    count = [0.0] * PEER_TOPK
    for c in cells:
        chosen = before[c] < float(PEER_TOPK)
        count[c[0]] = count[c[0]] + jnp.where(chosen, 1.0, 0.0)
        zsum = zsum + jnp.where(chosen, e1[c[0]] * e2[c[1]], 0.0)
    small = jnp.zeros((PEER_HEADS, tb), jnp.int32)
    for a in range(PEER_TOPK):
        if a < SMALL_COUNT_FROM:
            n_scr[a] = count[a]
        else:
            small = small | (count[a].astype(jnp.int32) << (COUNT_BITS * a))
    small_scr[...] = small
    n_scr[PEER_TOPK] = 0.5 / zsum

    for h in range(PEER_HEADS):
        row = pl.ds(h, 1)
        rank1 = rank_scr[0, h]
        taken1 = rank1 < PEER_TOPK
        r = jnp.minimum(rank1, PEER_TOPK - 1)
        packed = jnp.broadcast_to(small_scr[row, :], r.shape)
        n1 = (lax.shift_right_logical(packed, r << COUNT_SHIFT) & ((1 << COUNT_BITS) - 1)).astype(F32)
        for a in reversed(range(SMALL_COUNT_FROM)):
            n1 = jnp.where(r == a, n_scr[a, row, :], n1)
        n1_ref[h] = jnp.where(taken1, n1, 0.0)
        w1 = jnp.where(taken1, jnp.exp(orig_scr[0, h] - val_scr[0, 0, row, :]), 0.0)
        c1_ref[h] = w1 * n_scr[PEER_TOPK, row, :]
        rank2 = rank_scr[1, h]
        r2_ref[h] = rank2.astype(F32).astype(BF16)
        w2 = jnp.where(rank2 < PEER_TOPK, jnp.exp(orig_scr[1, h] - val_scr[1, 0, row, :]), 0.0)
        e2_ref[h] = w2.astype(BF16)


def _route(h2, wpq, keys, layer):
    tb = ROUTE_BLOCK
    out = lambda dt: jax.ShapeDtypeStruct((PEER_HEADS, PEER_KEYS, T_ALL), dt)
    ospec = pl.BlockSpec((PEER_HEADS, PEER_KEYS, tb), lambda i: (0, 0, i))
    hk = PEER_HEADS * PEER_KEYS
    return pl.pallas_call(
        _route_kernel,
        grid=(T_ALL // tb,),
        in_specs=[pl.BlockSpec((tb, D_MODEL), lambda i: (i, 0)),
                  pl.BlockSpec((None, D_MODEL, 2 * hk), lambda i: (layer, 0, 0)),
                  pl.BlockSpec((None, 2, hk, hk), lambda i: (layer, 0, 0, 0))],
        out_specs=[ospec, ospec, ospec, ospec],
        out_shape=[out(BF16), out(BF16), out(F32), out(F32)],
        scratch_shapes=[pltpu.VMEM((PEER_HEADS, PEER_KEYS, tb), F32),
                        pltpu.VMEM((2, PEER_HEADS, PEER_KEYS, tb), F32),
                        pltpu.VMEM((2, PEER_HEADS, PEER_KEYS, tb), jnp.int32),
                        pltpu.VMEM((2, PEER_TOPK, PEER_HEADS, tb), F32),
                        pltpu.VMEM((PEER_TOPK + 1, PEER_HEADS, tb), F32),
                        pltpu.VMEM((PEER_HEADS, tb), jnp.int32)],
        compiler_params=_params("arbitrary"),
        name="peer_route",
    )(h2, wpq, keys)


SQRT_HALF = math.sqrt(0.5)


EXPERT_TOKEN_CHUNK = 1024
N_EXPERT_CHUNKS = TOKEN_BLOCK // EXPERT_TOKEN_CHUNK
ACT_SLOTS = min(2, N_EXPERT_CHUNKS)


def _experts_kernel(xt_ref, u_ref, vt_ref, r2_ref, e2_ref, n1_ref, c1_ref, res_ref, m_ref, g_ref, mn_ref,
                    o1_ref, o2_ref, acc_scr, act_scr, p_scr, *, last_layer):
    eb = pl.program_id(1)

    @pl.when(eb == 0)
    def _():
        acc_scr[...] = jnp.zeros_like(acc_scr)

    cols = lambda j: slice(j * EXPERT_TOKEN_CHUNK, (j + 1) * EXPERT_TOKEN_CHUNK)
    def pre_activate(j):
        act_scr[j % ACT_SLOTS] = jnp.dot(u_ref[...].astype(BF16), xt_ref[:, cols(j)], preferred_element_type=F32)

    def activate(j):
        for c in range(I1_PER_BLOCK):
            i1 = pl.ds(eb * I1_PER_BLOCK + c, 1)
            tiles = (PEER_KEYS // BF16_TILE_ROWS, BF16_TILE_ROWS, EXPERT_TOKEN_CHUNK)
            tile_row = lambda ref, h: jnp.broadcast_to(ref[h, i1, cols(j)], tiles[1:]).astype(BF16)[None]
            gate = jnp.zeros(tiles, BF16)
            for h in range(PEER_HEADS):
                taken = r2_ref[h, :, cols(j)].reshape(tiles) < tile_row(n1_ref, h)
                e2 = e2_ref[h, :, cols(j)].reshape(tiles)
                gate = gate + jnp.where(taken, e2, jnp.zeros((), BF16)) * tile_row(c1_ref, h)
            keys = slice(c * PEER_KEYS, (c + 1) * PEER_KEYS)
            a = act_scr[j % ACT_SLOTS, keys, :]
            erf1 = 1.0 + lax.erf(a * SQRT_HALF)
            p_scr[keys, cols(j)] = (a.astype(BF16) * erf1.astype(BF16)
                                    * gate.reshape(PEER_KEYS, EXPERT_TOKEN_CHUNK))

    def mix(j):
        acc_scr[:, cols(j)] += jnp.dot(vt_ref[...], p_scr[:, cols(j)], preferred_element_type=F32)

    for stage in range(N_EXPERT_CHUNKS + 2):
        if stage < N_EXPERT_CHUNKS:
            pre_activate(stage)
        if 1 <= stage <= N_EXPERT_CHUNKS:
            activate(stage - 1)
        if stage >= 2:
            mix(stage - 2)

    @pl.when(eb == pl.num_programs(1) - 1)
    def _():
        x = res_ref[...] + m_ref[MOD_GATE2:MOD_GATE2 + 1, :] * acc_scr[...].T
        y = _rms(x, g_ref[...])
        if last_layer:
            t = pl.program_id(0)

            @pl.when(t < N_CTX_BLOCKS)
            def _():
                o1_ref[...] = y

            @pl.when(t >= N_CTX_BLOCKS)
            def _():
                o2_ref[...] = y
        else:
            o1_ref[...] = x
            h = y * (1.0 + mn_ref[MOD_SCALE1:MOD_SCALE1 + 1, :]) + mn_ref[MOD_SHIFT1:MOD_SHIFT1 + 1, :]
            o2_ref[...] = h.astype(BF16)


def _experts(h2t, u_bf, vt_bf, r2, e2, n1, c1, x_mid, mod_tok, g_next, mod_next, layer):
    once = pl.Buffered(1)
    last_layer = layer == DEPTH - 1
    tok = pl.BlockSpec((PEER_HEADS, PEER_KEYS, TOKEN_BLOCK), lambda t, e: (0, 0, t), pipeline_mode=once)
    block = (TOKEN_BLOCK, D_MODEL)
    if last_layer:
        out_specs = [pl.BlockSpec(block, lambda t, e: (jnp.minimum(t, N_CTX_BLOCKS - 1), 0), pipeline_mode=once),
                     pl.BlockSpec(block, lambda t, e: (jnp.maximum(t - N_CTX_BLOCKS, 0), 0), pipeline_mode=once)]
        out_shape = [jax.ShapeDtypeStruct((T_CTX, D_MODEL), F32), jax.ShapeDtypeStruct((T_LAT, D_MODEL), F32)]
    else:
        out_specs = [pl.BlockSpec(block, lambda t, e: (t, 0))] * 2
        out_shape = [jax.ShapeDtypeStruct((T_ALL, D_MODEL), F32), jax.ShapeDtypeStruct((T_ALL, D_MODEL), BF16)]
    return pl.pallas_call(
        functools.partial(_experts_kernel, last_layer=last_layer),
        grid=(N_TOKEN_BLOCKS, PEER_EXPERTS // EXPERT_BLOCK),
        in_specs=[pl.BlockSpec((D_MODEL, TOKEN_BLOCK), lambda t, e: (0, t)),
                  pl.BlockSpec((None, EXPERT_BLOCK, D_MODEL), lambda t, e: (layer, e, 0)),
                  pl.BlockSpec((None, D_MODEL, EXPERT_BLOCK), lambda t, e: (layer, 0, e)),
                  tok, tok, tok, tok,
                  pl.BlockSpec((TOKEN_BLOCK, D_MODEL), lambda t, e: (t, 0), pipeline_mode=once),
                  pl.BlockSpec((None, N_MOD, D_MODEL), lambda t, e: (t, 0, 0)),
                  pl.BlockSpec((1, D_MODEL), lambda t, e: (0, 0)),
                  pl.BlockSpec((None, N_MOD, D_MODEL), lambda t, e: (t, 0, 0))],
        out_specs=out_specs,
        out_shape=out_shape,
        scratch_shapes=[pltpu.VMEM((D_MODEL, TOKEN_BLOCK), F32),
                        pltpu.VMEM((ACT_SLOTS, EXPERT_BLOCK, EXPERT_TOKEN_CHUNK), F32),
                        pltpu.VMEM((EXPERT_BLOCK, TOKEN_BLOCK), BF16)],
        compiler_params=_params("arbitrary", "arbitrary"),
        name="peer_experts",
    )(h2t, u_bf, vt_bf, r2, e2, n1, c1, x_mid, mod_tok, g_next.reshape(1, D_MODEL), mod_next)


TRANSPOSE_ROWS = 2048


def _transpose_cast_kernel(v_ref, o_ref):
    o_ref[...] = v_ref[...].T.astype(BF16)


def _transpose_cast(peer_v):
    return pl.pallas_call(
        _transpose_cast_kernel,
        grid=(DEPTH, PEER_EXPERTS // TRANSPOSE_ROWS),
        in_specs=[pl.BlockSpec((None, TRANSPOSE_ROWS, D_MODEL), lambda l, e: (l, e, 0))],
        out_specs=pl.BlockSpec((None, D_MODEL, TRANSPOSE_ROWS), lambda l, e: (l, 0, e)),
        out_shape=jax.ShapeDtypeStruct((DEPTH, D_MODEL, PEER_EXPERTS), BF16),
        compiler_params=_params("arbitrary", "arbitrary"),
        name="expert_table_transpose",
    )(peer_v)


def _block_diag_keys(sub_keys):
    eye = jnp.eye(PEER_HEADS, dtype=sub_keys.dtype)
    keys = jnp.einsum('lhpjd,hg->lphjgd', sub_keys, eye)
    return keys.reshape(DEPTH, 2, PEER_HEADS * PEER_KEYS, PEER_HEADS * PEER_KEY_DIM).astype(BF16)


def kernel(x_prompt, x_sample, cache_k, cache_v, c, c_ctx, w_in, conv_w, w_conv_out, rpb, w_attn_out,
           w_four_out, w_o, g_mix, g_ffn, w_mod, b_mod, w_pq, sub_keys, peer_u, peer_v, g_final):
    assert SEQ & (SEQ - 1) == 0 and DEC_SEQ & (DEC_SEQ - 1) == 0
    x = jnp.concatenate([x_prompt.reshape(T_CTX, D_MODEL), x_sample.reshape(T_LAT, D_MODEL)], axis=0)
    cvec = jnp.zeros((COND_ROWS, D_MODEL), F32).at[0].set(c_ctx).at[1:1 + DEC_BATCH].set(c)
    mod = _modulation(cvec, w_mod, b_mod)
    lat_per_block = DEC_SEQ // TOKEN_BLOCK
    block_row = np.array([0] * N_CTX_BLOCKS
                         + [1 + b for b in range(DEC_BATCH) for _ in range(lat_per_block)])
    mod_tok = mod[:, block_row].reshape(DEPTH, N_TOKEN_BLOCKS, N_MOD, D_MODEL)
    cache_k4 = cache_k.reshape(DEC_BATCH, DEPTH, PAST_LEN, D_MODEL)
    cache_v4 = cache_v.reshape(DEC_BATCH, DEPTH, PAST_LEN, D_MODEL)

    na_bias = _na_bias(rpb)
    keys = _block_diag_keys(sub_keys)
    w_pq_bf = w_pq.astype(BF16)
    u_bf = peer_u
    vt_bf = _transpose_cast(peer_v)

    new_k = new_v = None
    h = _norm_mod(x, g_mix[0], mod_tok[0])
    for l in range(DEPTH):
        z, new_k, new_v = _in_proj(h, w_in, l, new_k, new_v)
        uconv = _short_conv(z, conv_w[l])
        att = _ctx_attention(z)
        att = _na_attention(z, cache_k4, cache_v4, na_bias, att, l)
        four = _fourier(z, SEQ, BATCH, 0, None)
        four = _fourier(z, DEC_SEQ, DEC_BATCH, T_CTX, four)
        x_mid, h2, h2t = _merge(uconv, att, four, z, x,
                           w_conv_out[l].astype(BF16), w_attn_out[l].astype(BF16),
                           w_four_out[l].astype(BF16), w_o[l].astype(BF16), mod_tok[l], g_ffn[l])
        r2, e2, n1, c1 = _route(h2, w_pq_bf, keys, l)
        if l + 1 < DEPTH:
            x, h = _experts(h2t, u_bf, vt_bf, r2, e2, n1, c1, x_mid, mod_tok[l], g_mix[l + 1], mod_tok[l + 1], l)
        else:
            y_ctx, y_lat = _experts(h2t, u_bf, vt_bf, r2, e2, n1, c1, x_mid, mod_tok[l], g_final, mod_tok[l], l)

    y_prompt = y_ctx.reshape(BATCH, SEQ, D_MODEL)
    y_sample = y_lat.reshape(DEC_BATCH, DEC_SEQ, D_MODEL)
    return (y_prompt, y_sample, new_k, new_v)
```
